```python
import math
import jax, jax.numpy as jnp
from jax import lax
import numpy as np

D_MODEL = 2048
BATCH = 8
SEQ = 2048
DEPTH = 1
DEC_BATCH = 128
DEC_SEQ = 8
PAST_LEN = 8192
PAGE_SIZE = 128

HEAD_DIM = 64
A_HEADS = 16
A_PATTERNS = ((128, 1), (512, 4), (2048, 16))
A_WIN = max(w for w, _ in A_PATTERNS)
B_HEADS = 16
B_KV_HEADS = 2
B_GROUP = B_HEADS // B_KV_HEADS
B_WIN = 128
A_WIDTH = A_HEADS * HEAD_DIM
B_WIDTH = B_HEADS * HEAD_DIM
B_KV_WIDTH = B_KV_HEADS * HEAD_DIM
MIX_WIDTH = A_WIDTH + B_WIDTH
IN_WIDTH = 3 * A_WIDTH + B_WIDTH + 2 * B_KV_WIDTH
ATTN_SCALE = HEAD_DIM ** -0.5
N_BUCKETS = 32
MAX_EXACT = N_BUCKETS // 2
MAX_DISTANCE = A_WIN
MEM_LEN = 256
MEM_HEADS = 4
MEM_HEAD_DIM = 128
MEM_WIDTH = MEM_HEADS * MEM_HEAD_DIM
PEER_HEADS = 8
PEER_NKEYS = 128
PEER_EXPERTS = PEER_NKEYS * PEER_NKEYS
PEER_DKEY = 256
PEER_TOPK = 16
TOKEN_BLOCK = 128
BLK = 128
EPS = 1e-6

kernel_name = "hybrid_dilated_swa_peer_decoder_step"


def _rmsnorm(x, g):
    xf = x.astype(jnp.float32)
    y = xf * lax.rsqrt(jnp.mean(xf * xf, axis=-1, keepdims=True) + EPS)
    return (y * g.astype(jnp.float32)).astype(x.dtype)


def _rel_bucket(dist):
    dist = np.maximum(np.asarray(dist), 0)
    ratio = np.log(np.maximum(dist, 1) / MAX_EXACT) / math.log(MAX_DISTANCE / MAX_EXACT)
    large = np.minimum(MAX_EXACT + (ratio * (N_BUCKETS - MAX_EXACT)).astype(np.int32), N_BUCKETS - 1)
    return np.where(dist < MAX_EXACT, dist, large).astype(np.int32)


def _pos_bias(rel_bias, dist, head_lo, hk, g):
    b = rel_bias[_rel_bucket(dist)][..., head_lo:head_lo + hk * g]
    return b.reshape(*dist.shape, hk, g).astype(jnp.float32)


def _softmax_stats(s, sink):
    m = jnp.max(s, axis=-1)
    if sink is not None:
        m = jnp.maximum(m, sink)
    e = jnp.exp(s - m[..., None])
    den = jnp.sum(e, axis=-1)
    if sink is not None:
        den = den + jnp.exp(sink - m)
    return e / den[..., None], m + jnp.log(den)


def _banded_attention(q, k, v, n_back, stride, head_lo, rel_bias, sink):
    n, l, hk, g, dh = q.shape
    nb = -(-l // BLK)
    lp = nb * BLK
    kb = BLK + n_back
    qb = jnp.pad(q, ((0, 0), (0, lp - l), (0, 0), (0, 0), (0, 0))).reshape(n, nb, BLK, hk, g, dh)
    pad_kv = ((0, 0), (n_back, lp - l), (0, 0), (0, 0))
    kidx = np.arange(nb)[:, None] * BLK + np.arange(kb)[None, :]
    kblk = jnp.pad(k, pad_kv)[:, kidx]
    vblk = jnp.pad(v, pad_kv)[:, kidx]
    rel = np.arange(BLK)[:, None] - np.arange(kb)[None, :] + n_back
    qpos = np.arange(nb)[:, None, None] * BLK + np.arange(BLK)[None, :, None]
    kpos = qpos - rel[None]
    valid = (kpos >= 0) & (rel[None] >= 0) & (rel[None] <= n_back)
    bias = _pos_bias(rel_bias, rel * stride, head_lo, hk, g).transpose(2, 3, 0, 1)
    s = jnp.einsum('nbqhgd,nbkhd->nbhgqk', qb, kblk).astype(jnp.float32) * ATTN_SCALE + bias
    s = jnp.where(valid[None, :, None, None], s, -jnp.inf)
    p, lse = _softmax_stats(s, None if sink is None else sink[None, None, :, :, None])
    o = jnp.einsum('nbhgqk,nbkhd->nbqhgd', p.astype(v.dtype), vblk).reshape(n, lp, hk, g, dh)[:, :l]
    lse = lse.transpose(0, 1, 4, 2, 3).reshape(n, lp, hk, g)[:, :l]
    return o, lse


def _gathered_attention(q, kseq, vseq, n_back, stride, head_lo, rel_bias, sink):
    n, t, hk, g, dh = q.shape
    m = kseq.shape[1]
    offs = np.arange(n_back + 1)
    idx = (m - t) + np.arange(t)[:, None] - stride * offs[None, :]
    valid = idx >= 0
    idx = np.maximum(idx, 0)
    kg = kseq[:, idx]
    vg = vseq[:, idx]
    bias = _pos_bias(rel_bias, offs * stride, head_lo, hk, g).transpose(1, 2, 0)
    s = jnp.einsum('nthgd,ntjhd->nhgtj', q, kg).astype(jnp.float32) * ATTN_SCALE + bias[:, :, None, :]
    s = jnp.where(valid[None, None, None], s, -jnp.inf)
    p, lse = _softmax_stats(s, None if sink is None else sink[None, :, :, None])
    o = jnp.einsum('nhgtj,ntjhd->nthgd', p.astype(vseq.dtype), vg)
    return o, lse.transpose(0, 3, 1, 2)


def _combine_patterns(outs, lses):
    w = jax.nn.softmax(jnp.stack(lses), axis=0)
    return jnp.einsum('pnlh,pnlhd->nlhd', w.astype(outs[0].dtype), jnp.stack(outs))


def _dilated_prompt(q, k, v, rel_bias):
    b, s, h, dh = q.shape
    outs, lses = [], []
    for w, d in A_PATTERNS:
        def to_stream(x):
            return x.reshape(b, s // d, d, h, dh).swapaxes(1, 2).reshape(b * d, s // d, h, dh)
        o, lse = _banded_attention(to_stream(q)[:, :, :, None], to_stream(k), to_stream(v),
                                   w // d, d, 0, rel_bias, None)
        outs.append(o.reshape(b, d, s // d, h, dh).swapaxes(1, 2).reshape(b, s, h, dh))
        lses.append(lse.reshape(b, d, s // d, h).swapaxes(1, 2).reshape(b, s, h))
    return _combine_patterns(outs, lses)


def _dilated_sample(q, kseq, vseq, rel_bias):
    outs, lses = [], []
    for w, d in A_PATTERNS:
        o, lse = _gathered_attention(q[:, :, :, None], kseq, vseq, w // d, d, 0, rel_bias, None)
        outs.append(o[:, :, :, 0])
        lses.append(lse[..., 0])
    return _combine_patterns(outs, lses)


def _project(h, w_in):
    lead = h.shape[:-1]
    cuts = [A_WIDTH, 2 * A_WIDTH, 3 * A_WIDTH, 3 * A_WIDTH + B_WIDTH, 3 * A_WIDTH + B_WIDTH + B_KV_WIDTH]
    qa, ka, va, qb, kb, vb = jnp.split(h @ w_in, cuts, axis=-1)
    a = [t.reshape(*lead, A_HEADS, HEAD_DIM) for t in (qa, ka, va)]
    qb = qb.reshape(*lead, B_KV_HEADS, B_GROUP, HEAD_DIM)
    kb = kb.reshape(*lead, B_KV_HEADS, HEAD_DIM)
    vb = vb.reshape(*lead, B_KV_HEADS, HEAD_DIM)
    return a[0], a[1], a[2], qb, kb, vb


def _merge(oa, ob, ln_a, ln_b, w_out):
    lead = oa.shape[:-2]
    oa = _rmsnorm(oa.reshape(*lead, A_WIDTH), ln_a)
    ob = _rmsnorm(ob.reshape(*lead, B_WIDTH), ln_b)
    return jnp.concatenate([oa, ob], axis=-1) @ w_out


def _mem_kv(mem, ln_mem, w_ckv):
    b, m, _ = mem.shape
    kv = _rmsnorm(mem, ln_mem) @ w_ckv
    k, v = jnp.split(kv, 2, axis=-1)
    return k.reshape(b, m, MEM_HEADS, MEM_HEAD_DIM), v.reshape(b, m, MEM_HEADS, MEM_HEAD_DIM)


def _cross_attn(h, mk, mv, w_cq, w_co):
    b, s, _ = h.shape
    q = (h @ w_cq).reshape(b, s, MEM_HEADS, MEM_HEAD_DIM)
    sc = jnp.einsum('bshd,bmhd->bhsm', q, mk).astype(jnp.float32) * (MEM_HEAD_DIM ** -0.5)
    p = jax.nn.softmax(sc, axis=-1).astype(mv.dtype)
    o = jnp.einsum('bhsm,bmhd->bshd', p, mv).reshape(b, s, MEM_WIDTH)
    return o @ w_co


def _peer(h, w_pq, sub_k1, sub_k2, expert_u, expert_v):
    shape = h.shape
    flat = h.reshape(-1, D_MODEL)
    n = flat.shape[0]
    npad = -(-n // TOKEN_BLOCK) * TOKEN_BLOCK
    blocks = jnp.pad(flat, ((0, npad - n), (0, 0))).reshape(-1, TOKEN_BLOCK, D_MODEL)
    half = PEER_DKEY // 2

    def one_block(xb):
        q = (xb @ w_pq).reshape(TOKEN_BLOCK, PEER_HEADS, PEER_DKEY)
        s1 = jnp.einsum('thd,kd->thk', q[..., :half], sub_k1)
        s2 = jnp.einsum('thd,kd->thk', q[..., half:], sub_k2)
        v1, i1 = lax.top_k(s1, PEER_TOPK)
        v2, i2 = lax.top_k(s2, PEER_TOPK)
        cand = (v1[..., :, None] + v2[..., None, :]).reshape(TOKEN_BLOCK, PEER_HEADS, PEER_TOPK * PEER_TOPK)
        cidx = (i1[..., :, None] * PEER_NKEYS + i2[..., None, :]).reshape(TOKEN_BLOCK, PEER_HEADS, PEER_TOPK * PEER_TOPK)
        best, pos = lax.top_k(cand, PEER_TOPK)
        eidx = jnp.take_along_axis(cidx, pos, axis=-1)
        gate = jax.nn.softmax(best.astype(jnp.float32), axis=-1)
        u = expert_u[eidx]
        act = jax.nn.gelu(jnp.einsum('td,thkd->thk', xb, u).astype(jnp.float32), approximate=False)
        vv = expert_v[eidx]
        return jnp.einsum('thk,thkd->td', (gate * act).astype(xb.dtype), vv)

    out = lax.map(one_block, blocks).reshape(npad, D_MODEL)[:n]
    return out.reshape(shape)


def setup_inputs(seed: int = 0) -> dict:
    key = jax.random.key(seed)
    ks = jax.random.split(key, 32)
    f32 = jnp.float32

    def nrm(k, shape, scale):
        return scale * jax.random.normal(k, shape, f32)

    def gain(k, shape):
        return 1.0 + 0.02 * jax.random.normal(k, shape, f32)

    la = min(A_WIN, PAST_LEN)
    lb = min(B_WIN, PAST_LEN)
    return {
        "x_prompt": nrm(ks[0], (BATCH, SEQ, D_MODEL), 1.0),
        "x_sample": nrm(ks[1], (DEC_BATCH, DEC_SEQ, D_MODEL), 1.0),
        "cache_a_k": nrm(ks[2], (DEPTH, DEC_BATCH, la, A_HEADS, HEAD_DIM), 1.0),
        "cache_a_v": nrm(ks[3], (DEPTH, DEC_BATCH, la, A_HEADS, HEAD_DIM), 1.0),
        "cache_b_k": nrm(ks[4], (DEPTH, DEC_BATCH, lb, B_KV_HEADS, HEAD_DIM), 1.0),
        "cache_b_v": nrm(ks[5], (DEPTH, DEC_BATCH, lb, B_KV_HEADS, HEAD_DIM), 1.0),
        "cache_mem_k": nrm(ks[6], (DEPTH, DEC_BATCH, MEM_LEN, MEM_HEADS, MEM_HEAD_DIM), 1.0),
        "cache_mem_v": nrm(ks[7], (DEPTH, DEC_BATCH, MEM_LEN, MEM_HEADS, MEM_HEAD_DIM), 1.0),
        "mem_prompt": nrm(ks[8], (BATCH, MEM_LEN, D_MODEL), 1.0),
        "ln_mix": gain(ks[9], (DEPTH, D_MODEL)),
        "w_in": nrm(ks[10], (DEPTH, D_MODEL, IN_WIDTH), D_MODEL ** -0.5),
        "ln_a_out": gain(ks[11], (DEPTH, A_WIDTH)),
        "ln_b_out": gain(ks[12], (DEPTH, B_WIDTH)),
        "w_out": nrm(ks[13], (DEPTH, MIX_WIDTH, D_MODEL), MIX_WIDTH ** -0.5),
        "b_sinks": nrm(ks[14], (DEPTH, B_HEADS), 1.0),
        "rel_bias": nrm(ks[15], (N_BUCKETS, A_HEADS + B_HEADS), 0.5),
        "ln_cross": gain(ks[16], (DEPTH, D_MODEL)),
        "ln_mem": gain(ks[17], (DEPTH, D_MODEL)),
        "w_cq": nrm(ks[18], (DEPTH, D_MODEL, MEM_WIDTH), D_MODEL ** -0.5),
        "w_ckv": nrm(ks[19], (DEPTH, D_MODEL, 2 * MEM_WIDTH), D_MODEL ** -0.5),
        "w_co": nrm(ks[20], (DEPTH, MEM_WIDTH, D_MODEL), MEM_WIDTH ** -0.5),
        "ln_ffn": gain(ks[21], (DEPTH, D_MODEL)),
        "w_pq": nrm(ks[22], (DEPTH, D_MODEL, PEER_HEADS * PEER_DKEY), D_MODEL ** -0.5),
        "sub_keys_1": nrm(ks[23], (DEPTH, PEER_NKEYS, PEER_DKEY // 2), (PEER_DKEY // 2) ** -0.5),
        "sub_keys_2": nrm(ks[24], (DEPTH, PEER_NKEYS, PEER_DKEY // 2), (PEER_DKEY // 2) ** -0.5),
        "expert_u": nrm(ks[25], (DEPTH, PEER_EXPERTS, D_MODEL), D_MODEL ** -0.5),
        "expert_v": nrm(ks[26], (DEPTH, PEER_EXPERTS, D_MODEL), PEER_TOPK ** -0.5),
        "ln_final": gain(ks[27], (D_MODEL,)),
    }


def reference(x_prompt, x_sample, cache_a_k, cache_a_v, cache_b_k, cache_b_v, cache_mem_k, cache_mem_v,
              mem_prompt, ln_mix, w_in, ln_a_out, ln_b_out, w_out, b_sinks, rel_bias, ln_cross, ln_mem,
              w_cq, w_ckv, w_co, ln_ffn, w_pq, sub_keys_1, sub_keys_2, expert_u, expert_v, ln_final):
    yp = x_prompt
    ys = x_sample
    akp, avp, bkp, bvp, mkp, mvp = [], [], [], [], [], []
    aks, avs, bks, bvs = [], [], [], []
    for l in range(DEPTH):
        sinks = b_sinks[l].astype(jnp.float32).reshape(B_KV_HEADS, B_GROUP)
        h = _rmsnorm(yp, ln_mix[l])
        qa, ka, va, qb, kb, vb = _project(h, w_in[l])
        oa = _dilated_prompt(qa, ka, va, rel_bias)
        ob, _ = _banded_attention(qb, kb, vb, B_WIN, 1, A_HEADS, rel_bias, sinks)
        yp = yp + _merge(oa, ob, ln_a_out[l], ln_b_out[l], w_out[l])
        mk, mv = _mem_kv(mem_prompt, ln_mem[l], w_ckv[l])
        yp = yp + _cross_attn(_rmsnorm(yp, ln_cross[l]), mk, mv, w_cq[l], w_co[l])
        yp = yp + _peer(_rmsnorm(yp, ln_ffn[l]), w_pq[l], sub_keys_1[l], sub_keys_2[l], expert_u[l], expert_v[l])
        s_len = ka.shape[1]
        akp.append(ka[:, s_len - min(A_WIN, s_len):])
        avp.append(va[:, s_len - min(A_WIN, s_len):])
        bkp.append(kb[:, s_len - min(B_WIN, s_len):])
        bvp.append(vb[:, s_len - min(B_WIN, s_len):])
        mkp.append(mk)
        mvp.append(mv)
        h = _rmsnorm(ys, ln_mix[l])
        qa, ka, va, qb, kb, vb = _project(h, w_in[l])
        ka_seq = jnp.concatenate([cache_a_k[l], ka], axis=1)
        va_seq = jnp.concatenate([cache_a_v[l], va], axis=1)
        kb_seq = jnp.concatenate([cache_b_k[l], kb], axis=1)
        vb_seq = jnp.concatenate([cache_b_v[l], vb], axis=1)
        oa = _dilated_sample(qa, ka_seq, va_seq, rel_bias)
        ob, _ = _gathered_attention(qb, kb_seq, vb_seq, B_WIN, 1, A_HEADS, rel_bias, sinks)
        ys = ys + _merge(oa, ob, ln_a_out[l], ln_b_out[l], w_out[l])
        ys = ys + _cross_attn(_rmsnorm(ys, ln_cross[l]), cache_mem_k[l], cache_mem_v[l], w_cq[l], w_co[l])
        ys = ys + _peer(_rmsnorm(ys, ln_ffn[l]), w_pq[l], sub_keys_1[l], sub_keys_2[l], expert_u[l], expert_v[l])
        ma = ka_seq.shape[1]
        mb = kb_seq.shape[1]
        aks.append(ka_seq[:, ma - min(A_WIN, ma):])
        avs.append(va_seq[:, ma - min(A_WIN, ma):])
        bks.append(kb_seq[:, mb - min(B_WIN, mb):])
        bvs.append(vb_seq[:, mb - min(B_WIN, mb):])
    y_prompt = _rmsnorm(yp, ln_final)
    y_sample = _rmsnorm(ys, ln_final)
    return (y_prompt, y_sample,
            jnp.stack(akp), jnp.stack(avp), jnp.stack(bkp), jnp.stack(bvp), jnp.stack(mkp), jnp.stack(mvp),
            jnp.stack(aks), jnp.stack(avs), jnp.stack(bks), jnp.stack(bvs))
```

```python
import functools
import math

import numpy as np
import jax
import jax.numpy as jnp
from jax import lax
from jax.experimental import pallas as pl
from jax.experimental.pallas import tpu as pltpu

F32 = jnp.float32
BF16 = jnp.bfloat16
I32 = jnp.int32

EPS = 1e-6
MASKED = -1e30

LANES = 128
SUBLANES = 8
VMEM_LIMIT = 48 * 1024 * 1024

HEAD_DIM = 64
A_HEADS = 16
A_PATTERNS = ((128, 1), (512, 4), (2048, 16))
A_WIN = 2048
B_HEADS = 16
B_KV_HEADS = 2
B_WIN = 128
A_WIDTH = A_HEADS * HEAD_DIM
B_WIDTH = B_HEADS * HEAD_DIM
B_KV_WIDTH = B_KV_HEADS * HEAD_DIM
ATTN_SCALE = HEAD_DIM ** -0.5
N_BUCKETS = 32
MAX_EXACT = N_BUCKETS // 2
MAX_DISTANCE = A_WIN
MEM_HEADS = 4
MEM_HEAD_DIM = 128
MEM_WIDTH = MEM_HEADS * MEM_HEAD_DIM
PEER_HEADS = 8
PEER_NKEYS = 128
PEER_DKEY = 256
PEER_TOPK = 16
BLK = 128


def _params(*semantics):
    return pltpu.CompilerParams(dimension_semantics=semantics, vmem_limit_bytes=VMEM_LIMIT)


def _norm_matmul_kernel(*refs, n_groups, norm, residual):
    xs = refs[:n_groups]
    pos = n_groups
    gs = refs[pos:pos + n_groups] if norm else ()
    pos += n_groups if norm else 0
    w_ref = refs[pos]
    pos += 1
    r_ref = refs[pos] if residual else None
    pos += 1 if residual else 0
    o_ref, xn_ref = refs[pos], refs[pos + 1]

    @pl.when(pl.program_id(1) == 0)
    def _():
        off = 0
        for gi in range(n_groups):
            x = xs[gi][...]
            if norm:
                ms = jnp.mean(x * x, axis=-1, keepdims=True)
                x = x * lax.rsqrt(ms + EPS) * gs[gi][...]
            width = x.shape[-1]
            xn_ref[:, off:off + width] = x.astype(BF16)
            off += width

    acc = jnp.dot(xn_ref[...], w_ref[...], preferred_element_type=F32)
    if residual:
        acc = acc + r_ref[...]
    o_ref[...] = acc


def _norm_matmul(xs, gains, w, residual=None, *, tm, tn, out_split=1):
    m = xs[0].shape[0]
    k_total, n = w.shape
    assert sum(x.shape[1] for x in xs) == k_total and m % tm == 0 and n % (tn * out_split) == 0
    norm = gains is not None
    nj_per = n // out_split // tn
    in_specs = [pl.BlockSpec((tm, x.shape[1]), lambda i, j: (i, 0)) for x in xs]
    args = list(xs)
    if norm:
        in_specs += [pl.BlockSpec((1, g.shape[-1]), lambda i, j: (0, 0)) for g in gains]
        args += [g.reshape(1, -1) for g in gains]
    in_specs.append(pl.BlockSpec((k_total, tn), lambda i, j: (0, j)))
    args.append(w)
    if residual is not None:
        in_specs.append(pl.BlockSpec((tm, tn), lambda i, j: (i, j)))
        args.append(residual)
    if out_split == 1:
        out_shape = jax.ShapeDtypeStruct((m, n), F32)
        out_spec = pl.BlockSpec((tm, tn), lambda i, j: (i, j))
    else:
        out_shape = jax.ShapeDtypeStruct((out_split, m, n // out_split), F32)
        out_spec = pl.BlockSpec((None, tm, tn), lambda i, j: (j // nj_per, i, j % nj_per))
    return pl.pallas_call(
        functools.partial(_norm_matmul_kernel, n_groups=len(xs), norm=norm, residual=residual is not None),
        grid=(m // tm, n // tn),
        in_specs=in_specs,
        out_specs=out_spec,
        out_shape=out_shape,
        scratch_shapes=[pltpu.VMEM((tm, k_total), BF16)],
        compiler_params=_params("parallel", "arbitrary"),
    )(*args)


def _rel_bucket(dist):
    dist = np.maximum(np.asarray(dist), 0)
    ratio = np.log(np.maximum(dist, 1) / MAX_EXACT) / math.log(MAX_DISTANCE / MAX_EXACT)
    large = np.minimum(MAX_EXACT + (ratio * (N_BUCKETS - MAX_EXACT)).astype(np.int32), N_BUCKETS - 1)
    return np.where(dist < MAX_EXACT, dist, large).astype(np.int32)


def _mixer_a_multiplicity(dist):
    dist = np.asarray(dist)
    mult = np.zeros(dist.shape, np.int32)
    for window, dilation in A_PATTERNS:
        mult += ((dist >= 0) & (dist <= window) & (dist % dilation == 0)).astype(np.int32)
    return mult


def _distance_bias(rel_bias, dist, head_lo, n_heads, mult):
    table = rel_bias[:, head_lo:head_lo + n_heads].astype(F32).T
    vals = table[:, _rel_bucket(dist)]
    logm = np.log(np.maximum(mult, 1)).astype(np.float32)
    return jnp.where(jnp.asarray(mult > 0)[None], vals + jnp.asarray(logm)[None], MASKED)


def _pair_select(x, which):
    lane_head = lax.broadcasted_iota(I32, x.shape, 1) // HEAD_DIM
    swapped = pltpu.roll(x, HEAD_DIM, axis=1)
    return jnp.where(lane_head == which, x, swapped)


def _prompt_attn_kernel(*refs, n_delta, gqa_pairs, has_sink):
    if has_sink:
        q_ref, k_ref, v_ref, bias_ref, sink_ref, o_ref = refs
    else:
        q_ref, k_ref, v_ref, bias_ref, o_ref = refs
        sink_ref = None
    hp = pl.program_id(1)
    qb = pl.program_id(2)
    lane = lax.broadcasted_iota(I32, (BLK, LANES), 1)
    left = lane < HEAD_DIM
    q = q_ref[...] * ATTN_SCALE
    q_heads = (jnp.where(left, q, 0.0).astype(BF16), jnp.where(left, 0.0, q).astype(BF16))

    def body(kb, carry):
        m0, l0, m1, l1, acc = carry
        start = pl.multiple_of(kb * BLK, BLK)
        k = k_ref[pl.ds(start, BLK), :]
        v = v_ref[pl.ds(start, BLK), :]
        if gqa_pairs:
            kv_head = hp // gqa_pairs
            k = _pair_select(k, kv_head)
            v = _pair_select(v, kv_head)
        k = k.astype(BF16)
        v = v.astype(BF16)
        delta = qb - kb
        stats = []
        pvs = []
        for h2, (m, l) in enumerate(((m0, l0), (m1, l1))):
            s = lax.dot_general(q_heads[h2], k, (((1,), (1,)), ((), ())), preferred_element_type=F32)
            s = s + bias_ref[h2, delta]
            m_new = jnp.maximum(m, jnp.max(s, axis=-1, keepdims=True))
            alpha = jnp.exp(m - m_new)
            p = jnp.exp(s - m_new)
            l_new = alpha * l + jnp.sum(p, axis=-1, keepdims=True)
            pvs.append(jnp.dot(p.astype(BF16), v, preferred_element_type=F32))
            stats.append((m_new, l_new, alpha))
        alpha = jnp.where(left, stats[0][2], stats[1][2])
        acc = alpha * acc + jnp.where(left, pvs[0], pvs[1])
        return stats[0][0], stats[0][1], stats[1][0], stats[1][1], acc

    if has_sink:
        m_init = (jnp.full((BLK, 1), sink_ref[2 * hp], F32), jnp.full((BLK, 1), sink_ref[2 * hp + 1], F32))
        l_init = jnp.ones((BLK, 1), F32)
    else:
        m_init = (jnp.full((BLK, 1), MASKED, F32),) * 2
        l_init = jnp.zeros((BLK, 1), F32)
    init = (m_init[0], l_init, m_init[1], l_init, jnp.zeros((BLK, LANES), F32))
    lo = jnp.maximum(qb - (n_delta - 1), 0)
    _, l0, _, l1, acc = lax.fori_loop(lo, qb + 1, body, init)
    o_ref[...] = acc / jnp.where(left, l0, l1)


def _prompt_attention(q_src, k_src, v_src, bias, sinks, *, batch, seq, gqa_pairs):
    n_pairs = A_WIDTH // LANES
    n_delta = bias.shape[1]
    nq = seq // BLK
    (q_arr, q_idx), (k_arr, k_idx), (v_arr, v_idx) = q_src, k_src, v_src
    kv_map = (lambda which: (lambda b, hp, qb: (which, b, 0))) if gqa_pairs else \
        (lambda which: (lambda b, hp, qb: (which, b, hp)))
    in_specs = [
        pl.BlockSpec((None, BLK, LANES), lambda b, hp, qb: (q_idx, b * nq + qb, hp)),
        pl.BlockSpec((None, seq, LANES), kv_map(k_idx)),
        pl.BlockSpec((None, seq, LANES), kv_map(v_idx)),
        pl.BlockSpec((2, n_delta, BLK, BLK), lambda b, hp, qb: (hp, 0, 0, 0)),
    ]
    args = [q_arr, k_arr, v_arr, bias]
    if sinks is not None:
        in_specs.append(pl.BlockSpec(memory_space=pltpu.SMEM))
        args.append(sinks)
    return pl.pallas_call(
        functools.partial(_prompt_attn_kernel, n_delta=n_delta, gqa_pairs=gqa_pairs, has_sink=sinks is not None),
        grid=(batch, n_pairs, nq),
        in_specs=in_specs,
        out_specs=pl.BlockSpec((BLK, LANES), lambda b, hp, qb: (b * nq + qb, hp)),
        out_shape=jax.ShapeDtypeStruct((batch * seq, A_WIDTH), F32),
        compiler_params=_params("parallel", "parallel", "arbitrary"),
    )(*args)


def _prompt_bias_tiles(rel_bias, head_lo, mult_fn, n_delta):
    delta = np.arange(n_delta)[:, None, None]
    dist = delta * BLK + np.arange(BLK)[None, :, None] - np.arange(BLK)[None, None, :]
    return _distance_bias(rel_bias, dist, head_lo, 16, mult_fn(dist))


def _block_diag_queries(q, n_heads):
    t, width = q.shape
    rows = n_heads * t
    tiled = jnp.broadcast_to(q[None], (n_heads, t, width)).reshape(rows, width)
    row_head = lax.broadcasted_iota(I32, (rows, width), 0) // t
    lane_head = lax.broadcasted_iota(I32, (rows, width), 1) // (width // n_heads)
    return jnp.where(row_head == lane_head, tiled * ATTN_SCALE, 0.0).astype(BF16)


def _block_diag_extract(o, n_heads):
    rows, width = o.shape
    t = rows // n_heads
    row_head = lax.broadcasted_iota(I32, (rows, width), 0) // t
    lane_head = lax.broadcasted_iota(I32, (rows, width), 1) // (width // n_heads)
    return jnp.sum(jnp.where(row_head == lane_head, o, 0.0).reshape(n_heads, t, width), axis=0)


def _softmax_step(s, v, m_ref, l_ref, acc_ref):
    m_old = m_ref[...]
    m_new = jnp.maximum(m_old, jnp.max(s, axis=-1, keepdims=True))
    alpha = jnp.exp(m_old - m_new)
    p = jnp.exp(s - m_new)
    l_ref[...] = alpha * l_ref[...] + jnp.sum(p, axis=-1, keepdims=True)
    acc_ref[...] = alpha * acc_ref[...] + jnp.dot(p.astype(BF16), v, preferred_element_type=F32)
    m_ref[...] = m_new


def _sample_a_kernel(q_ref, kn_ref, vn_ref, kc_ref, vc_ref, knext_ref, vnext_ref, bias_ref, biasn_ref,
                     o_ref, ko_ref, vo_ref, qbd_ref, m_ref, l_ref, acc_ref):
    c = pl.program_id(1)
    last = pl.num_programs(1) - 1
    t_new = kn_ref.shape[0]
    chunk = kc_ref.shape[0]

    @pl.when(c == 0)
    def _():
        qbd_ref[...] = _block_diag_queries(q_ref[...], A_HEADS)
        m_ref[...] = jnp.full(m_ref.shape, MASKED, F32)
        l_ref[...] = jnp.zeros(l_ref.shape, F32)
        acc_ref[...] = jnp.zeros(acc_ref.shape, F32)

    kc = kc_ref[...]
    vc = vc_ref[...]
    s = lax.dot_general(qbd_ref[...], kc.astype(BF16), (((1,), (1,)), ((), ())), preferred_element_type=F32)
    _softmax_step(s + bias_ref[...], vc.astype(BF16), m_ref, l_ref, acc_ref)

    ko_ref[0:chunk - t_new, :] = kc[t_new:, :]
    vo_ref[0:chunk - t_new, :] = vc[t_new:, :]

    @pl.when(c < last)
    def _():
        ko_ref[chunk - t_new:, :] = knext_ref[...]
        vo_ref[chunk - t_new:, :] = vnext_ref[...]

    @pl.when(c == last)
    def _():
        kn = kn_ref[...]
        vn = vn_ref[...]
        ko_ref[chunk - t_new:, :] = kn
        vo_ref[chunk - t_new:, :] = vn
        zeros = jnp.zeros((LANES - t_new, kn.shape[1]), F32)
        kpad = jnp.concatenate([kn, zeros], axis=0).astype(BF16)
        vpad = jnp.concatenate([vn, zeros], axis=0).astype(BF16)
        s_new = lax.dot_general(qbd_ref[...], kpad, (((1,), (1,)), ((), ())), preferred_element_type=F32)
        _softmax_step(s_new + biasn_ref[...], vpad, m_ref, l_ref, acc_ref)
        o_ref[...] = _block_diag_extract(acc_ref[...] / l_ref[...], A_HEADS)


def _sample_attention_a(q, k_new, v_new, cache_k, cache_v, bias, bias_new, *, chunk):
    nb, win, width = cache_k.shape
    t = q.shape[0] // nb
    nc = win // chunk
    rows = A_HEADS * t
    per_chunk = chunk // t
    new_spec = pl.BlockSpec((t, width), lambda b, c: (b, 0))
    cache_spec = pl.BlockSpec((None, chunk, width), lambda b, c: (b, c, 0))
    next_spec = pl.BlockSpec((None, t, width), lambda b, c: (b, jnp.minimum(c + 1, nc - 1) * per_chunk, 0))
    return pl.pallas_call(
        _sample_a_kernel,
        grid=(nb, nc),
        in_specs=[new_spec, new_spec, new_spec, cache_spec, cache_spec, next_spec, next_spec,
                  pl.BlockSpec((None, rows, chunk), lambda b, c: (c, 0, 0)),
                  pl.BlockSpec((rows, LANES), lambda b, c: (0, 0))],
        out_specs=[new_spec, cache_spec, cache_spec],
        out_shape=[jax.ShapeDtypeStruct((nb * t, width), F32),
                   jax.ShapeDtypeStruct(cache_k.shape, F32), jax.ShapeDtypeStruct(cache_v.shape, F32)],
        scratch_shapes=[pltpu.VMEM((rows, width), BF16), pltpu.VMEM((rows, 1), F32), pltpu.VMEM((rows, 1), F32),
                        pltpu.VMEM((rows, width), F32)],
        compiler_params=_params("parallel", "arbitrary"),
    )(q, k_new, v_new, cache_k, cache_v, cache_k, cache_v, bias, bias_new)


def _expand_kv(x):
    first = _pair_select(x, False)
    second = _pair_select(x, True)
    reps = B_HEADS // B_KV_HEADS // 2
    return jnp.concatenate([first] * reps + [second] * reps, axis=1)


def _sample_b_kernel(q_ref, kn_ref, vn_ref, kc_ref, vc_ref, bias_ref, sink_ref, o_ref, ko_ref, vo_ref, seq_ref):
    t_new = kn_ref.shape[0]
    win = kc_ref.shape[0]
    seq_ref[...] = jnp.zeros(seq_ref.shape, F32)
    for idx, (c_ref, n_ref, out_ref) in enumerate(((kc_ref, kn_ref, ko_ref), (vc_ref, vn_ref, vo_ref))):
        seq_ref[idx, 0:win, :] = c_ref[...]
        seq_ref[idx, win:win + t_new, :] = n_ref[...]
        out_ref[...] = seq_ref[idx, t_new:win + t_new, :]
    qbd = _block_diag_queries(q_ref[...], B_HEADS)
    k = _expand_kv(seq_ref[0]).astype(BF16)
    v = _expand_kv(seq_ref[1]).astype(BF16)
    s = lax.dot_general(qbd, k, (((1,), (1,)), ((), ())), preferred_element_type=F32) + bias_ref[...]
    sink = sink_ref[...]
    m = jnp.maximum(jnp.max(s, axis=-1, keepdims=True), sink)
    p = jnp.exp(s - m)
    den = jnp.sum(p, axis=-1, keepdims=True) + jnp.exp(sink - m)
    o = jnp.dot(p.astype(BF16), v, preferred_element_type=F32) / den
    o_ref[...] = _block_diag_extract(o, B_HEADS)


def _sample_attention_b(q, k_new, v_new, cache_k, cache_v, bias, sink_rows):
    nb, win, kvw = cache_k.shape
    t = q.shape[0] // nb
    rows = B_HEADS * t
    q_spec = pl.BlockSpec((t, B_WIDTH), lambda b: (b, 0))
    new_spec = pl.BlockSpec((t, kvw), lambda b: (b, 0))
    cache_spec = pl.BlockSpec((None, win, kvw), lambda b: (b, 0, 0))
    return pl.pallas_call(
        _sample_b_kernel,
        grid=(nb,),
        in_specs=[q_spec, new_spec, new_spec, cache_spec, cache_spec,
                  pl.BlockSpec((rows, 2 * win), lambda b: (0, 0)), pl.BlockSpec((rows, 1), lambda b: (0, 0))],
        out_specs=[q_spec, cache_spec, cache_spec],
        out_shape=[jax.ShapeDtypeStruct(q.shape, F32),
                   jax.ShapeDtypeStruct(cache_k.shape, F32), jax.ShapeDtypeStruct(cache_v.shape, F32)],
        scratch_shapes=[pltpu.VMEM((2, 2 * win, kvw), F32)],
        compiler_params=_params("parallel"),
    )(q, k_new, v_new, cache_k, cache_v, bias, sink_rows)


def _cross_attn_kernel(q_ref, k_ref, v_ref, o_ref):
    scale = MEM_HEAD_DIM ** -0.5
    for h in range(MEM_HEADS):
        cols = slice(h * MEM_HEAD_DIM, (h + 1) * MEM_HEAD_DIM)
        q = q_ref[:, cols].astype(BF16)
        k = k_ref[:, cols].astype(BF16)
        v = v_ref[:, cols].astype(BF16)
        s = lax.dot_general(q, k, (((1,), (1,)), ((), ())), preferred_element_type=F32) * scale
        m = jnp.max(s, axis=-1, keepdims=True)
        e = jnp.exp(s - m)
        p = e / jnp.sum(e, axis=-1, keepdims=True)
        o_ref[:, cols] = jnp.dot(p.astype(BF16), v, preferred_element_type=F32)


def _cross_attention(q, mk, mv, *, tq):
    nb, s, width = q.shape
    mem = mk.shape[1]
    q_spec = pl.BlockSpec((None, tq, width), lambda b, i: (b, i, 0))
    m_spec = pl.BlockSpec((None, mem, width), lambda b, i: (b, 0, 0))
    return pl.pallas_call(
        _cross_attn_kernel,
        grid=(nb, s // tq),
        in_specs=[q_spec, m_spec, m_spec],
        out_specs=q_spec,
        out_shape=jax.ShapeDtypeStruct(q.shape, F32),
        compiler_params=_params("parallel", "arbitrary"),
    )(q, mk, mv)


def _top_rows(s, ids, k):
    n = s.shape[0]
    row = lax.broadcasted_iota(I32, s.shape, 0).astype(F32)
    vals, picked = [], []
    for _ in range(k):
        m = jnp.max(s, axis=0, keepdims=True)
        pos = jnp.min(jnp.where(s == m, row, float(n)), axis=0, keepdims=True)
        hit = row == pos
        vals.append(m)
        picked.append(jnp.max(jnp.where(hit, ids, -1.0), axis=0, keepdims=True))
        s = jnp.where(hit, -jnp.inf, s)
    return jnp.concatenate(vals, axis=0), jnp.concatenate(picked, axis=0)


def _peer_topk_kernel(q_ref, k1_ref, k2_ref, eidx_ref, gate_ref):
    half = PEER_DKEY // 2
    q = q_ref[...]
    tb = q.shape[0]
    key_ids = lax.broadcasted_iota(I32, (PEER_NKEYS, tb), 0).astype(F32)
    nt = (((1,), (1,)), ((), ()))
    s1 = lax.dot_general(k1_ref[...], q[:, :half].astype(BF16), nt, preferred_element_type=F32)
    s2 = lax.dot_general(k2_ref[...], q[:, half:].astype(BF16), nt, preferred_element_type=F32)
    v1, i1 = _top_rows(s1, key_ids, PEER_TOPK)
    v2, i2 = _top_rows(s2, key_ids, PEER_TOPK)
    cand = jnp.concatenate([v1[a:a + 1] + v2 for a in range(PEER_TOPK)], axis=0)
    cidx = jnp.concatenate([i1[a:a + 1] * PEER_NKEYS + i2 for a in range(PEER_TOPK)], axis=0)
    best, eidx = _top_rows(cand, cidx, PEER_TOPK)
    e = jnp.exp(best - best[0:1])
    gate_ref[...] = e / jnp.sum(e, axis=0, keepdims=True)
    eidx_ref[...] = eidx.astype(I32)


def _peer_topk(q, sub_k1, sub_k2, *, tb):
    m = q.shape[0]
    rows = PEER_HEADS * PEER_TOPK
    key_spec = pl.BlockSpec((PEER_NKEYS, PEER_DKEY // 2), lambda i, h: (0, 0))
    out_spec = pl.BlockSpec((PEER_TOPK, tb), lambda i, h: (h, i))
    return pl.pallas_call(
        _peer_topk_kernel,
        grid=(m // tb, PEER_HEADS),
        in_specs=[pl.BlockSpec((tb, PEER_DKEY), lambda i, h: (i, h)), key_spec, key_spec],
        out_specs=[out_spec, out_spec],
        out_shape=[jax.ShapeDtypeStruct((rows, m), I32), jax.ShapeDtypeStruct((rows, m), F32)],
        compiler_params=_params("parallel", "arbitrary"),
    )(q, sub_k1, sub_k2)


def _peer_expert_kernel(idx_ref, idxn_ref, gate_ref, y_ref, lnx_ref, lnf_ref, u_hbm, v_hbm, o_ref,
                        ubuf, vbuf, sem, *, tg, n_sel):
    i = pl.program_id(0)
    n_steps = pl.num_programs(0)
    slot = i % 2
    rows = tg * n_sel

    def row_copy(table, buf, s, expert, j):
        return pltpu.make_async_copy(table.at[pl.ds(expert, 1), :], buf.at[s, pl.ds(j, 1), :], sem.at[s])

    def issue(ids, s):
        def body(j, carry):
            expert = ids[j // n_sel, j % n_sel]
            row_copy(u_hbm, ubuf, s, expert, j).start()
            row_copy(v_hbm, vbuf, s, expert, j).start()
            return carry
        lax.fori_loop(0, rows, body, 0, unroll=8)

    @pl.when(i == 0)
    def _():
        issue(idx_ref, 0)

    @pl.when(i + 1 < n_steps)
    def _():
        issue(idxn_ref, 1 - slot)

    pltpu.make_async_copy(ubuf.at[slot], ubuf.at[slot], sem.at[slot]).wait()
    pltpu.make_async_copy(vbuf.at[slot], vbuf.at[slot], sem.at[slot]).wait()

    d = y_ref.shape[1]
    y = y_ref[...]
    x = y * lax.rsqrt(jnp.mean(y * y, axis=-1, keepdims=True) + EPS) * lnx_ref[...]
    token = lax.broadcasted_iota(I32, (n_sel, tg), 1)
    pre = jnp.zeros((n_sel, tg), F32)
    for t in range(tg):
        u = ubuf[slot, t * n_sel:(t + 1) * n_sel, :]
        prod = u * x[t:t + 1, :]
        part = prod[:, 0:LANES]
        for c in range(1, d // LANES):
            part = part + prod[:, c * LANES:(c + 1) * LANES]
        pre = jnp.where(token == t, jnp.sum(part, axis=-1, keepdims=True), pre)
    act = 0.5 * pre * (1.0 + lax.erf(pre * (2.0 ** -0.5)))
    w = gate_ref[...] * act
    for t in range(tg):
        v = vbuf[slot, t * n_sel:(t + 1) * n_sel, :]
        out = y[t:t + 1, :] + jnp.sum(v * w[:, t:t + 1], axis=0, keepdims=True)
        ms = jnp.mean(out * out, axis=-1, keepdims=True)
        o_ref[t:t + 1, :] = out * lax.rsqrt(ms + EPS) * lnf_ref[...]


def _peer_experts(eidx, gate_cols, y, ln_ffn, ln_final, expert_u, expert_v, *, tg):
    m, d = y.shape
    n_sel = eidx.shape[1]
    n_steps = m // tg
    row_spec = pl.BlockSpec((tg, d), lambda i: (i, 0))
    return pl.pallas_call(
        functools.partial(_peer_expert_kernel, tg=tg, n_sel=n_sel),
        grid=(n_steps,),
        in_specs=[
            pl.BlockSpec((tg, n_sel), lambda i: (i, 0), memory_space=pltpu.SMEM),
            pl.BlockSpec((tg, n_sel), lambda i: (jnp.minimum(i + 1, n_steps - 1), 0), memory_space=pltpu.SMEM),
            pl.BlockSpec((None, n_sel, tg), lambda i: (i, 0, 0)),
            row_spec,
            pl.BlockSpec((1, d), lambda i: (0, 0)),
            pl.BlockSpec((1, d), lambda i: (0, 0)),
            pl.BlockSpec(memory_space=pl.ANY),
            pl.BlockSpec(memory_space=pl.ANY),
        ],
        out_specs=row_spec,
        out_shape=jax.ShapeDtypeStruct((m, d), F32),
        scratch_shapes=[pltpu.VMEM((2, tg * n_sel, d), F32), pltpu.VMEM((2, tg * n_sel, d), F32),
                        pltpu.SemaphoreType.DMA((2,))],
        compiler_params=_params("arbitrary"),
    )(eidx, eidx, gate_cols, y, ln_ffn.reshape(1, d), ln_final.reshape(1, d), expert_u, expert_v)


def _channel_mixers(y, mk, mv, nb, p, *, tm, tq, tb, tg):
    m, d = y.shape
    q = _norm_matmul([y], [p["ln_cross"]], p["w_cq"], tm=tm, tn=MEM_WIDTH)
    o = _cross_attention(q.reshape(nb, m // nb, MEM_WIDTH), mk, mv, tq=tq).reshape(m, MEM_WIDTH)
    y = _norm_matmul([o], None, p["w_co"], residual=y, tm=tm, tn=512)
    pq = _norm_matmul([y], [p["ln_ffn"]], p["w_pq"], tm=tm, tn=512)
    eidx_t, gate_t = _peer_topk(pq, p["sub_k1"], p["sub_k2"], tb=tb)
    n_sel = eidx_t.shape[0]
    gate_cols = gate_t.reshape(n_sel, m // tg, tg).transpose(1, 0, 2)
    return _peer_experts(eidx_t.T, gate_cols, y, p["ln_ffn"], p["ln_final"], p["expert_u"], p["expert_v"], tg=tg)


def kernel(x_prompt, x_sample, cache_a_k, cache_a_v, cache_b_k, cache_b_v, cache_mem_k, cache_mem_v, mem_prompt, ln_mix, w_in, ln_a_out, ln_b_out, w_out, b_sinks, rel_bias, ln_cross, ln_mem, w_cq, w_ckv, w_co, ln_ffn, w_pq, sub_keys_1, sub_keys_2, expert_u, expert_v, ln_final):
    depth = w_in.shape[0]
    assert depth == 1, "the caches are laid out for a single layer"
    batch, seq, d = x_prompt.shape
    dec_batch, dec_seq, _ = x_sample.shape
    a_win = cache_a_k.shape[2]
    b_win = cache_b_k.shape[2]
    mem_len = mem_prompt.shape[1]
    assert seq == a_win == A_WIN and b_win == B_WIN and seq % BLK == 0
    l = 0
    split = 3 * A_WIDTH + B_WIDTH
    w_in_main = w_in[l, :, :split].astype(BF16)
    w_in_kvb = w_in[l, :, split:].astype(BF16)
    p = dict(ln_cross=ln_cross[l], w_cq=w_cq[l].astype(BF16), w_co=w_co[l].astype(BF16), ln_ffn=ln_ffn[l],
             w_pq=w_pq[l].astype(BF16), sub_k1=sub_keys_1[l].astype(BF16), sub_k2=sub_keys_2[l].astype(BF16),
             ln_final=ln_final, expert_u=expert_u[l], expert_v=expert_v[l])
    w_out_bf = w_out[l].astype(BF16)
    w_ckv_bf = w_ckv[l].astype(BF16)
    sinks = b_sinks[l].astype(F32)

    def project(x2d, tm):
        main = _norm_matmul([x2d], [ln_mix[l]], w_in_main, tm=tm, tn=512, out_split=4)
        kvb = _norm_matmul([x2d], [ln_mix[l]], w_in_kvb, tm=tm, tn=B_KV_WIDTH, out_split=2)
        return main, kvb

    def merge(oa, ob, resid, tm):
        return _norm_matmul([oa, ob], [ln_a_out[l], ln_b_out[l]], w_out_bf, residual=resid, tm=tm, tn=512)

    xp = x_prompt.reshape(batch * seq, d)
    main, kvb = project(xp, 512)
    n_delta_a = seq // BLK + 1
    bias_a = _prompt_bias_tiles(rel_bias, 0, _mixer_a_multiplicity, n_delta_a)
    window_b = lambda dist: ((dist >= 0) & (dist <= B_WIN)).astype(np.int32)
    bias_b = _prompt_bias_tiles(rel_bias, A_HEADS, window_b, B_WIN // BLK + 1)
    oa = _prompt_attention((main, 0), (main, 1), (main, 2), bias_a, None, batch=batch, seq=seq, gqa_pairs=0)
    ob = _prompt_attention((main, 3), (kvb, 0), (kvb, 1), bias_b, sinks, batch=batch, seq=seq,
                           gqa_pairs=B_HEADS // B_KV_HEADS // 2)
    yp = merge(oa, ob, xp, 512)
    mem_kv = _norm_matmul([mem_prompt.reshape(batch * mem_len, d)], [ln_mem[l]], w_ckv_bf, tm=512, tn=MEM_WIDTH,
                          out_split=2)
    mk = mem_kv[0].reshape(batch, mem_len, MEM_WIDTH)
    mv = mem_kv[1].reshape(batch, mem_len, MEM_WIDTH)
    y_prompt = _channel_mixers(yp, mk, mv, batch, p, tm=512, tq=512, tb=128, tg=8)

    xs = x_sample.reshape(dec_batch * dec_seq, d)
    main_s, kvb_s = project(xs, 512)
    chunk = 512
    key_pos = np.arange(a_win)
    t_pos = np.arange(dec_seq)
    dist_cache = a_win + t_pos[:, None] - key_pos[None, :]
    bias_sa = _distance_bias(rel_bias, dist_cache, 0, A_HEADS, _mixer_a_multiplicity(dist_cache))
    bias_sa = bias_sa.reshape(A_HEADS * dec_seq, a_win // chunk, chunk).transpose(1, 0, 2)
    dist_new = t_pos[:, None] - np.arange(LANES)[None, :]
    mult_new = np.where(np.arange(LANES)[None, :] < dec_seq, _mixer_a_multiplicity(dist_new), 0)
    bias_sa_new = _distance_bias(rel_bias, dist_new, 0, A_HEADS, mult_new).reshape(A_HEADS * dec_seq, LANES)
    oa_s, aks, avs = _sample_attention_a(
        main_s[0], main_s[1], main_s[2],
        cache_a_k[l].reshape(dec_batch, a_win, A_WIDTH), cache_a_v[l].reshape(dec_batch, a_win, A_WIDTH),
        bias_sa, bias_sa_new, chunk=chunk)
    seq_pos = np.arange(2 * b_win)
    dist_b = b_win + t_pos[:, None] - seq_pos[None, :]
    mult_b = ((dist_b >= 0) & (dist_b <= B_WIN) & (seq_pos[None, :] < b_win + dec_seq)).astype(np.int32)
    bias_sb = _distance_bias(rel_bias, dist_b, A_HEADS, B_HEADS, mult_b).reshape(B_HEADS * dec_seq, 2 * b_win)
    sink_rows = jnp.repeat(sinks, dec_seq).reshape(B_HEADS * dec_seq, 1)
    ob_s, bks, bvs = _sample_attention_b(
        main_s[3], kvb_s[0], kvb_s[1],
        cache_b_k[l].reshape(dec_batch, b_win, B_KV_WIDTH), cache_b_v[l].reshape(dec_batch, b_win, B_KV_WIDTH),
        bias_sb, sink_rows)
    ys = merge(oa_s, ob_s, xs, 512)
    mk_s = cache_mem_k[l].reshape(dec_batch, mem_len, MEM_WIDTH)
    mv_s = cache_mem_v[l].reshape(dec_batch, mem_len, MEM_WIDTH)
    y_sample = _channel_mixers(ys, mk_s, mv_s, dec_batch, p, tm=512, tq=dec_seq, tb=128, tg=8)

    def heads(x, *shape):
        return x.reshape(1, *shape)

    return (y_prompt.reshape(batch, seq, d), y_sample.reshape(dec_batch, dec_seq, d),
            heads(main[1], batch, seq, A_HEADS, HEAD_DIM), heads(main[2], batch, seq, A_HEADS, HEAD_DIM),
            heads(kvb[0].reshape(batch, seq, B_KV_WIDTH)[:, seq - b_win:], batch, b_win, B_KV_HEADS, HEAD_DIM),
            heads(kvb[1].reshape(batch, seq, B_KV_WIDTH)[:, seq - b_win:], batch, b_win, B_KV_HEADS, HEAD_DIM),
            heads(mk, batch, mem_len, MEM_HEADS, MEM_HEAD_DIM), heads(mv, batch, mem_len, MEM_HEADS, MEM_HEAD_DIM),
            heads(aks, dec_batch, a_win, A_HEADS, HEAD_DIM), heads(avs, dec_batch, a_win, A_HEADS, HEAD_DIM),
            heads(bks, dec_batch, b_win, B_KV_HEADS, HEAD_DIM), heads(bvs, dec_batch, b_win, B_KV_HEADS, HEAD_DIM))
```

```python
import functools
import math

import numpy as np
import jax
import jax.numpy as jnp
from jax import lax
from jax.experimental import pallas as pl
from jax.experimental.pallas import tpu as pltpu

F32 = jnp.float32
BF16 = jnp.bfloat16
I32 = jnp.int32

EPS = 1e-6
MASKED = -1e30

LANES = 128
SUBLANES = 8
VMEM_LIMIT = 48 * 1024 * 1024

HEAD_DIM = 64
A_HEADS = 16
A_PATTERNS = ((128, 1), (512, 4), (2048, 16))
A_WIN = 2048
B_HEADS = 16
B_KV_HEADS = 2
B_WIN = 128
A_WIDTH = A_HEADS * HEAD_DIM
B_WIDTH = B_HEADS * HEAD_DIM
B_KV_WIDTH = B_KV_HEADS * HEAD_DIM
ATTN_SCALE = HEAD_DIM ** -0.5
N_BUCKETS = 32
MAX_EXACT = N_BUCKETS // 2
MAX_DISTANCE = A_WIN
MEM_HEADS = 4
MEM_HEAD_DIM = 128
MEM_WIDTH = MEM_HEADS * MEM_HEAD_DIM
PEER_HEADS = 8
PEER_NKEYS = 128
PEER_DKEY = 256
PEER_TOPK = 16
BLK = 128


def _params(*semantics, flags=None):
    return pltpu.CompilerParams(dimension_semantics=semantics, vmem_limit_bytes=VMEM_LIMIT, flags=flags)


def _norm_matmul_kernel(*refs, n_groups, norm, residual):
    xs = refs[:n_groups]
    pos = n_groups
    gs = refs[pos:pos + n_groups] if norm else ()
    pos += n_groups if norm else 0
    w_ref = refs[pos]
    pos += 1
    r_ref = refs[pos] if residual else None
    pos += 1 if residual else 0
    o_ref, xn_ref = refs[pos], refs[pos + 1]

    @pl.when(pl.program_id(1) == 0)
    def _():
        off = 0
        for gi in range(n_groups):
            x = xs[gi][...]
            if norm:
                ms = jnp.mean(x * x, axis=-1, keepdims=True)
                x = x * lax.rsqrt(ms + EPS) * gs[gi][...]
            width = x.shape[-1]
            xn_ref[:, off:off + width] = x.astype(BF16)
            off += width

    acc = jnp.dot(xn_ref[...], w_ref[...], preferred_element_type=F32)
    if residual:
        acc = acc + r_ref[...]
    o_ref[...] = acc


def _norm_matmul(xs, gains, w, residual=None, *, tm, tn, out_split=1, name="norm_matmul"):
    m = xs[0].shape[0]
    k_total, n = w.shape
    assert sum(x.shape[1] for x in xs) == k_total and m % tm == 0 and n % (tn * out_split) == 0
    norm = gains is not None
    nj_per = n // out_split // tn
    in_specs = [pl.BlockSpec((tm, x.shape[1]), lambda i, j: (i, 0)) for x in xs]
    args = list(xs)
    if norm:
        in_specs += [pl.BlockSpec((1, g.shape[-1]), lambda i, j: (0, 0)) for g in gains]
        args += [g.reshape(1, -1) for g in gains]
    in_specs.append(pl.BlockSpec((k_total, tn), lambda i, j: (0, j)))
    args.append(w)
    if residual is not None:
        in_specs.append(pl.BlockSpec((tm, tn), lambda i, j: (i, j)))
        args.append(residual)
    if out_split == 1:
        out_shape = jax.ShapeDtypeStruct((m, n), F32)
        out_spec = pl.BlockSpec((tm, tn), lambda i, j: (i, j))
    else:
        out_shape = jax.ShapeDtypeStruct((out_split, m, n // out_split), F32)
        out_spec = pl.BlockSpec((None, tm, tn), lambda i, j: (j // nj_per, i, j % nj_per))
    return pl.pallas_call(
        functools.partial(_norm_matmul_kernel, n_groups=len(xs), norm=norm, residual=residual is not None),
        grid=(m // tm, n // tn),
        in_specs=in_specs,
        out_specs=out_spec,
        out_shape=out_shape,
        scratch_shapes=[pltpu.VMEM((tm, k_total), BF16)],
        compiler_params=_params("parallel", "arbitrary"),
        name=name,
    )(*args)


def _rel_bucket(dist):
    dist = np.maximum(np.asarray(dist), 0)
    ratio = np.log(np.maximum(dist, 1) / MAX_EXACT) / math.log(MAX_DISTANCE / MAX_EXACT)
    large = np.minimum(MAX_EXACT + (ratio * (N_BUCKETS - MAX_EXACT)).astype(np.int32), N_BUCKETS - 1)
    return np.where(dist < MAX_EXACT, dist, large).astype(np.int32)


def _mixer_a_multiplicity(dist):
    dist = np.asarray(dist)
    mult = np.zeros(dist.shape, np.int32)
    for window, dilation in A_PATTERNS:
        mult += ((dist >= 0) & (dist <= window) & (dist % dilation == 0)).astype(np.int32)
    return mult


def _distance_bias(rel_bias, dist, head_lo, n_heads, mult):
    table = rel_bias[:, head_lo:head_lo + n_heads].astype(F32).T
    vals = table[:, _rel_bucket(dist)]
    logm = np.log(np.maximum(mult, 1)).astype(np.float32)
    return jnp.where(jnp.asarray(mult > 0)[None], vals + jnp.asarray(logm)[None], MASKED)


def _pair_select(x, which):
    lane_head = lax.broadcasted_iota(I32, x.shape, 1) // HEAD_DIM
    swapped = pltpu.roll(x, HEAD_DIM, axis=1)
    return jnp.where(lane_head == which, x, swapped)


def _prompt_attn_kernel(*refs, n_delta, chunk_tiles, n_chunks, gqa_pairs, has_sink):
    if has_sink:
        q_ref, k_ref, v_ref, bias_ref, sink_ref, o_ref, m_ref, l_ref, acc_ref = refs
    else:
        q_ref, k_ref, v_ref, bias_ref, o_ref, m_ref, l_ref, acc_ref = refs
        sink_ref = None
    hp = pl.program_id(1)
    qb = pl.program_id(2)
    lane = lax.broadcasted_iota(I32, (BLK, LANES), 1)
    left = lane < HEAD_DIM
    q = q_ref[...] * ATTN_SCALE
    q_heads = (jnp.where(left, q, 0.0).astype(BF16), jnp.where(left, 0.0, q).astype(BF16))
    for h2 in range(2):
        if has_sink:
            m_ref[h2] = jnp.full((BLK, 1), sink_ref[2 * hp + h2], F32)
            l_ref[h2] = jnp.ones((BLK, 1), F32)
        else:
            m_ref[h2] = jnp.full((BLK, 1), MASKED, F32)
            l_ref[h2] = jnp.zeros((BLK, 1), F32)
    acc_ref[...] = jnp.zeros(acc_ref.shape, F32)

    def chunk(start_tile):
        start = pl.multiple_of(start_tile * BLK, BLK)
        k = k_ref[pl.ds(start, chunk_tiles * BLK), :]
        v = v_ref[pl.ds(start, chunk_tiles * BLK), :]
        if gqa_pairs:
            kv_head = hp // gqa_pairs
            k = _pair_select(k, kv_head)
            v = _pair_select(v, kv_head)
        k = k.astype(BF16)
        v = v.astype(BF16)
        alphas, pvs = [], []
        for h2 in range(2):
            s = lax.dot_general(q_heads[h2], k, (((1,), (1,)), ((), ())), preferred_element_type=F32)
            tiles = []
            for j in range(chunk_tiles):
                delta = qb - (start_tile + j)
                tiles.append(bias_ref[h2, jnp.where((delta >= 0) & (delta < n_delta), delta, n_delta)])
            s = s + jnp.concatenate(tiles, axis=1)
            m_old = m_ref[h2]
            m_new = jnp.maximum(m_old, jnp.max(s, axis=-1, keepdims=True))
            alpha = jnp.exp(m_old - m_new)
            p = jnp.exp(s - m_new)
            l_ref[h2] = alpha * l_ref[h2] + jnp.sum(p, axis=-1, keepdims=True)
            m_ref[h2] = m_new
            pvs.append(jnp.dot(p.astype(BF16), v, preferred_element_type=F32))
            alphas.append(alpha)
        acc_ref[...] = jnp.where(left, alphas[0], alphas[1]) * acc_ref[...] + jnp.where(left, pvs[0], pvs[1])

    if n_chunks == 1:
        chunk(jnp.maximum(qb - (chunk_tiles - 1), 0))
    else:
        for c in range(n_chunks):
            pl.when(c * chunk_tiles <= qb)(functools.partial(chunk, c * chunk_tiles))
    o_ref[...] = acc_ref[...] / jnp.where(left, l_ref[0], l_ref[1])


def _prompt_attention(q_src, k_src, v_src, bias, sinks, *, batch, seq, gqa_pairs, chunk_tiles):
    n_pairs = A_WIDTH // LANES
    n_delta = bias.shape[1] - 1
    nq = seq // BLK
    n_chunks = 1 if n_delta <= chunk_tiles else nq // chunk_tiles
    assert nq % chunk_tiles == 0
    (q_arr, q_idx), (k_arr, k_idx), (v_arr, v_idx) = q_src, k_src, v_src
    kv_map = (lambda which: (lambda b, hp, qb: (which, b, 0))) if gqa_pairs else \
        (lambda which: (lambda b, hp, qb: (which, b, hp)))
    in_specs = [
        pl.BlockSpec((None, BLK, LANES), lambda b, hp, qb: (q_idx, b * nq + qb, hp)),
        pl.BlockSpec((None, seq, LANES), kv_map(k_idx)),
        pl.BlockSpec((None, seq, LANES), kv_map(v_idx)),
        pl.BlockSpec((2, n_delta + 1, BLK, BLK), lambda b, hp, qb: (hp, 0, 0, 0)),
    ]
    args = [q_arr, k_arr, v_arr, bias]
    if sinks is not None:
        in_specs.append(pl.BlockSpec(memory_space=pltpu.SMEM))
        args.append(sinks)
    return pl.pallas_call(
        functools.partial(_prompt_attn_kernel, n_delta=n_delta, chunk_tiles=chunk_tiles, n_chunks=n_chunks,
                          gqa_pairs=gqa_pairs, has_sink=sinks is not None),
        grid=(batch, n_pairs, nq),
        in_specs=in_specs,
        out_specs=pl.BlockSpec((BLK, LANES), lambda b, hp, qb: (b * nq + qb, hp)),
        out_shape=jax.ShapeDtypeStruct((batch * seq, A_WIDTH), F32),
        scratch_shapes=[pltpu.VMEM((2, BLK, 1), F32), pltpu.VMEM((2, BLK, 1), F32), pltpu.VMEM((BLK, LANES), F32)],
        compiler_params=_params("parallel", "parallel", "arbitrary"),
        name="prompt_attn_b" if gqa_pairs else "prompt_attn_a",
    )(*args)


def _prompt_bias_tiles(rel_bias, head_lo, mult_fn, n_delta):
    delta = np.arange(n_delta + 1)[:, None, None]
    dist = delta * BLK + np.arange(BLK)[None, :, None] - np.arange(BLK)[None, None, :]
    mult = np.where(delta < n_delta, mult_fn(dist), 0)
    return _distance_bias(rel_bias, dist, head_lo, 16, mult)


def _block_diag_queries(q, n_heads):
    t, width = q.shape
    rows = n_heads * t
    tiled = jnp.broadcast_to(q[None], (n_heads, t, width)).reshape(rows, width)
    row_head = lax.broadcasted_iota(I32, (rows, width), 0) // t
    lane_head = lax.broadcasted_iota(I32, (rows, width), 1) // (width // n_heads)
    return jnp.where(row_head == lane_head, tiled * ATTN_SCALE, 0.0).astype(BF16)


def _block_diag_extract(o, n_heads):
    rows, width = o.shape
    t = rows // n_heads
    row_head = lax.broadcasted_iota(I32, (rows, width), 0) // t
    lane_head = lax.broadcasted_iota(I32, (rows, width), 1) // (width // n_heads)
    return jnp.sum(jnp.where(row_head == lane_head, o, 0.0).reshape(n_heads, t, width), axis=0)


def _softmax_step(s, v, m_ref, l_ref, acc_ref):
    m_old = m_ref[...]
    m_new = jnp.maximum(m_old, jnp.max(s, axis=-1, keepdims=True))
    alpha = jnp.exp(m_old - m_new)
    p = jnp.exp(s - m_new)
    l_ref[...] = alpha * l_ref[...] + jnp.sum(p, axis=-1, keepdims=True)
    acc_ref[...] = alpha * acc_ref[...] + jnp.dot(p.astype(BF16), v, preferred_element_type=F32)
    m_ref[...] = m_new


def _sample_a_kernel(q_ref, kn_ref, vn_ref, kc_ref, vc_ref, knext_ref, vnext_ref, bias_ref, biasn_ref,
                     o_ref, ko_ref, vo_ref, qbd_ref, m_ref, l_ref, acc_ref):
    c = pl.program_id(1)
    last = pl.num_programs(1) - 1
    t_new = kn_ref.shape[0]
    chunk = kc_ref.shape[0]

    @pl.when(c == 0)
    def _():
        qbd_ref[...] = _block_diag_queries(q_ref[...], A_HEADS)
        m_ref[...] = jnp.full(m_ref.shape, MASKED, F32)
        l_ref[...] = jnp.zeros(l_ref.shape, F32)
        acc_ref[...] = jnp.zeros(acc_ref.shape, F32)

    kc = kc_ref[...]
    vc = vc_ref[...]
    s = lax.dot_general(qbd_ref[...], kc.astype(BF16), (((1,), (1,)), ((), ())), preferred_element_type=F32)
    _softmax_step(s + bias_ref[...], vc.astype(BF16), m_ref, l_ref, acc_ref)

    ko_ref[0:chunk - t_new, :] = kc[t_new:, :]
    vo_ref[0:chunk - t_new, :] = vc[t_new:, :]

    @pl.when(c < last)
    def _():
        ko_ref[chunk - t_new:, :] = knext_ref[...]
        vo_ref[chunk - t_new:, :] = vnext_ref[...]

    @pl.when(c == last)
    def _():
        kn = kn_ref[...]
        vn = vn_ref[...]
        ko_ref[chunk - t_new:, :] = kn
        vo_ref[chunk - t_new:, :] = vn
        zeros = jnp.zeros((LANES - t_new, kn.shape[1]), F32)
        kpad = jnp.concatenate([kn, zeros], axis=0).astype(BF16)
        vpad = jnp.concatenate([vn, zeros], axis=0).astype(BF16)
        s_new = lax.dot_general(qbd_ref[...], kpad, (((1,), (1,)), ((), ())), preferred_element_type=F32)
        _softmax_step(s_new + biasn_ref[...], vpad, m_ref, l_ref, acc_ref)
        o_ref[...] = _block_diag_extract(acc_ref[...] / l_ref[...], A_HEADS)


def _sample_attention_a(q, k_new, v_new, cache_k, cache_v, bias, bias_new, *, chunk):
    nb, win, width = cache_k.shape
    t = q.shape[0] // nb
    nc = win // chunk
    rows = A_HEADS * t
    per_chunk = chunk // t
    new_spec = pl.BlockSpec((t, width), lambda b, c: (b, 0))
    cache_spec = pl.BlockSpec((None, chunk, width), lambda b, c: (b, c, 0))
    next_spec = pl.BlockSpec((None, t, width), lambda b, c: (b, jnp.minimum(c + 1, nc - 1) * per_chunk, 0))
    return pl.pallas_call(
        _sample_a_kernel,
        grid=(nb, nc),
        in_specs=[new_spec, new_spec, new_spec, cache_spec, cache_spec, next_spec, next_spec,
                  pl.BlockSpec((None, rows, chunk), lambda b, c: (c, 0, 0)),
                  pl.BlockSpec((rows, LANES), lambda b, c: (0, 0))],
        out_specs=[new_spec, cache_spec, cache_spec],
        out_shape=[jax.ShapeDtypeStruct((nb * t, width), F32),
                   jax.ShapeDtypeStruct(cache_k.shape, F32), jax.ShapeDtypeStruct(cache_v.shape, F32)],
        scratch_shapes=[pltpu.VMEM((rows, width), BF16), pltpu.VMEM((rows, 1), F32), pltpu.VMEM((rows, 1), F32),
                        pltpu.VMEM((rows, width), F32)],
        compiler_params=_params("parallel", "arbitrary"),
        name="sample_attn_a",
    )(q, k_new, v_new, cache_k, cache_v, cache_k, cache_v, bias, bias_new)


def _expand_kv(x):
    first = _pair_select(x, False)
    second = _pair_select(x, True)
    reps = B_HEADS // B_KV_HEADS // 2
    return jnp.concatenate([first] * reps + [second] * reps, axis=1)


def _sample_b_kernel(q_ref, kn_ref, vn_ref, kc_ref, vc_ref, bias_ref, sink_ref, o_ref, ko_ref, vo_ref, seq_ref):
    t_new = kn_ref.shape[0]
    win = kc_ref.shape[0]
    seq_ref[...] = jnp.zeros(seq_ref.shape, F32)
    for idx, (c_ref, n_ref, out_ref) in enumerate(((kc_ref, kn_ref, ko_ref), (vc_ref, vn_ref, vo_ref))):
        seq_ref[idx, 0:win, :] = c_ref[...]
        seq_ref[idx, win:win + t_new, :] = n_ref[...]
        out_ref[...] = seq_ref[idx, t_new:win + t_new, :]
    qbd = _block_diag_queries(q_ref[...], B_HEADS)
    k = _expand_kv(seq_ref[0]).astype(BF16)
    v = _expand_kv(seq_ref[1]).astype(BF16)
    s = lax.dot_general(qbd, k, (((1,), (1,)), ((), ())), preferred_element_type=F32) + bias_ref[...]
    sink = sink_ref[...]
    m = jnp.maximum(jnp.max(s, axis=-1, keepdims=True), sink)
    p = jnp.exp(s - m)
    den = jnp.sum(p, axis=-1, keepdims=True) + jnp.exp(sink - m)
    o = jnp.dot(p.astype(BF16), v, preferred_element_type=F32) / den
    o_ref[...] = _block_diag_extract(o, B_HEADS)


def _sample_attention_b(q, k_new, v_new, cache_k, cache_v, bias, sink_rows):
    nb, win, kvw = cache_k.shape
    t = q.shape[0] // nb
    rows = B_HEADS * t
    q_spec = pl.BlockSpec((t, B_WIDTH), lambda b: (b, 0))
    new_spec = pl.BlockSpec((t, kvw), lambda b: (b, 0))
    cache_spec = pl.BlockSpec((None, win, kvw), lambda b: (b, 0, 0))
    return pl.pallas_call(
        _sample_b_kernel,
        grid=(nb,),
        in_specs=[q_spec, new_spec, new_spec, cache_spec, cache_spec,
                  pl.BlockSpec((rows, 2 * win), lambda b: (0, 0)), pl.BlockSpec((rows, 1), lambda b: (0, 0))],
        out_specs=[q_spec, cache_spec, cache_spec],
        out_shape=[jax.ShapeDtypeStruct(q.shape, F32),
                   jax.ShapeDtypeStruct(cache_k.shape, F32), jax.ShapeDtypeStruct(cache_v.shape, F32)],
        scratch_shapes=[pltpu.VMEM((2, 2 * win, kvw), F32)],
        compiler_params=_params("parallel"),
        name="sample_attn_b",
    )(q, k_new, v_new, cache_k, cache_v, bias, sink_rows)


def _cross_attn_kernel(q_ref, k_ref, v_ref, o_ref):
    scale = MEM_HEAD_DIM ** -0.5
    for h in range(MEM_HEADS):
        cols = slice(h * MEM_HEAD_DIM, (h + 1) * MEM_HEAD_DIM)
        q = q_ref[:, cols].astype(BF16)
        k = k_ref[:, cols].astype(BF16)
        v = v_ref[:, cols].astype(BF16)
        s = lax.dot_general(q, k, (((1,), (1,)), ((), ())), preferred_element_type=F32) * scale
        m = jnp.max(s, axis=-1, keepdims=True)
        e = jnp.exp(s - m)
        p = e / jnp.sum(e, axis=-1, keepdims=True)
        o_ref[:, cols] = jnp.dot(p.astype(BF16), v, preferred_element_type=F32)


def _cross_attention(q, mk, mv, *, tq):
    nb, s, width = q.shape
    mem = mk.shape[1]
    q_spec = pl.BlockSpec((None, tq, width), lambda b, i: (b, i, 0))
    m_spec = pl.BlockSpec((None, mem, width), lambda b, i: (b, 0, 0))
    return pl.pallas_call(
        _cross_attn_kernel,
        grid=(nb, s // tq),
        in_specs=[q_spec, m_spec, m_spec],
        out_specs=q_spec,
        out_shape=jax.ShapeDtypeStruct(q.shape, F32),
        compiler_params=_params("parallel", "arbitrary"),
        name="cross_attn",
    )(q, mk, mv)


def _top_rows(s, ids, k):
    n = s.shape[0]
    row = lax.broadcasted_iota(I32, s.shape, 0).astype(F32)
    vals, picked = [], []
    for _ in range(k):
        m = jnp.max(s, axis=0, keepdims=True)
        pos = jnp.min(jnp.where(s == m, row, float(n)), axis=0, keepdims=True)
        hit = row == pos
        vals.append(m)
        picked.append(jnp.max(jnp.where(hit, ids, -1.0), axis=0, keepdims=True))
        s = jnp.where(hit, -jnp.inf, s)
    return jnp.concatenate(vals, axis=0), jnp.concatenate(picked, axis=0)


def _peer_topk_kernel(q_ref, k1_ref, k2_ref, eidx_ref, gate_ref):
    half = PEER_DKEY // 2
    q = q_ref[...]
    tb = q.shape[0]
    key_ids = lax.broadcasted_iota(I32, (PEER_NKEYS, tb), 0).astype(F32)
    nt = (((1,), (1,)), ((), ()))
    s1 = lax.dot_general(k1_ref[...], q[:, :half].astype(BF16), nt, preferred_element_type=F32)
    s2 = lax.dot_general(k2_ref[...], q[:, half:].astype(BF16), nt, preferred_element_type=F32)
    v1, i1 = _top_rows(s1, key_ids, PEER_TOPK)
    v2, i2 = _top_rows(s2, key_ids, PEER_TOPK)
    cand = jnp.concatenate([v1[a:a + 1] + v2 for a in range(PEER_TOPK)], axis=0)
    cidx = jnp.concatenate([i1[a:a + 1] * PEER_NKEYS + i2 for a in range(PEER_TOPK)], axis=0)
    best, eidx = _top_rows(cand, cidx, PEER_TOPK)
    e = jnp.exp(best - best[0:1])
    gate_ref[...] = e / jnp.sum(e, axis=0, keepdims=True)
    eidx_ref[...] = eidx.astype(I32)


def _peer_topk(q, sub_k1, sub_k2, *, tb):
    m = q.shape[0]
    rows = PEER_HEADS * PEER_TOPK
    key_spec = pl.BlockSpec((PEER_NKEYS, PEER_DKEY // 2), lambda i, h: (0, 0))
    out_spec = pl.BlockSpec((PEER_TOPK, tb), lambda i, h: (h, i))
    return pl.pallas_call(
        _peer_topk_kernel,
        grid=(m // tb, PEER_HEADS),
        in_specs=[pl.BlockSpec((tb, PEER_DKEY), lambda i, h: (i, h)), key_spec, key_spec],
        out_specs=[out_spec, out_spec],
        out_shape=[jax.ShapeDtypeStruct((rows, m), I32), jax.ShapeDtypeStruct((rows, m), F32)],
        compiler_params=_params("parallel", "arbitrary"),
        name="peer_topk",
    )(q, sub_k1, sub_k2)


def _tree_sum(terms):
    while len(terms) > 1:
        terms = [terms[j] + terms[j + 1] for j in range(0, len(terms) - 1, 2)] + \
            ([terms[-1]] if len(terms) % 2 else [])
    return terms[0]


def _pack_expert_tables(expert_u, expert_v):
    ub = lax.bitcast_convert_type(expert_u.astype(BF16), jnp.uint16).astype(jnp.uint32)
    vb = lax.bitcast_convert_type(expert_v.astype(BF16), jnp.uint16).astype(jnp.uint32)
    n_exp, d = expert_u.shape
    return ((ub << 16) | vb).reshape(n_exp, d // LANES, LANES)


def _peer_expert_kernel(idx_ref, idxn_ref, gate_ref, y_ref, lnx_ref, lnf_ref, tab_hbm, o_ref, buf0, buf1, rows_ref,
                        cols_ref, sem, *, tg, n_sel):
    i = pl.program_id(0)
    last = pl.num_programs(0) - 1
    bufs = (buf0, buf1)
    n_tiles = buf0.shape[1]
    high = jnp.uint32(0xFFFF0000)

    def slab_copy(ids, row, s, t, k):
        return pltpu.make_async_copy(tab_hbm.at[ids[row, k]], bufs[s].at[t, :, k, :], sem.at[s])

    def wait_slot(s):
        pltpu.make_async_copy(bufs[s], bufs[s], sem.at[s]).wait()

    @pl.when(i == 0)
    def _():
        def prime(t, carry):
            for k in range(n_sel):
                slab_copy(idx_ref, t, 0, t, k).start()
            return carry
        lax.fori_loop(0, tg, prime, 0)

    def group(s, ids_next, next_row0):
        rows = slice(s * tg, (s + 1) * tg)
        wait_slot(s)
        y = y_ref[rows, :]
        x = y * lax.rsqrt(jnp.mean(y * y, axis=-1, keepdims=True) + EPS) * lnx_ref[...]
        for t in range(tg):
            rows_ref[0, t] = x[t:t + 1, :]
            rows_ref[1, t] = y[t:t + 1, :]
        token = lax.broadcasted_iota(I32, (n_sel, tg), 1)
        half = n_sel // 2

        def dot_pass(t, carry):
            for k in range(half):
                slab_copy(ids_next, next_row0 + t, 1 - s, t, k).start()
            part = _tree_sum([lax.bitcast_convert_type(bufs[s][t, c] & high, F32)
                              * rows_ref[0, t, :, c * LANES:(c + 1) * LANES] for c in range(n_tiles)])
            cols_ref[t] = jnp.broadcast_to(jnp.sum(part, axis=-1, keepdims=True), (n_sel, LANES))
            return carry

        lax.fori_loop(0, tg, dot_pass, 0)
        pre = jnp.zeros((n_sel, tg), F32)
        for t in range(tg):
            pre = jnp.where(token == t, cols_ref[t][:, 0:tg], pre)
        act = 0.5 * pre * (1.0 + lax.erf(pre * (2.0 ** -0.5)))
        w = gate_ref[s] * act
        for t in range(tg):
            cols_ref[t] = jnp.broadcast_to(w[:, t:t + 1], (n_sel, LANES))

        def mix_pass(t, carry):
            for k in range(half, n_sel):
                slab_copy(ids_next, next_row0 + t, 1 - s, t, k).start()
            wt = cols_ref[t]
            sums = []
            for c in range(n_tiles):
                prod = lax.bitcast_convert_type(bufs[s][t, c] << 16, F32) * wt
                groups = [prod[g * SUBLANES:(g + 1) * SUBLANES, :] for g in range(n_sel // SUBLANES)]
                sums.append(jnp.sum(_tree_sum(groups), axis=0, keepdims=True))
            out = rows_ref[1, t] + jnp.concatenate(sums, axis=1)
            ms = jnp.mean(out * out, axis=-1, keepdims=True)
            rows_ref[2, t] = out * lax.rsqrt(ms + EPS) * lnf_ref[...]
            return carry

        lax.fori_loop(0, tg, mix_pass, 0)
        for t in range(tg):
            o_ref[s * tg + t:s * tg + t + 1, :] = rows_ref[2, t]

    group(0, idx_ref, tg)
    group(1, idxn_ref, 0)

    @pl.when(i == last)
    def _():
        wait_slot(0)


def _peer_experts(eidx, gate_cols, y, ln_ffn, ln_final, table, *, tg):
    m, d = y.shape
    n_sel = eidx.shape[1]
    n_tiles = table.shape[1]
    assert m % (2 * tg) == 0 and n_tiles * LANES == d
    n_steps = m // (2 * tg)
    row_spec = pl.BlockSpec((2 * tg, d), lambda i: (i, 0))
    return pl.pallas_call(
        functools.partial(_peer_expert_kernel, tg=tg, n_sel=n_sel),
        grid=(n_steps,),
        in_specs=[
            pl.BlockSpec((2 * tg, n_sel), lambda i: (i, 0), memory_space=pltpu.SMEM),
            pl.BlockSpec((2 * tg, n_sel), lambda i: (jnp.minimum(i + 1, n_steps - 1), 0), memory_space=pltpu.SMEM),
            pl.BlockSpec((2, n_sel, tg), lambda i: (i, 0, 0)),
            row_spec,
            pl.BlockSpec((1, d), lambda i: (0, 0)),
            pl.BlockSpec((1, d), lambda i: (0, 0)),
            pl.BlockSpec(memory_space=pl.ANY),
        ],
        out_specs=row_spec,
        out_shape=jax.ShapeDtypeStruct((m, d), F32),
        scratch_shapes=[pltpu.VMEM((tg, n_tiles, n_sel, LANES), jnp.uint32),
                        pltpu.VMEM((tg, n_tiles, n_sel, LANES), jnp.uint32), pltpu.VMEM((3, tg, 1, d), F32),
                        pltpu.VMEM((tg, n_sel, LANES), F32), pltpu.SemaphoreType.DMA((2,))],
        compiler_params=_params("arbitrary"),
        name="peer_experts",
    )(eidx, eidx, gate_cols, y, ln_ffn.reshape(1, d), ln_final.reshape(1, d), table)


def _channel_mixers(y, mk, mv, nb, p, *, tm, tq, tb, tg):
    m, d = y.shape
    q = _norm_matmul([y], [p["ln_cross"]], p["w_cq"], tm=tm, tn=MEM_WIDTH, name="cross_q_proj")
    o = _cross_attention(q.reshape(nb, m // nb, MEM_WIDTH), mk, mv, tq=tq).reshape(m, MEM_WIDTH)
    y = _norm_matmul([o], None, p["w_co"], residual=y, tm=tm, tn=512, name="cross_out_proj")
    pq = _norm_matmul([y], [p["ln_ffn"]], p["w_pq"], tm=tm, tn=512, name="peer_query_proj")
    eidx_t, gate_t = _peer_topk(pq, p["sub_k1"], p["sub_k2"], tb=tb)
    n_sel = eidx_t.shape[0]
    gate_cols = gate_t.reshape(n_sel, m // tg, tg).transpose(1, 0, 2)
    return _peer_experts(eidx_t.T, gate_cols, y, p["ln_ffn"], p["ln_final"], p["expert_table"], tg=tg)


def kernel(x_prompt, x_sample, cache_a_k, cache_a_v, cache_b_k, cache_b_v, cache_mem_k, cache_mem_v, mem_prompt, ln_mix, w_in, ln_a_out, ln_b_out, w_out, b_sinks, rel_bias, ln_cross, ln_mem, w_cq, w_ckv, w_co, ln_ffn, w_pq, sub_keys_1, sub_keys_2, expert_u, expert_v, ln_final):
    depth = w_in.shape[0]
    assert depth == 1, "the caches are laid out for a single layer"
    batch, seq, d = x_prompt.shape
    dec_batch, dec_seq, _ = x_sample.shape
    a_win = cache_a_k.shape[2]
    b_win = cache_b_k.shape[2]
    mem_len = mem_prompt.shape[1]
    assert seq == a_win == A_WIN and b_win == B_WIN and seq % BLK == 0
    l = 0
    split = 3 * A_WIDTH + B_WIDTH
    w_in_main = w_in[l, :, :split].astype(BF16)
    w_in_kvb = w_in[l, :, split:].astype(BF16)
    p = dict(ln_cross=ln_cross[l], w_cq=w_cq[l].astype(BF16), w_co=w_co[l].astype(BF16), ln_ffn=ln_ffn[l],
             w_pq=w_pq[l].astype(BF16), sub_k1=sub_keys_1[l].astype(BF16), sub_k2=sub_keys_2[l].astype(BF16),
             ln_final=ln_final, expert_table=_pack_expert_tables(expert_u[l], expert_v[l]))
    w_out_bf = w_out[l].astype(BF16)
    w_ckv_bf = w_ckv[l].astype(BF16)
    sinks = b_sinks[l].astype(F32)

    def project(x2d, tm):
        main = _norm_matmul([x2d], [ln_mix[l]], w_in_main, tm=tm, tn=512, out_split=4, name="in_proj")
        kvb = _norm_matmul([x2d], [ln_mix[l]], w_in_kvb, tm=tm, tn=B_KV_WIDTH, out_split=2, name="in_proj_kvb")
        return main, kvb

    def merge(oa, ob, resid, tm):
        return _norm_matmul([oa, ob], [ln_a_out[l], ln_b_out[l]], w_out_bf, residual=resid, tm=tm, tn=512,
                            name="mixer_out_proj")

    xp = x_prompt.reshape(batch * seq, d)
    main, kvb = project(xp, 512)
    n_delta_a = seq // BLK + 1
    bias_a = _prompt_bias_tiles(rel_bias, 0, _mixer_a_multiplicity, n_delta_a)
    window_b = lambda dist: ((dist >= 0) & (dist <= B_WIN)).astype(np.int32)
    bias_b = _prompt_bias_tiles(rel_bias, A_HEADS, window_b, B_WIN // BLK + 1)
    oa = _prompt_attention((main, 0), (main, 1), (main, 2), bias_a, None, batch=batch, seq=seq, gqa_pairs=0,
                           chunk_tiles=4)
    ob = _prompt_attention((main, 3), (kvb, 0), (kvb, 1), bias_b, sinks, batch=batch, seq=seq,
                           gqa_pairs=B_HEADS // B_KV_HEADS // 2, chunk_tiles=B_WIN // BLK + 1)
    yp = merge(oa, ob, xp, 512)
    mem_kv = _norm_matmul([mem_prompt.reshape(batch * mem_len, d)], [ln_mem[l]], w_ckv_bf, tm=512, tn=MEM_WIDTH,
                          out_split=2, name="mem_kv_proj")
    mk = mem_kv[0].reshape(batch, mem_len, MEM_WIDTH)
    mv = mem_kv[1].reshape(batch, mem_len, MEM_WIDTH)
    y_prompt = _channel_mixers(yp, mk, mv, batch, p, tm=512, tq=512, tb=128, tg=8)

    xs = x_sample.reshape(dec_batch * dec_seq, d)
    main_s, kvb_s = project(xs, 512)
    chunk = 512
    key_pos = np.arange(a_win)
    t_pos = np.arange(dec_seq)
    dist_cache = a_win + t_pos[:, None] - key_pos[None, :]
    bias_sa = _distance_bias(rel_bias, dist_cache, 0, A_HEADS, _mixer_a_multiplicity(dist_cache))
    bias_sa = bias_sa.reshape(A_HEADS * dec_seq, a_win // chunk, chunk).transpose(1, 0, 2)
    dist_new = t_pos[:, None] - np.arange(LANES)[None, :]
    mult_new = np.where(np.arange(LANES)[None, :] < dec_seq, _mixer_a_multiplicity(dist_new), 0)
    bias_sa_new = _distance_bias(rel_bias, dist_new, 0, A_HEADS, mult_new).reshape(A_HEADS * dec_seq, LANES)
    oa_s, aks, avs = _sample_attention_a(
        main_s[0], main_s[1], main_s[2],
        cache_a_k[l].reshape(dec_batch, a_win, A_WIDTH), cache_a_v[l].reshape(dec_batch, a_win, A_WIDTH),
        bias_sa, bias_sa_new, chunk=chunk)
    seq_pos = np.arange(2 * b_win)
    dist_b = b_win + t_pos[:, None] - seq_pos[None, :]
    mult_b = ((dist_b >= 0) & (dist_b <= B_WIN) & (seq_pos[None, :] < b_win + dec_seq)).astype(np.int32)
    bias_sb = _distance_bias(rel_bias, dist_b, A_HEADS, B_HEADS, mult_b).reshape(B_HEADS * dec_seq, 2 * b_win)
    sink_rows = jnp.repeat(sinks, dec_seq).reshape(B_HEADS * dec_seq, 1)
    ob_s, bks, bvs = _sample_attention_b(
        main_s[3], kvb_s[0], kvb_s[1],
        cache_b_k[l].reshape(dec_batch, b_win, B_KV_WIDTH), cache_b_v[l].reshape(dec_batch, b_win, B_KV_WIDTH),
        bias_sb, sink_rows)
    ys = merge(oa_s, ob_s, xs, 512)
    mk_s = cache_mem_k[l].reshape(dec_batch, mem_len, MEM_WIDTH)
    mv_s = cache_mem_v[l].reshape(dec_batch, mem_len, MEM_WIDTH)
    y_sample = _channel_mixers(ys, mk_s, mv_s, dec_batch, p, tm=512, tq=dec_seq, tb=128, tg=8)

    def heads(x, *shape):
        return x.reshape(1, *shape)

    return (y_prompt.reshape(batch, seq, d), y_sample.reshape(dec_batch, dec_seq, d),
            heads(main[1], batch, seq, A_HEADS, HEAD_DIM), heads(main[2], batch, seq, A_HEADS, HEAD_DIM),
            heads(kvb[0].reshape(batch, seq, B_KV_WIDTH)[:, seq - b_win:], batch, b_win, B_KV_HEADS, HEAD_DIM),
            heads(kvb[1].reshape(batch, seq, B_KV_WIDTH)[:, seq - b_win:], batch, b_win, B_KV_HEADS, HEAD_DIM),
            heads(mk, batch, mem_len, MEM_HEADS, MEM_HEAD_DIM), heads(mv, batch, mem_len, MEM_HEADS, MEM_HEAD_DIM),
            heads(aks, dec_batch, a_win, A_HEADS, HEAD_DIM), heads(avs, dec_batch, a_win, A_HEADS, HEAD_DIM),
            heads(bks, dec_batch, b_win, B_KV_HEADS, HEAD_DIM), heads(bvs, dec_batch, b_win, B_KV_HEADS, HEAD_DIM))
```

```python
import functools
import math

import numpy as np
import jax
import jax.numpy as jnp
from jax import lax
from jax.experimental import pallas as pl
from jax.experimental.pallas import tpu as pltpu

F32 = jnp.float32
BF16 = jnp.bfloat16
I32 = jnp.int32

EPS = 1e-6
MASKED = -1e30

LANES = 128
SUBLANES = 8
VMEM_LIMIT = 48 * 1024 * 1024

HEAD_DIM = 64
A_HEADS = 16
A_PATTERNS = ((128, 1), (512, 4), (2048, 16))
A_WIN = 2048
B_HEADS = 16
B_KV_HEADS = 2
B_WIN = 128
A_WIDTH = A_HEADS * HEAD_DIM
B_WIDTH = B_HEADS * HEAD_DIM
B_KV_WIDTH = B_KV_HEADS * HEAD_DIM
ATTN_SCALE = HEAD_DIM ** -0.5
N_BUCKETS = 32
MAX_EXACT = N_BUCKETS // 2
MAX_DISTANCE = A_WIN
MEM_HEADS = 4
MEM_HEAD_DIM = 128
MEM_WIDTH = MEM_HEADS * MEM_HEAD_DIM
PEER_HEADS = 8
PEER_NKEYS = 128
PEER_DKEY = 256
PEER_TOPK = 16
BLK = 128


def _params(*semantics, flags=None):
    return pltpu.CompilerParams(dimension_semantics=semantics, vmem_limit_bytes=VMEM_LIMIT, flags=flags)


def _norm_matmul_kernel(*refs, n_groups, norm, residual):
    xs = refs[:n_groups]
    pos = n_groups
    gs = refs[pos:pos + n_groups] if norm else ()
    pos += n_groups if norm else 0
    w_ref = refs[pos]
    pos += 1
    r_ref = refs[pos] if residual else None
    pos += 1 if residual else 0
    o_ref, xn_ref = refs[pos], refs[pos + 1]

    @pl.when(pl.program_id(1) == 0)
    def _():
        off = 0
        for gi in range(n_groups):
            x = xs[gi][...]
            if norm:
                ms = jnp.mean(x * x, axis=-1, keepdims=True)
                x = x * lax.rsqrt(ms + EPS) * gs[gi][...]
            width = x.shape[-1]
            xn_ref[:, off:off + width] = x.astype(BF16)
            off += width

    acc = jnp.dot(xn_ref[...], w_ref[...], preferred_element_type=F32)
    if residual:
        acc = acc + r_ref[...]
    o_ref[...] = acc


def _norm_matmul(xs, gains, w, residual=None, *, tm, tn, out_split=1, name="norm_matmul"):
    m = xs[0].shape[0]
    k_total, n = w.shape
    assert sum(x.shape[1] for x in xs) == k_total and m % tm == 0 and n % (tn * out_split) == 0
    norm = gains is not None
    nj_per = n // out_split // tn
    in_specs = [pl.BlockSpec((tm, x.shape[1]), lambda i, j: (i, 0)) for x in xs]
    args = list(xs)
    if norm:
        in_specs += [pl.BlockSpec((1, g.shape[-1]), lambda i, j: (0, 0)) for g in gains]
        args += [g.reshape(1, -1) for g in gains]
    in_specs.append(pl.BlockSpec((k_total, tn), lambda i, j: (0, j)))
    args.append(w)
    if residual is not None:
        in_specs.append(pl.BlockSpec((tm, tn), lambda i, j: (i, j)))
        args.append(residual)
    if out_split == 1:
        out_shape = jax.ShapeDtypeStruct((m, n), F32)
        out_spec = pl.BlockSpec((tm, tn), lambda i, j: (i, j))
    else:
        out_shape = jax.ShapeDtypeStruct((out_split, m, n // out_split), F32)
        out_spec = pl.BlockSpec((None, tm, tn), lambda i, j: (j // nj_per, i, j % nj_per))
    return pl.pallas_call(
        functools.partial(_norm_matmul_kernel, n_groups=len(xs), norm=norm, residual=residual is not None),
        grid=(m // tm, n // tn),
        in_specs=in_specs,
        out_specs=out_spec,
        out_shape=out_shape,
        scratch_shapes=[pltpu.VMEM((tm, k_total), BF16)],
        compiler_params=_params("parallel", "arbitrary"),
        name=name,
    )(*args)


def _rel_bucket(dist):
    dist = np.maximum(np.asarray(dist), 0)
    ratio = np.log(np.maximum(dist, 1) / MAX_EXACT) / math.log(MAX_DISTANCE / MAX_EXACT)
    large = np.minimum(MAX_EXACT + (ratio * (N_BUCKETS - MAX_EXACT)).astype(np.int32), N_BUCKETS - 1)
    return np.where(dist < MAX_EXACT, dist, large).astype(np.int32)


def _mixer_a_multiplicity(dist):
    dist = np.asarray(dist)
    mult = np.zeros(dist.shape, np.int32)
    for window, dilation in A_PATTERNS:
        mult += ((dist >= 0) & (dist <= window) & (dist % dilation == 0)).astype(np.int32)
    return mult


def _distance_bias(rel_bias, dist, head_lo, n_heads, mult):
    table = rel_bias[:, head_lo:head_lo + n_heads].astype(F32).T
    vals = table[:, _rel_bucket(dist)]
    logm = np.log(np.maximum(mult, 1)).astype(np.float32)
    return jnp.where(jnp.asarray(mult > 0)[None], vals + jnp.asarray(logm)[None], MASKED)


def _pair_select(x, which):
    lane_head = lax.broadcasted_iota(I32, x.shape, 1) // HEAD_DIM
    swapped = pltpu.roll(x, HEAD_DIM, axis=1)
    return jnp.where(lane_head == which, x, swapped)


def _prompt_attn_kernel(*refs, n_delta, chunk_tiles, n_chunks, gqa_pairs, has_sink):
    if has_sink:
        q_ref, k_ref, v_ref, bias_ref, sink_ref, o_ref, m_ref, l_ref, acc_ref, kb_ref, vb_ref = refs
    else:
        q_ref, k_ref, v_ref, bias_ref, o_ref, m_ref, l_ref, acc_ref, kb_ref, vb_ref = refs
        sink_ref = None
    hp = pl.program_id(1)
    qb = pl.program_id(2)
    keys_on_lanes = n_chunks > 1

    @pl.when(qb == 0)
    def _():
        k = k_ref[...]
        v = v_ref[...]
        if gqa_pairs:
            kv_head = hp // gqa_pairs
            k = _pair_select(k, kv_head)
            v = _pair_select(v, kv_head)
        kb_ref[...] = (k.T if keys_on_lanes else k).astype(BF16)
        vb_ref[...] = v.astype(BF16)

    lane = lax.broadcasted_iota(I32, (BLK, LANES), 1)
    left = lane < HEAD_DIM
    q = q_ref[...] * ATTN_SCALE
    q2 = jnp.concatenate([jnp.where(left, q, 0.0), jnp.where(left, 0.0, q)], axis=0).astype(BF16)
    if has_sink:
        m_ref[...] = jnp.concatenate([jnp.full((BLK, 1), sink_ref[2 * hp], F32),
                                      jnp.full((BLK, 1), sink_ref[2 * hp + 1], F32)], axis=0)
        l_ref[...] = jnp.ones(l_ref.shape, F32)
    else:
        m_ref[...] = jnp.full(m_ref.shape, MASKED, F32)
        l_ref[...] = jnp.zeros(l_ref.shape, F32)
    acc_ref[...] = jnp.zeros(acc_ref.shape, F32)

    def chunk(start_tile):
        start = pl.multiple_of(start_tile * BLK, BLK)
        v = vb_ref[pl.ds(start, chunk_tiles * BLK), :]
        if keys_on_lanes:
            k = kb_ref[:, start_tile * BLK:(start_tile + chunk_tiles) * BLK]
        else:
            k = kb_ref[pl.ds(start, chunk_tiles * BLK), :]
        if keys_on_lanes:
            s = jnp.dot(q2, k, preferred_element_type=F32)
        else:
            s = lax.dot_general(q2, k, (((1,), (1,)), ((), ())), preferred_element_type=F32)
        rows = []
        for h2 in range(2):
            tiles = []
            for j in range(chunk_tiles):
                delta = qb - (start_tile + j)
                tiles.append(bias_ref[h2, jnp.where((delta >= 0) & (delta < n_delta), delta, n_delta)])
            rows.append(jnp.concatenate(tiles, axis=1))
        s = s + jnp.concatenate(rows, axis=0)
        m_old = m_ref[...]
        m_new = jnp.maximum(m_old, jnp.max(s, axis=-1, keepdims=True))
        alpha = jnp.exp(m_old - m_new)
        p = jnp.exp(s - m_new)
        l_ref[...] = alpha * l_ref[...] + jnp.sum(p, axis=-1, keepdims=True)
        m_ref[...] = m_new
        acc_ref[...] = alpha * acc_ref[...] + jnp.dot(p.astype(BF16), v, preferred_element_type=F32)

    if n_chunks == 1:
        chunk(jnp.maximum(qb - (chunk_tiles - 1), 0))
    else:
        for c in range(n_chunks):
            pl.when(c * chunk_tiles <= qb)(functools.partial(chunk, c * chunk_tiles))
    o = acc_ref[...] / l_ref[...]
    o_ref[...] = jnp.where(left, o[0:BLK], o[BLK:2 * BLK])


def _prompt_attention(q_src, k_src, v_src, bias, sinks, *, batch, seq, gqa_pairs, chunk_tiles):
    n_pairs = A_WIDTH // LANES
    n_delta = bias.shape[1] - 1
    nq = seq // BLK
    n_chunks = 1 if n_delta <= chunk_tiles else nq // chunk_tiles
    assert nq % chunk_tiles == 0
    (q_arr, q_idx), (k_arr, k_idx), (v_arr, v_idx) = q_src, k_src, v_src
    kv_map = (lambda which: (lambda b, hp, qb: (which, b, 0))) if gqa_pairs else \
        (lambda which: (lambda b, hp, qb: (which, b, hp)))
    in_specs = [
        pl.BlockSpec((None, BLK, LANES), lambda b, hp, qb: (q_idx, b * nq + qb, hp)),
        pl.BlockSpec((None, seq, LANES), kv_map(k_idx)),
        pl.BlockSpec((None, seq, LANES), kv_map(v_idx)),
        pl.BlockSpec((2, n_delta + 1, BLK, BLK), lambda b, hp, qb: (hp, 0, 0, 0)),
    ]
    args = [q_arr, k_arr, v_arr, bias]
    if sinks is not None:
        in_specs.append(pl.BlockSpec(memory_space=pltpu.SMEM))
        args.append(sinks)
    return pl.pallas_call(
        functools.partial(_prompt_attn_kernel, n_delta=n_delta, chunk_tiles=chunk_tiles, n_chunks=n_chunks,
                          gqa_pairs=gqa_pairs, has_sink=sinks is not None),
        grid=(batch, n_pairs, nq),
        in_specs=in_specs,
        out_specs=pl.BlockSpec((BLK, LANES), lambda b, hp, qb: (b * nq + qb, hp)),
        out_shape=jax.ShapeDtypeStruct((batch * seq, A_WIDTH), F32),
        scratch_shapes=[pltpu.VMEM((2 * BLK, 1), F32), pltpu.VMEM((2 * BLK, 1), F32), pltpu.VMEM((2 * BLK, LANES), F32),
                        pltpu.VMEM((LANES, seq) if n_chunks > 1 else (seq, LANES), BF16),
                        pltpu.VMEM((seq, LANES), BF16)],
        compiler_params=_params("parallel", "parallel", "arbitrary"),
        name="prompt_attn_b" if gqa_pairs else "prompt_attn_a",
    )(*args)


def _prompt_bias_tiles(rel_bias, head_lo, mult_fn, n_delta):
    length = (n_delta + 1) * BLK + BLK - 1
    dist = np.arange(length) - (BLK - 1)
    mult = mult_fn(dist)
    assert not mult[dist > (n_delta - 1) * BLK].any(), "the reach must end before the masked tile"
    by_dist = _distance_bias(rel_bias, dist, head_lo, 16, mult)
    flipped = by_dist[:, ::-1]
    starts = (length - BLK - np.arange(n_delta + 1)[:, None] * BLK - np.arange(BLK)[None, :]).reshape(-1)
    window = lambda row, start: lax.dynamic_slice_in_dim(row, start, BLK)
    tiles = jax.vmap(jax.vmap(window, in_axes=(None, 0)), in_axes=(0, None))(flipped, jnp.asarray(starts))
    return tiles.reshape(16, n_delta + 1, BLK, BLK)


def _block_diag_queries(q, n_heads):
    t, width = q.shape
    rows = n_heads * t
    tiled = jnp.broadcast_to(q[None], (n_heads, t, width)).reshape(rows, width)
    row_head = lax.broadcasted_iota(I32, (rows, width), 0) // t
    lane_head = lax.broadcasted_iota(I32, (rows, width), 1) // (width // n_heads)
    return jnp.where(row_head == lane_head, tiled * ATTN_SCALE, 0.0).astype(BF16)


def _block_diag_extract(o, n_heads):
    rows, width = o.shape
    t = rows // n_heads
    row_head = lax.broadcasted_iota(I32, (rows, width), 0) // t
    lane_head = lax.broadcasted_iota(I32, (rows, width), 1) // (width // n_heads)
    return jnp.sum(jnp.where(row_head == lane_head, o, 0.0).reshape(n_heads, t, width), axis=0)


def _softmax_step(s, v, m_ref, l_ref, acc_ref):
    m_old = m_ref[...]
    m_new = jnp.maximum(m_old, jnp.max(s, axis=-1, keepdims=True))
    alpha = jnp.exp(m_old - m_new)
    p = jnp.exp(s - m_new)
    l_ref[...] = alpha * l_ref[...] + jnp.sum(p, axis=-1, keepdims=True)
    acc_ref[...] = alpha * acc_ref[...] + jnp.dot(p.astype(BF16), v, preferred_element_type=F32)
    m_ref[...] = m_new


def _sample_a_kernel(q_ref, kn_ref, vn_ref, kc_ref, vc_ref, knext_ref, vnext_ref, bias_ref, biasn_ref,
                     o_ref, ko_ref, vo_ref, qbd_ref, m_ref, l_ref, acc_ref):
    c = pl.program_id(1)
    last = pl.num_programs(1) - 1
    t_new = kn_ref.shape[0]
    chunk = kc_ref.shape[0]

    @pl.when(c == 0)
    def _():
        qbd_ref[...] = _block_diag_queries(q_ref[...], A_HEADS)
        m_ref[...] = jnp.full(m_ref.shape, MASKED, F32)
        l_ref[...] = jnp.zeros(l_ref.shape, F32)
        acc_ref[...] = jnp.zeros(acc_ref.shape, F32)

    kc = kc_ref[...]
    vc = vc_ref[...]
    s = lax.dot_general(qbd_ref[...], kc.astype(BF16), (((1,), (1,)), ((), ())), preferred_element_type=F32)
    _softmax_step(s + bias_ref[...], vc.astype(BF16), m_ref, l_ref, acc_ref)

    ko_ref[0:chunk - t_new, :] = kc[t_new:, :]
    vo_ref[0:chunk - t_new, :] = vc[t_new:, :]

    @pl.when(c < last)
    def _():
        ko_ref[chunk - t_new:, :] = knext_ref[...]
        vo_ref[chunk - t_new:, :] = vnext_ref[...]

    @pl.when(c == last)
    def _():
        kn = kn_ref[...]
        vn = vn_ref[...]
        ko_ref[chunk - t_new:, :] = kn
        vo_ref[chunk - t_new:, :] = vn
        zeros = jnp.zeros((LANES - t_new, kn.shape[1]), F32)
        kpad = jnp.concatenate([kn, zeros], axis=0).astype(BF16)
        vpad = jnp.concatenate([vn, zeros], axis=0).astype(BF16)
        s_new = lax.dot_general(qbd_ref[...], kpad, (((1,), (1,)), ((), ())), preferred_element_type=F32)
        _softmax_step(s_new + biasn_ref[...], vpad, m_ref, l_ref, acc_ref)
        o_ref[...] = _block_diag_extract(acc_ref[...] / l_ref[...], A_HEADS)


def _sample_attention_a(q, k_new, v_new, cache_k, cache_v, bias, bias_new, *, chunk):
    nb, win, width = cache_k.shape
    t = q.shape[0] // nb
    nc = win // chunk
    rows = A_HEADS * t
    per_chunk = chunk // t
    new_spec = pl.BlockSpec((t, width), lambda b, c: (b, 0))
    cache_spec = pl.BlockSpec((None, chunk, width), lambda b, c: (b, c, 0))
    next_spec = pl.BlockSpec((None, t, width), lambda b, c: (b, jnp.minimum(c + 1, nc - 1) * per_chunk, 0))
    return pl.pallas_call(
        _sample_a_kernel,
        grid=(nb, nc),
        in_specs=[new_spec, new_spec, new_spec, cache_spec, cache_spec, next_spec, next_spec,
                  pl.BlockSpec((None, rows, chunk), lambda b, c: (c, 0, 0)),
                  pl.BlockSpec((rows, LANES), lambda b, c: (0, 0))],
        out_specs=[new_spec, cache_spec, cache_spec],
        out_shape=[jax.ShapeDtypeStruct((nb * t, width), F32),
                   jax.ShapeDtypeStruct(cache_k.shape, F32), jax.ShapeDtypeStruct(cache_v.shape, F32)],
        scratch_shapes=[pltpu.VMEM((rows, width), BF16), pltpu.VMEM((rows, 1), F32), pltpu.VMEM((rows, 1), F32),
                        pltpu.VMEM((rows, width), F32)],
        compiler_params=_params("parallel", "arbitrary"),
        name="sample_attn_a",
    )(q, k_new, v_new, cache_k, cache_v, cache_k, cache_v, bias, bias_new)


def _expand_kv(x):
    first = _pair_select(x, False)
    second = _pair_select(x, True)
    reps = B_HEADS // B_KV_HEADS // 2
    return jnp.concatenate([first] * reps + [second] * reps, axis=1)


def _sample_b_kernel(q_ref, kn_ref, vn_ref, kc_ref, vc_ref, bias_ref, sink_ref, o_ref, ko_ref, vo_ref, seq_ref):
    t_new = kn_ref.shape[0]
    win = kc_ref.shape[0]
    seq_ref[...] = jnp.zeros(seq_ref.shape, F32)
    for idx, (c_ref, n_ref, out_ref) in enumerate(((kc_ref, kn_ref, ko_ref), (vc_ref, vn_ref, vo_ref))):
        seq_ref[idx, 0:win, :] = c_ref[...]
        seq_ref[idx, win:win + t_new, :] = n_ref[...]
        out_ref[...] = seq_ref[idx, t_new:win + t_new, :]
    qbd = _block_diag_queries(q_ref[...], B_HEADS)
    k = _expand_kv(seq_ref[0]).astype(BF16)
    v = _expand_kv(seq_ref[1]).astype(BF16)
    s = lax.dot_general(qbd, k, (((1,), (1,)), ((), ())), preferred_element_type=F32) + bias_ref[...]
    sink = sink_ref[...]
    m = jnp.maximum(jnp.max(s, axis=-1, keepdims=True), sink)
    p = jnp.exp(s - m)
    den = jnp.sum(p, axis=-1, keepdims=True) + jnp.exp(sink - m)
    o = jnp.dot(p.astype(BF16), v, preferred_element_type=F32) / den
    o_ref[...] = _block_diag_extract(o, B_HEADS)


def _sample_attention_b(q, k_new, v_new, cache_k, cache_v, bias, sink_rows):
    nb, win, kvw = cache_k.shape
    t = q.shape[0] // nb
    rows = B_HEADS * t
    q_spec = pl.BlockSpec((t, B_WIDTH), lambda b: (b, 0))
    new_spec = pl.BlockSpec((t, kvw), lambda b: (b, 0))
    cache_spec = pl.BlockSpec((None, win, kvw), lambda b: (b, 0, 0))
    return pl.pallas_call(
        _sample_b_kernel,
        grid=(nb,),
        in_specs=[q_spec, new_spec, new_spec, cache_spec, cache_spec,
                  pl.BlockSpec((rows, 2 * win), lambda b: (0, 0)), pl.BlockSpec((rows, 1), lambda b: (0, 0))],
        out_specs=[q_spec, cache_spec, cache_spec],
        out_shape=[jax.ShapeDtypeStruct(q.shape, F32),
                   jax.ShapeDtypeStruct(cache_k.shape, F32), jax.ShapeDtypeStruct(cache_v.shape, F32)],
        scratch_shapes=[pltpu.VMEM((2, 2 * win, kvw), F32)],
        compiler_params=_params("parallel"),
        name="sample_attn_b",
    )(q, k_new, v_new, cache_k, cache_v, bias, sink_rows)


def _cross_attn_kernel(q_ref, k_ref, v_ref, o_ref):
    scale = MEM_HEAD_DIM ** -0.5
    for h in range(MEM_HEADS):
        cols = slice(h * MEM_HEAD_DIM, (h + 1) * MEM_HEAD_DIM)
        q = q_ref[:, cols].astype(BF16)
        k = k_ref[:, cols].astype(BF16)
        v = v_ref[:, cols].astype(BF16)
        s = lax.dot_general(q, k, (((1,), (1,)), ((), ())), preferred_element_type=F32) * scale
        m = jnp.max(s, axis=-1, keepdims=True)
        e = jnp.exp(s - m)
        p = e / jnp.sum(e, axis=-1, keepdims=True)
        o_ref[:, cols] = jnp.dot(p.astype(BF16), v, preferred_element_type=F32)


def _cross_attention(q, mk, mv, *, tq):
    nb, s, width = q.shape
    mem = mk.shape[1]
    q_spec = pl.BlockSpec((None, tq, width), lambda b, i: (b, i, 0))
    m_spec = pl.BlockSpec((None, mem, width), lambda b, i: (b, 0, 0))
    return pl.pallas_call(
        _cross_attn_kernel,
        grid=(nb, s // tq),
        in_specs=[q_spec, m_spec, m_spec],
        out_specs=q_spec,
        out_shape=jax.ShapeDtypeStruct(q.shape, F32),
        compiler_params=_params("parallel", "arbitrary"),
        name="cross_attn",
    )(q, mk, mv)


def _top_rows(s, ids, k):
    n = s.shape[0]
    row = lax.broadcasted_iota(I32, s.shape, 0).astype(F32)
    vals, picked = [], []
    for _ in range(k):
        m = jnp.max(s, axis=0, keepdims=True)
        pos = jnp.min(jnp.where(s == m, row, float(n)), axis=0, keepdims=True)
        hit = row == pos
        vals.append(m)
        picked.append(jnp.max(jnp.where(hit, ids, -1.0), axis=0, keepdims=True))
        s = jnp.where(hit, -jnp.inf, s)
    return jnp.concatenate(vals, axis=0), jnp.concatenate(picked, axis=0)


def _peer_topk_kernel(q_ref, k1_ref, k2_ref, eidx_ref, gate_ref):
    half = PEER_DKEY // 2
    q = q_ref[...]
    tb = q.shape[0]
    key_ids = lax.broadcasted_iota(I32, (PEER_NKEYS, tb), 0).astype(F32)
    nt = (((1,), (1,)), ((), ()))
    s1 = lax.dot_general(k1_ref[...], q[:, :half].astype(BF16), nt, preferred_element_type=F32)
    s2 = lax.dot_general(k2_ref[...], q[:, half:].astype(BF16), nt, preferred_element_type=F32)
    v1, i1 = _top_rows(s1, key_ids, PEER_TOPK)
    v2, i2 = _top_rows(s2, key_ids, PEER_TOPK)
    counts = [PEER_TOPK // (a + 1) for a in range(PEER_TOPK)]
    pad = -sum(counts) % SUBLANES

    def per_a(x, fill):
        rows = [jnp.broadcast_to(x[a:a + 1], (counts[a], tb)) for a in range(PEER_TOPK)]
        return jnp.concatenate(rows + [jnp.full((pad, tb), fill, F32)], axis=0)

    def per_b(x):
        return jnp.concatenate([x[0:counts[a]] for a in range(PEER_TOPK)] + [jnp.zeros((pad, tb), F32)], axis=0)

    cand = per_a(v1, -jnp.inf) + per_b(v2)
    cidx = per_a(i1, 0.0) * PEER_NKEYS + per_b(i2)
    best, eidx = _top_rows(cand, cidx, PEER_TOPK)
    e = jnp.exp(best - best[0:1])
    gate_ref[...] = e / jnp.sum(e, axis=0, keepdims=True)
    eidx_ref[...] = eidx.astype(I32)


def _peer_topk(q, sub_k1, sub_k2, *, tb):
    m = q.shape[0]
    rows = PEER_HEADS * PEER_TOPK
    key_spec = pl.BlockSpec((PEER_NKEYS, PEER_DKEY // 2), lambda i, h: (0, 0))
    out_spec = pl.BlockSpec((PEER_TOPK, tb), lambda i, h: (h, i))
    return pl.pallas_call(
        _peer_topk_kernel,
        grid=(m // tb, PEER_HEADS),
        in_specs=[pl.BlockSpec((tb, PEER_DKEY), lambda i, h: (i, h)), key_spec, key_spec],
        out_specs=[out_spec, out_spec],
        out_shape=[jax.ShapeDtypeStruct((rows, m), I32), jax.ShapeDtypeStruct((rows, m), F32)],
        compiler_params=_params("parallel", "arbitrary"),
        name="peer_topk",
    )(q, sub_k1, sub_k2)


def _tree_sum(terms):
    while len(terms) > 1:
        terms = [terms[j] + terms[j + 1] for j in range(0, len(terms) - 1, 2)] + \
            ([terms[-1]] if len(terms) % 2 else [])
    return terms[0]


def _pack_expert_tables(expert_u, expert_v):
    ub = lax.bitcast_convert_type(expert_u.astype(BF16), jnp.uint16).astype(jnp.uint32)
    vb = lax.bitcast_convert_type(expert_v.astype(BF16), jnp.uint16).astype(jnp.uint32)
    n_exp, d = expert_u.shape
    return ((ub << 16) | vb).reshape(n_exp, d // LANES, LANES)


def _peer_expert_kernel(idx_ref, idxn_ref, gate_ref, y_ref, lnx_ref, lnf_ref, tab_hbm, o_ref, buf0, buf1, rows_ref,
                        cols_ref, sem, *, tg, n_sel):
    i = pl.program_id(0)
    last = pl.num_programs(0) - 1
    bufs = (buf0, buf1)
    n_tiles = buf0.shape[1]
    high = jnp.uint32(0xFFFF0000)

    def slab_copy(ids, row, s, t, k):
        return pltpu.make_async_copy(tab_hbm.at[ids[row, k]], bufs[s].at[t, :, k, :], sem.at[s])

    def wait_slot(s):
        pltpu.make_async_copy(bufs[s], bufs[s], sem.at[s]).wait()

    @pl.when(i == 0)
    def _():
        def prime(t, carry):
            for k in range(n_sel):
                slab_copy(idx_ref, t, 0, t, k).start()
            return carry
        lax.fori_loop(0, tg, prime, 0)

    def group(s, ids_next, next_row0):
        rows = slice(s * tg, (s + 1) * tg)
        wait_slot(s)
        y = y_ref[rows, :]
        x = y * lax.rsqrt(jnp.mean(y * y, axis=-1, keepdims=True) + EPS) * lnx_ref[...]
        for t in range(tg):
            rows_ref[0, t] = x[t:t + 1, :]
            rows_ref[1, t] = y[t:t + 1, :]
        token = lax.broadcasted_iota(I32, (n_sel, tg), 1)
        half = n_sel // 2

        def dot_pass(t, carry):
            for k in range(half):
                slab_copy(ids_next, next_row0 + t, 1 - s, t, k).start(priority=k % 2)
            part = _tree_sum([lax.bitcast_convert_type(bufs[s][t, c] & high, F32)
                              * rows_ref[0, t, :, c * LANES:(c + 1) * LANES] for c in range(n_tiles)])
            cols_ref[t] = jnp.broadcast_to(jnp.sum(part, axis=-1, keepdims=True), (n_sel, LANES))
            return carry

        lax.fori_loop(0, tg, dot_pass, 0)
        pre = jnp.zeros((n_sel, tg), F32)
        for t in range(tg):
            pre = jnp.where(token == t, cols_ref[t][:, 0:tg], pre)
        act = 0.5 * pre * (1.0 + lax.erf(pre * (2.0 ** -0.5)))
        w = gate_ref[s] * act
        for t in range(tg):
            cols_ref[t] = jnp.broadcast_to(w[:, t:t + 1], (n_sel, LANES))

        def mix_pass(t, carry):
            for k in range(half, n_sel):
                slab_copy(ids_next, next_row0 + t, 1 - s, t, k).start(priority=k % 2)
            wt = cols_ref[t]
            sums = []
            for c in range(n_tiles):
                prod = lax.bitcast_convert_type(bufs[s][t, c] << 16, F32) * wt
                groups = [prod[g * SUBLANES:(g + 1) * SUBLANES, :] for g in range(n_sel // SUBLANES)]
                sums.append(jnp.sum(_tree_sum(groups), axis=0, keepdims=True))
            out = rows_ref[1, t] + jnp.concatenate(sums, axis=1)
            ms = jnp.mean(out * out, axis=-1, keepdims=True)
            rows_ref[2, t] = out * lax.rsqrt(ms + EPS) * lnf_ref[...]
            return carry

        lax.fori_loop(0, tg, mix_pass, 0)
        for t in range(tg):
            o_ref[s * tg + t:s * tg + t + 1, :] = rows_ref[2, t]

    group(0, idx_ref, tg)
    group(1, idxn_ref, 0)

    @pl.when(i == last)
    def _():
        wait_slot(0)


def _peer_experts(eidx, gate_cols, y, ln_ffn, ln_final, table, *, tg):
    m, d = y.shape
    n_sel = eidx.shape[1]
    n_tiles = table.shape[1]
    assert m % (2 * tg) == 0 and n_tiles * LANES == d
    n_steps = m // (2 * tg)
    row_spec = pl.BlockSpec((2 * tg, d), lambda i: (i, 0))
    return pl.pallas_call(
        functools.partial(_peer_expert_kernel, tg=tg, n_sel=n_sel),
        grid=(n_steps,),
        in_specs=[
            pl.BlockSpec((2 * tg, n_sel), lambda i: (i, 0), memory_space=pltpu.SMEM),
            pl.BlockSpec((2 * tg, n_sel), lambda i: (jnp.minimum(i + 1, n_steps - 1), 0), memory_space=pltpu.SMEM),
            pl.BlockSpec((2, n_sel, tg), lambda i: (i, 0, 0)),
            row_spec,
            pl.BlockSpec((1, d), lambda i: (0, 0)),
            pl.BlockSpec((1, d), lambda i: (0, 0)),
            pl.BlockSpec(memory_space=pl.ANY),
        ],
        out_specs=row_spec,
        out_shape=jax.ShapeDtypeStruct((m, d), F32),
        scratch_shapes=[pltpu.VMEM((tg, n_tiles, n_sel, LANES), jnp.uint32),
                        pltpu.VMEM((tg, n_tiles, n_sel, LANES), jnp.uint32), pltpu.VMEM((3, tg, 1, d), F32),
                        pltpu.VMEM((tg, n_sel, LANES), F32), pltpu.SemaphoreType.DMA((2,))],
        compiler_params=_params("arbitrary"),
        name="peer_experts",
    )(eidx, eidx, gate_cols, y, ln_ffn.reshape(1, d), ln_final.reshape(1, d), table)


def _channel_mixers(y, mk, mv, nb, p, *, tm, tq, tb, tg):
    m, d = y.shape
    q = _norm_matmul([y], [p["ln_cross"]], p["w_cq"], tm=tm, tn=MEM_WIDTH, name="cross_q_proj")
    o = _cross_attention(q.reshape(nb, m // nb, MEM_WIDTH), mk, mv, tq=tq).reshape(m, MEM_WIDTH)
    y = _norm_matmul([o], None, p["w_co"], residual=y, tm=tm, tn=512, name="cross_out_proj")
    pq = _norm_matmul([y], [p["ln_ffn"]], p["w_pq"], tm=tm, tn=512, name="peer_query_proj")
    eidx_t, gate_t = _peer_topk(pq, p["sub_k1"], p["sub_k2"], tb=tb)
    n_sel = eidx_t.shape[0]
    gate_cols = gate_t.reshape(n_sel, m // tg, tg).transpose(1, 0, 2)
    return _peer_experts(eidx_t.T, gate_cols, y, p["ln_ffn"], p["ln_final"], p["expert_table"], tg=tg)


def kernel(x_prompt, x_sample, cache_a_k, cache_a_v, cache_b_k, cache_b_v, cache_mem_k, cache_mem_v, mem_prompt, ln_mix, w_in, ln_a_out, ln_b_out, w_out, b_sinks, rel_bias, ln_cross, ln_mem, w_cq, w_ckv, w_co, ln_ffn, w_pq, sub_keys_1, sub_keys_2, expert_u, expert_v, ln_final):
    depth = w_in.shape[0]
    assert depth == 1, "the caches are laid out for a single layer"
    batch, seq, d = x_prompt.shape
    dec_batch, dec_seq, _ = x_sample.shape
    a_win = cache_a_k.shape[2]
    b_win = cache_b_k.shape[2]
    mem_len = mem_prompt.shape[1]
    assert seq == a_win == A_WIN and b_win == B_WIN and seq % BLK == 0
    l = 0
    split = 3 * A_WIDTH + B_WIDTH
    w_in_main = w_in[l, :, :split].astype(BF16)
    w_in_kvb = w_in[l, :, split:].astype(BF16)
    p = dict(ln_cross=ln_cross[l], w_cq=w_cq[l].astype(BF16), w_co=w_co[l].astype(BF16), ln_ffn=ln_ffn[l],
             w_pq=w_pq[l].astype(BF16), sub_k1=sub_keys_1[l].astype(BF16), sub_k2=sub_keys_2[l].astype(BF16),
             ln_final=ln_final, expert_table=_pack_expert_tables(expert_u[l], expert_v[l]))
    w_out_bf = w_out[l].astype(BF16)
    w_ckv_bf = w_ckv[l].astype(BF16)
    sinks = b_sinks[l].astype(F32)

    def project(x2d, tm):
        main = _norm_matmul([x2d], [ln_mix[l]], w_in_main, tm=tm, tn=512, out_split=4, name="in_proj")
        kvb = _norm_matmul([x2d], [ln_mix[l]], w_in_kvb, tm=tm, tn=B_KV_WIDTH, out_split=2, name="in_proj_kvb")
        return main, kvb

    def merge(oa, ob, resid, tm):
        return _norm_matmul([oa, ob], [ln_a_out[l], ln_b_out[l]], w_out_bf, residual=resid, tm=tm, tn=512,
                            name="mixer_out_proj")

    xp = x_prompt.reshape(batch * seq, d)
    main, kvb = project(xp, 512)
    n_delta_a = seq // BLK + 1
    bias_a = _prompt_bias_tiles(rel_bias, 0, _mixer_a_multiplicity, n_delta_a)
    window_b = lambda dist: ((dist >= 0) & (dist <= B_WIN)).astype(np.int32)
    bias_b = _prompt_bias_tiles(rel_bias, A_HEADS, window_b, B_WIN // BLK + 1)
    oa = _prompt_attention((main, 0), (main, 1), (main, 2), bias_a, None, batch=batch, seq=seq, gqa_pairs=0,
                           chunk_tiles=8)
    ob = _prompt_attention((main, 3), (kvb, 0), (kvb, 1), bias_b, sinks, batch=batch, seq=seq,
                           gqa_pairs=B_HEADS // B_KV_HEADS // 2, chunk_tiles=B_WIN // BLK + 1)
    yp = merge(oa, ob, xp, 512)
    mem_kv = _norm_matmul([mem_prompt.reshape(batch * mem_len, d)], [ln_mem[l]], w_ckv_bf, tm=512, tn=MEM_WIDTH,
                          out_split=2, name="mem_kv_proj")
    mk = mem_kv[0].reshape(batch, mem_len, MEM_WIDTH)
    mv = mem_kv[1].reshape(batch, mem_len, MEM_WIDTH)
    y_prompt = _channel_mixers(yp, mk, mv, batch, p, tm=512, tq=512, tb=128, tg=8)

    xs = x_sample.reshape(dec_batch * dec_seq, d)
    main_s, kvb_s = project(xs, 512)
    chunk = 512
    key_pos = np.arange(a_win)
    t_pos = np.arange(dec_seq)
    dist_cache = a_win + t_pos[:, None] - key_pos[None, :]
    bias_sa = _distance_bias(rel_bias, dist_cache, 0, A_HEADS, _mixer_a_multiplicity(dist_cache))
    bias_sa = bias_sa.reshape(A_HEADS * dec_seq, a_win // chunk, chunk).transpose(1, 0, 2)
    dist_new = t_pos[:, None] - np.arange(LANES)[None, :]
    mult_new = np.where(np.arange(LANES)[None, :] < dec_seq, _mixer_a_multiplicity(dist_new), 0)
    bias_sa_new = _distance_bias(rel_bias, dist_new, 0, A_HEADS, mult_new).reshape(A_HEADS * dec_seq, LANES)
    oa_s, aks, avs = _sample_attention_a(
        main_s[0], main_s[1], main_s[2],
        cache_a_k[l].reshape(dec_batch, a_win, A_WIDTH), cache_a_v[l].reshape(dec_batch, a_win, A_WIDTH),
        bias_sa, bias_sa_new, chunk=chunk)
    seq_pos = np.arange(2 * b_win)
    dist_b = b_win + t_pos[:, None] - seq_pos[None, :]
    mult_b = ((dist_b >= 0) & (dist_b <= B_WIN) & (seq_pos[None, :] < b_win + dec_seq)).astype(np.int32)
    bias_sb = _distance_bias(rel_bias, dist_b, A_HEADS, B_HEADS, mult_b).reshape(B_HEADS * dec_seq, 2 * b_win)
    sink_rows = jnp.repeat(sinks, dec_seq).reshape(B_HEADS * dec_seq, 1)
    ob_s, bks, bvs = _sample_attention_b(
        main_s[3], kvb_s[0], kvb_s[1],
        cache_b_k[l].reshape(dec_batch, b_win, B_KV_WIDTH), cache_b_v[l].reshape(dec_batch, b_win, B_KV_WIDTH),
        bias_sb, sink_rows)
    ys = merge(oa_s, ob_s, xs, 512)
    mk_s = cache_mem_k[l].reshape(dec_batch, mem_len, MEM_WIDTH)
    mv_s = cache_mem_v[l].reshape(dec_batch, mem_len, MEM_WIDTH)
    y_sample = _channel_mixers(ys, mk_s, mv_s, dec_batch, p, tm=512, tq=dec_seq, tb=128, tg=8)

    def heads(x, *shape):
        return x.reshape(1, *shape)

    return (y_prompt.reshape(batch, seq, d), y_sample.reshape(dec_batch, dec_seq, d),
            heads(main[1], batch, seq, A_HEADS, HEAD_DIM), heads(main[2], batch, seq, A_HEADS, HEAD_DIM),
            heads(kvb[0].reshape(batch, seq, B_KV_WIDTH)[:, seq - b_win:], batch, b_win, B_KV_HEADS, HEAD_DIM),
            heads(kvb[1].reshape(batch, seq, B_KV_WIDTH)[:, seq - b_win:], batch, b_win, B_KV_HEADS, HEAD_DIM),
            heads(mk, batch, mem_len, MEM_HEADS, MEM_HEAD_DIM), heads(mv, batch, mem_len, MEM_HEADS, MEM_HEAD_DIM),
            heads(aks, dec_batch, a_win, A_HEADS, HEAD_DIM), heads(avs, dec_batch, a_win, A_HEADS, HEAD_DIM),
            heads(bks, dec_batch, b_win, B_KV_HEADS, HEAD_DIM), heads(bvs, dec_batch, b_win, B_KV_HEADS, HEAD_DIM))
```

```python
import functools
import math

import numpy as np
import jax
import jax.numpy as jnp
from jax import lax
from jax.experimental import pallas as pl
from jax.experimental.pallas import tpu as pltpu

F32 = jnp.float32
BF16 = jnp.bfloat16
I32 = jnp.int32

EPS = 1e-6
MASKED = -1e30

LANES = 128
SUBLANES = 8
VMEM_LIMIT = 48 * 1024 * 1024

HEAD_DIM = 64
A_HEADS = 16
A_PATTERNS = ((128, 1), (512, 4), (2048, 16))
A_WIN = 2048
B_HEADS = 16
B_KV_HEADS = 2
B_WIN = 128
A_WIDTH = A_HEADS * HEAD_DIM
B_WIDTH = B_HEADS * HEAD_DIM
B_KV_WIDTH = B_KV_HEADS * HEAD_DIM
ATTN_SCALE = HEAD_DIM ** -0.5
N_BUCKETS = 32
MAX_EXACT = N_BUCKETS // 2
MAX_DISTANCE = A_WIN
MEM_HEADS = 4
MEM_HEAD_DIM = 128
MEM_WIDTH = MEM_HEADS * MEM_HEAD_DIM
PEER_HEADS = 8
PEER_NKEYS = 128
PEER_DKEY = 256
PEER_TOPK = 16
BLK = 128


def _params(*semantics, flags=None):
    return pltpu.CompilerParams(dimension_semantics=semantics, vmem_limit_bytes=VMEM_LIMIT, flags=flags)


def _norm_matmul_kernel(*refs, n_groups, norm, residual):
    xs = refs[:n_groups]
    pos = n_groups
    gs = refs[pos:pos + n_groups] if norm else ()
    pos += n_groups if norm else 0
    w_ref = refs[pos]
    pos += 1
    r_ref = refs[pos] if residual else None
    pos += 1 if residual else 0
    o_ref, xn_ref = refs[pos], refs[pos + 1]

    @pl.when(pl.program_id(1) == 0)
    def _():
        off = 0
        for gi in range(n_groups):
            x = xs[gi][...]
            if norm:
                ms = jnp.mean(x * x, axis=-1, keepdims=True)
                x = x * lax.rsqrt(ms + EPS) * gs[gi][...]
            width = x.shape[-1]
            xn_ref[:, off:off + width] = x.astype(BF16)
            off += width

    acc = jnp.dot(xn_ref[...], w_ref[...], preferred_element_type=F32)
    if residual:
        acc = acc + r_ref[...]
    o_ref[...] = acc


def _norm_matmul(xs, gains, w, residual=None, *, tm, tn, out_split=1, name="norm_matmul"):
    m = xs[0].shape[0]
    k_total, n = w.shape
    assert sum(x.shape[1] for x in xs) == k_total and m % tm == 0 and n % (tn * out_split) == 0
    norm = gains is not None
    nj_per = n // out_split // tn
    in_specs = [pl.BlockSpec((tm, x.shape[1]), lambda i, j: (i, 0)) for x in xs]
    args = list(xs)
    if norm:
        in_specs += [pl.BlockSpec((1, g.shape[-1]), lambda i, j: (0, 0)) for g in gains]
        args += [g.reshape(1, -1) for g in gains]
    in_specs.append(pl.BlockSpec((k_total, tn), lambda i, j: (0, j)))
    args.append(w)
    if residual is not None:
        in_specs.append(pl.BlockSpec((tm, tn), lambda i, j: (i, j)))
        args.append(residual)
    if out_split == 1:
        out_shape = jax.ShapeDtypeStruct((m, n), F32)
        out_spec = pl.BlockSpec((tm, tn), lambda i, j: (i, j))
    else:
        out_shape = jax.ShapeDtypeStruct((out_split, m, n // out_split), F32)
        out_spec = pl.BlockSpec((None, tm, tn), lambda i, j: (j // nj_per, i, j % nj_per))
    return pl.pallas_call(
        functools.partial(_norm_matmul_kernel, n_groups=len(xs), norm=norm, residual=residual is not None),
        grid=(m // tm, n // tn),
        in_specs=in_specs,
        out_specs=out_spec,
        out_shape=out_shape,
        scratch_shapes=[pltpu.VMEM((tm, k_total), BF16)],
        compiler_params=_params("parallel", "arbitrary"),
        name=name,
    )(*args)


def _rel_bucket(dist):
    dist = np.maximum(np.asarray(dist), 0)
    ratio = np.log(np.maximum(dist, 1) / MAX_EXACT) / math.log(MAX_DISTANCE / MAX_EXACT)
    large = np.minimum(MAX_EXACT + (ratio * (N_BUCKETS - MAX_EXACT)).astype(np.int32), N_BUCKETS - 1)
    return np.where(dist < MAX_EXACT, dist, large).astype(np.int32)


def _mixer_a_multiplicity(dist):
    dist = np.asarray(dist)
    mult = np.zeros(dist.shape, np.int32)
    for window, dilation in A_PATTERNS:
        mult += ((dist >= 0) & (dist <= window) & (dist % dilation == 0)).astype(np.int32)
    return mult


def _distance_bias(rel_bias, dist, head_lo, n_heads, mult):
    table = rel_bias[:, head_lo:head_lo + n_heads].astype(F32).T
    vals = table[:, _rel_bucket(dist)]
    logm = np.log(np.maximum(mult, 1)).astype(np.float32)
    return jnp.where(jnp.asarray(mult > 0)[None], vals + jnp.asarray(logm)[None], MASKED)


def _pair_select(x, which):
    lane_head = lax.broadcasted_iota(I32, x.shape, 1) // HEAD_DIM
    swapped = pltpu.roll(x, HEAD_DIM, axis=1)
    return jnp.where(lane_head == which, x, swapped)


def _prompt_attn_kernel(*refs, n_delta, chunk_tiles, n_chunks, gqa_pairs, has_sink):
    if has_sink:
        q_ref, k_ref, v_ref, bias_ref, sink_ref, o_ref, m_ref, l_ref, acc_ref, kb_ref, vb_ref = refs
    else:
        q_ref, k_ref, v_ref, bias_ref, o_ref, m_ref, l_ref, acc_ref, kb_ref, vb_ref = refs
        sink_ref = None
    hp = pl.program_id(1)
    qb = pl.program_id(2)
    keys_on_lanes = n_chunks > 1

    @pl.when(qb == 0)
    def _():
        k = k_ref[...]
        v = v_ref[...]
        if gqa_pairs:
            kv_head = hp // gqa_pairs
            k = _pair_select(k, kv_head)
            v = _pair_select(v, kv_head)
        kb_ref[...] = (k.T if keys_on_lanes else k).astype(BF16)
        vb_ref[...] = v.astype(BF16)

    lane = lax.broadcasted_iota(I32, (BLK, LANES), 1)
    left = lane < HEAD_DIM
    q = q_ref[...] * ATTN_SCALE
    q2 = jnp.concatenate([jnp.where(left, q, 0.0), jnp.where(left, 0.0, q)], axis=0).astype(BF16)
    if has_sink:
        m_ref[...] = jnp.concatenate([jnp.full((BLK, 1), sink_ref[2 * hp], F32),
                                      jnp.full((BLK, 1), sink_ref[2 * hp + 1], F32)], axis=0)
        l_ref[...] = jnp.ones(l_ref.shape, F32)
    else:
        m_ref[...] = jnp.full(m_ref.shape, MASKED, F32)
        l_ref[...] = jnp.zeros(l_ref.shape, F32)
    acc_ref[...] = jnp.zeros(acc_ref.shape, F32)

    def chunk(start_tile):
        start = pl.multiple_of(start_tile * BLK, BLK)
        v = vb_ref[pl.ds(start, chunk_tiles * BLK), :]
        if keys_on_lanes:
            k = kb_ref[:, start_tile * BLK:(start_tile + chunk_tiles) * BLK]
        else:
            k = kb_ref[pl.ds(start, chunk_tiles * BLK), :]
        if keys_on_lanes:
            s = jnp.dot(q2, k, preferred_element_type=F32)
        else:
            s = lax.dot_general(q2, k, (((1,), (1,)), ((), ())), preferred_element_type=F32)
        rows = []
        for h2 in range(2):
            tiles = []
            for j in range(chunk_tiles):
                delta = qb - (start_tile + j)
                tiles.append(bias_ref[h2, jnp.where((delta >= 0) & (delta < n_delta), delta, n_delta)])
            rows.append(jnp.concatenate(tiles, axis=1))
        s = s + jnp.concatenate(rows, axis=0)
        m_old = m_ref[...]
        m_new = jnp.maximum(m_old, jnp.max(s, axis=-1, keepdims=True))
        alpha = jnp.exp(m_old - m_new)
        p = jnp.exp(s - m_new)
        l_ref[...] = alpha * l_ref[...] + jnp.sum(p, axis=-1, keepdims=True)
        m_ref[...] = m_new
        acc_ref[...] = alpha * acc_ref[...] + jnp.dot(p.astype(BF16), v, preferred_element_type=F32)

    if n_chunks == 1:
        chunk(jnp.maximum(qb - (chunk_tiles - 1), 0))
    else:
        for c in range(n_chunks):
            pl.when(c * chunk_tiles <= qb)(functools.partial(chunk, c * chunk_tiles))
    o = acc_ref[...] / l_ref[...]
    o_ref[...] = jnp.where(left, o[0:BLK], o[BLK:2 * BLK])


def _prompt_attention(q_src, k_src, v_src, bias, sinks, *, batch, seq, gqa_pairs, chunk_tiles):
    n_pairs = A_WIDTH // LANES
    n_delta = bias.shape[1] - 1
    nq = seq // BLK
    n_chunks = 1 if n_delta <= chunk_tiles else nq // chunk_tiles
    assert nq % chunk_tiles == 0
    (q_arr, q_idx), (k_arr, k_idx), (v_arr, v_idx) = q_src, k_src, v_src
    kv_map = (lambda which: (lambda b, hp, qb: (which, b, 0))) if gqa_pairs else \
        (lambda which: (lambda b, hp, qb: (which, b, hp)))
    in_specs = [
        pl.BlockSpec((None, BLK, LANES), lambda b, hp, qb: (q_idx, b * nq + qb, hp)),
        pl.BlockSpec((None, seq, LANES), kv_map(k_idx)),
        pl.BlockSpec((None, seq, LANES), kv_map(v_idx)),
        pl.BlockSpec((2, n_delta + 1, BLK, BLK), lambda b, hp, qb: (hp, 0, 0, 0)),
    ]
    args = [q_arr, k_arr, v_arr, bias]
    if sinks is not None:
        in_specs.append(pl.BlockSpec(memory_space=pltpu.SMEM))
        args.append(sinks)
    return pl.pallas_call(
        functools.partial(_prompt_attn_kernel, n_delta=n_delta, chunk_tiles=chunk_tiles, n_chunks=n_chunks,
                          gqa_pairs=gqa_pairs, has_sink=sinks is not None),
        grid=(batch, n_pairs, nq),
        in_specs=in_specs,
        out_specs=pl.BlockSpec((BLK, LANES), lambda b, hp, qb: (b * nq + qb, hp)),
        out_shape=jax.ShapeDtypeStruct((batch * seq, A_WIDTH), F32),
        scratch_shapes=[pltpu.VMEM((2 * BLK, 1), F32), pltpu.VMEM((2 * BLK, 1), F32), pltpu.VMEM((2 * BLK, LANES), F32),
                        pltpu.VMEM((LANES, seq) if n_chunks > 1 else (seq, LANES), BF16),
                        pltpu.VMEM((seq, LANES), BF16)],
        compiler_params=_params("parallel", "parallel", "arbitrary"),
        name="prompt_attn_b" if gqa_pairs else "prompt_attn_a",
    )(*args)


def _prompt_bias_tiles(rel_bias, head_lo, mult_fn, n_delta):
    length = (n_delta + 1) * BLK + BLK - 1
    dist = np.arange(length) - (BLK - 1)
    mult = mult_fn(dist)
    assert not mult[dist > (n_delta - 1) * BLK].any(), "the reach must end before the masked tile"
    by_dist = _distance_bias(rel_bias, dist, head_lo, 16, mult)
    seg_len = 2 * BLK - 1
    segs = jnp.stack([by_dist[:, t * BLK:t * BLK + seg_len] for t in range(n_delta + 1)], axis=1)
    flat = jnp.broadcast_to(segs[:, :, None, :], (16, n_delta + 1, BLK, seg_len)).reshape(16, n_delta + 1, -1)
    flat = jnp.pad(flat, ((0, 0), (0, 0), (0, BLK)))
    hankel = flat.reshape(16, n_delta + 1, BLK, seg_len + 1)[..., :BLK]
    return hankel[..., ::-1]


def _block_diag_queries(q, n_heads):
    t, width = q.shape
    rows = n_heads * t
    tiled = jnp.broadcast_to(q[None], (n_heads, t, width)).reshape(rows, width)
    row_head = lax.broadcasted_iota(I32, (rows, width), 0) // t
    lane_head = lax.broadcasted_iota(I32, (rows, width), 1) // (width // n_heads)
    return jnp.where(row_head == lane_head, tiled * ATTN_SCALE, 0.0).astype(BF16)


def _block_diag_extract(o, n_heads):
    rows, width = o.shape
    t = rows // n_heads
    row_head = lax.broadcasted_iota(I32, (rows, width), 0) // t
    lane_head = lax.broadcasted_iota(I32, (rows, width), 1) // (width // n_heads)
    return jnp.sum(jnp.where(row_head == lane_head, o, 0.0).reshape(n_heads, t, width), axis=0)


def _softmax_step(s, v_t, m_ref, l_ref, acc_ref):
    m_old = m_ref[...]
    m_new = jnp.maximum(m_old, jnp.max(s, axis=-1, keepdims=True))
    alpha = jnp.exp(m_old - m_new)
    p = jnp.exp(s - m_new)
    l_ref[...] = alpha * l_ref[...] + jnp.sum(p, axis=-1, keepdims=True)
    pv = lax.dot_general(p.astype(BF16), v_t, (((1,), (1,)), ((), ())), preferred_element_type=F32)
    acc_ref[...] = alpha * acc_ref[...] + pv
    m_ref[...] = m_new


def _sample_a_kernel(q_ref, knt_ref, vnt_ref, kc_ref, vc_ref, knext_ref, vnext_ref, bias_ref, biasn_ref,
                     o_ref, ko_ref, vo_ref, qbd_ref, m_ref, l_ref, acc_ref, *, t_new):
    c = pl.program_id(1)
    last = pl.num_programs(1) - 1
    chunk = kc_ref.shape[1]

    @pl.when(c == 0)
    def _():
        qbd_ref[...] = _block_diag_queries(q_ref[...], A_HEADS)
        m_ref[...] = jnp.full(m_ref.shape, MASKED, F32)
        l_ref[...] = jnp.zeros(l_ref.shape, F32)
        acc_ref[...] = jnp.zeros(acc_ref.shape, F32)

    kc = kc_ref[...]
    vc = vc_ref[...]
    s = jnp.dot(qbd_ref[...], kc.astype(BF16), preferred_element_type=F32)
    _softmax_step(s + bias_ref[...], vc.astype(BF16), m_ref, l_ref, acc_ref)

    at_end = c == last
    k_after = jnp.where(at_end, knt_ref[...], knext_ref[...])
    v_after = jnp.where(at_end, vnt_ref[...], vnext_ref[...])
    ko_ref[...] = jnp.concatenate([kc, k_after], axis=1)[:, t_new:t_new + chunk]
    vo_ref[...] = jnp.concatenate([vc, v_after], axis=1)[:, t_new:t_new + chunk]

    @pl.when(at_end)
    def _():
        s_new = jnp.dot(qbd_ref[...], knt_ref[...].astype(BF16), preferred_element_type=F32)
        _softmax_step(s_new + biasn_ref[...], vnt_ref[...].astype(BF16), m_ref, l_ref, acc_ref)
        o_ref[...] = _block_diag_extract(acc_ref[...] / l_ref[...], A_HEADS)


def _sample_attention_a(q, k_new_t, v_new_t, cache_k_t, cache_v_t, bias, bias_new, *, chunk, t_new):
    nb, width, win = cache_k_t.shape
    nc = win // chunk
    rows = A_HEADS * t_new
    tiles_per_chunk = chunk // LANES
    q_spec = pl.BlockSpec((t_new, width), lambda b, c: (b, 0))
    new_spec = pl.BlockSpec((None, width, LANES), lambda b, c: (b, 0, 0))
    cache_spec = pl.BlockSpec((None, width, chunk), lambda b, c: (b, 0, c))
    next_spec = pl.BlockSpec((None, width, LANES),
                             lambda b, c: (b, 0, jnp.minimum(c + 1, nc - 1) * tiles_per_chunk))
    return pl.pallas_call(
        functools.partial(_sample_a_kernel, t_new=t_new),
        grid=(nb, nc),
        in_specs=[q_spec, new_spec, new_spec, cache_spec, cache_spec, next_spec, next_spec,
                  pl.BlockSpec((None, rows, chunk), lambda b, c: (c, 0, 0)),
                  pl.BlockSpec((rows, LANES), lambda b, c: (0, 0))],
        out_specs=[q_spec, cache_spec, cache_spec],
        out_shape=[jax.ShapeDtypeStruct((nb * t_new, width), F32),
                   jax.ShapeDtypeStruct(cache_k_t.shape, F32), jax.ShapeDtypeStruct(cache_v_t.shape, F32)],
        scratch_shapes=[pltpu.VMEM((rows, width), BF16), pltpu.VMEM((rows, 1), F32), pltpu.VMEM((rows, 1), F32),
                        pltpu.VMEM((rows, width), F32)],
        compiler_params=_params("parallel", "arbitrary"),
        name="sample_attn_a",
    )(q, k_new_t, v_new_t, cache_k_t, cache_v_t, cache_k_t, cache_v_t, bias, bias_new)


def _expand_kv(x):
    first = _pair_select(x, False)
    second = _pair_select(x, True)
    reps = B_HEADS // B_KV_HEADS // 2
    return jnp.concatenate([first] * reps + [second] * reps, axis=1)


def _sample_b_kernel(q_ref, kn_ref, vn_ref, kc_ref, vc_ref, bias_ref, sink_ref, o_ref, ko_ref, vo_ref, seq_ref):
    t_new = kn_ref.shape[0]
    win = kc_ref.shape[0]
    seq_ref[...] = jnp.zeros(seq_ref.shape, F32)
    for idx, (c_ref, n_ref, out_ref) in enumerate(((kc_ref, kn_ref, ko_ref), (vc_ref, vn_ref, vo_ref))):
        seq_ref[idx, 0:win, :] = c_ref[...]
        seq_ref[idx, win:win + t_new, :] = n_ref[...]
        out_ref[...] = seq_ref[idx, t_new:win + t_new, :]
    qbd = _block_diag_queries(q_ref[...], B_HEADS)
    k = _expand_kv(seq_ref[0]).astype(BF16)
    v = _expand_kv(seq_ref[1]).astype(BF16)
    s = lax.dot_general(qbd, k, (((1,), (1,)), ((), ())), preferred_element_type=F32) + bias_ref[...]
    sink = sink_ref[...]
    m = jnp.maximum(jnp.max(s, axis=-1, keepdims=True), sink)
    p = jnp.exp(s - m)
    den = jnp.sum(p, axis=-1, keepdims=True) + jnp.exp(sink - m)
    o = jnp.dot(p.astype(BF16), v, preferred_element_type=F32) / den
    o_ref[...] = _block_diag_extract(o, B_HEADS)


def _sample_attention_b(q, k_new, v_new, cache_k, cache_v, bias, sink_rows):
    nb, win, kvw = cache_k.shape
    t = q.shape[0] // nb
    rows = B_HEADS * t
    q_spec = pl.BlockSpec((t, B_WIDTH), lambda b: (b, 0))
    new_spec = pl.BlockSpec((t, kvw), lambda b: (b, 0))
    cache_spec = pl.BlockSpec((None, win, kvw), lambda b: (b, 0, 0))
    return pl.pallas_call(
        _sample_b_kernel,
        grid=(nb,),
        in_specs=[q_spec, new_spec, new_spec, cache_spec, cache_spec,
                  pl.BlockSpec((rows, 2 * win), lambda b: (0, 0)), pl.BlockSpec((rows, 1), lambda b: (0, 0))],
        out_specs=[q_spec, cache_spec, cache_spec],
        out_shape=[jax.ShapeDtypeStruct(q.shape, F32),
                   jax.ShapeDtypeStruct(cache_k.shape, F32), jax.ShapeDtypeStruct(cache_v.shape, F32)],
        scratch_shapes=[pltpu.VMEM((2, 2 * win, kvw), F32)],
        compiler_params=_params("parallel"),
        name="sample_attn_b",
    )(q, k_new, v_new, cache_k, cache_v, bias, sink_rows)


def _cross_attn_kernel(q_ref, k_ref, v_ref, o_ref):
    scale = MEM_HEAD_DIM ** -0.5
    for h in range(MEM_HEADS):
        cols = slice(h * MEM_HEAD_DIM, (h + 1) * MEM_HEAD_DIM)
        q = q_ref[:, cols].astype(BF16)
        k = k_ref[:, cols].astype(BF16)
        v = v_ref[:, cols].astype(BF16)
        s = lax.dot_general(q, k, (((1,), (1,)), ((), ())), preferred_element_type=F32) * scale
        m = jnp.max(s, axis=-1, keepdims=True)
        e = jnp.exp(s - m)
        p = e / jnp.sum(e, axis=-1, keepdims=True)
        o_ref[:, cols] = jnp.dot(p.astype(BF16), v, preferred_element_type=F32)


def _cross_attention(q, mk, mv, *, tq):
    nb, s, width = q.shape
    mem = mk.shape[1]
    q_spec = pl.BlockSpec((None, tq, width), lambda b, i: (b, i, 0))
    m_spec = pl.BlockSpec((None, mem, width), lambda b, i: (b, 0, 0))
    return pl.pallas_call(
        _cross_attn_kernel,
        grid=(nb, s // tq),
        in_specs=[q_spec, m_spec, m_spec],
        out_specs=q_spec,
        out_shape=jax.ShapeDtypeStruct(q.shape, F32),
        compiler_params=_params("parallel", "arbitrary"),
        name="cross_attn",
    )(q, mk, mv)


def _top_rows(s, ids, k):
    n = s.shape[0]
    row = lax.broadcasted_iota(I32, s.shape, 0).astype(F32)
    vals, picked = [], []
    for _ in range(k):
        m = jnp.max(s, axis=0, keepdims=True)
        pos = jnp.min(jnp.where(s == m, row, float(n)), axis=0, keepdims=True)
        hit = row == pos
        vals.append(m)
        picked.append(jnp.max(jnp.where(hit, ids, -1.0), axis=0, keepdims=True))
        s = jnp.where(hit, -jnp.inf, s)
    return jnp.concatenate(vals, axis=0), jnp.concatenate(picked, axis=0)


def _peer_topk_kernel(q_ref, k1_ref, k2_ref, eidx_ref, gate_ref):
    half = PEER_DKEY // 2
    q = q_ref[...]
    tb = q.shape[0]
    key_ids = lax.broadcasted_iota(I32, (PEER_NKEYS, tb), 0).astype(F32)
    nt = (((1,), (1,)), ((), ()))
    s1 = lax.dot_general(k1_ref[...], q[:, :half].astype(BF16), nt, preferred_element_type=F32)
    s2 = lax.dot_general(k2_ref[...], q[:, half:].astype(BF16), nt, preferred_element_type=F32)
    v1, i1 = _top_rows(s1, key_ids, PEER_TOPK)
    v2, i2 = _top_rows(s2, key_ids, PEER_TOPK)
    counts = [PEER_TOPK // (a + 1) for a in range(PEER_TOPK)]
    pad = -sum(counts) % SUBLANES

    def per_a(x, fill):
        rows = [jnp.broadcast_to(x[a:a + 1], (counts[a], tb)) for a in range(PEER_TOPK)]
        return jnp.concatenate(rows + [jnp.full((pad, tb), fill, F32)], axis=0)

    def per_b(x):
        return jnp.concatenate([x[0:counts[a]] for a in range(PEER_TOPK)] + [jnp.zeros((pad, tb), F32)], axis=0)

    cand = per_a(v1, -jnp.inf) + per_b(v2)
    cidx = per_a(i1, 0.0) * PEER_NKEYS + per_b(i2)
    best, eidx = _top_rows(cand, cidx, PEER_TOPK)
    e = jnp.exp(best - best[0:1])
    gate_ref[...] = e / jnp.sum(e, axis=0, keepdims=True)
    eidx_ref[...] = eidx.astype(I32)


def _peer_topk(q, sub_k1, sub_k2, *, tb):
    m = q.shape[0]
    rows = PEER_HEADS * PEER_TOPK
    key_spec = pl.BlockSpec((PEER_NKEYS, PEER_DKEY // 2), lambda i, h: (0, 0))
    out_spec = pl.BlockSpec((PEER_TOPK, tb), lambda i, h: (h, i))
    return pl.pallas_call(
        _peer_topk_kernel,
        grid=(m // tb, PEER_HEADS),
        in_specs=[pl.BlockSpec((tb, PEER_DKEY), lambda i, h: (i, h)), key_spec, key_spec],
        out_specs=[out_spec, out_spec],
        out_shape=[jax.ShapeDtypeStruct((rows, m), I32), jax.ShapeDtypeStruct((rows, m), F32)],
        compiler_params=_params("parallel", "arbitrary"),
        name="peer_topk",
    )(q, sub_k1, sub_k2)


def _tree_sum(terms):
    while len(terms) > 1:
        terms = [terms[j] + terms[j + 1] for j in range(0, len(terms) - 1, 2)] + \
            ([terms[-1]] if len(terms) % 2 else [])
    return terms[0]


def _pack_expert_tables(expert_u, expert_v):
    ub = lax.bitcast_convert_type(expert_u.astype(BF16), jnp.uint16).astype(jnp.uint32)
    vb = lax.bitcast_convert_type(expert_v.astype(BF16), jnp.uint16).astype(jnp.uint32)
    n_exp, d = expert_u.shape
    return ((ub << 16) | vb).reshape(n_exp, d // LANES, LANES)


def _peer_expert_kernel(idx_ref, idxn_ref, gate_ref, y_ref, lnx_ref, lnf_ref, tab_hbm, o_ref, buf0, buf1, rows_ref,
                        cols_ref, sem, *, tg, n_sel):
    i = pl.program_id(0)
    last = pl.num_programs(0) - 1
    bufs = (buf0, buf1)
    n_tiles = buf0.shape[1]
    high = jnp.uint32(0xFFFF0000)

    def slab_copy(ids, row, s, t, k):
        return pltpu.make_async_copy(tab_hbm.at[ids[row, k]], bufs[s].at[t, :, k, :], sem.at[s])

    def wait_slot(s):
        pltpu.make_async_copy(bufs[s], bufs[s], sem.at[s]).wait()

    @pl.when(i == 0)
    def _():
        def prime(t, carry):
            for k in range(n_sel):
                slab_copy(idx_ref, t, 0, t, k).start()
            return carry
        lax.fori_loop(0, tg, prime, 0)

    def group(s, ids_next, next_row0):
        rows = slice(s * tg, (s + 1) * tg)
        wait_slot(s)
        y = y_ref[rows, :]
        x = y * lax.rsqrt(jnp.mean(y * y, axis=-1, keepdims=True) + EPS) * lnx_ref[...]
        for t in range(tg):
            rows_ref[0, t] = x[t:t + 1, :]
            rows_ref[1, t] = y[t:t + 1, :]
        token = lax.broadcasted_iota(I32, (n_sel, tg), 1)
        half = n_sel // 2

        def dot_pass(t, carry):
            for k in range(half):
                slab_copy(ids_next, next_row0 + t, 1 - s, t, k).start(priority=k % 2)
            part = _tree_sum([lax.bitcast_convert_type(bufs[s][t, c] & high, F32)
                              * rows_ref[0, t, :, c * LANES:(c + 1) * LANES] for c in range(n_tiles)])
            cols_ref[t] = jnp.broadcast_to(jnp.sum(part, axis=-1, keepdims=True), (n_sel, LANES))
            return carry

        lax.fori_loop(0, tg, dot_pass, 0)
        pre = jnp.zeros((n_sel, tg), F32)
        for t in range(tg):
            pre = jnp.where(token == t, cols_ref[t][:, 0:tg], pre)
        act = 0.5 * pre * (1.0 + lax.erf(pre * (2.0 ** -0.5)))
        w = gate_ref[s] * act
        for t in range(tg):
            cols_ref[t] = jnp.broadcast_to(w[:, t:t + 1], (n_sel, LANES))

        def mix_pass(t, carry):
            for k in range(half, n_sel):
                slab_copy(ids_next, next_row0 + t, 1 - s, t, k).start(priority=k % 2)
            wt = cols_ref[t]
            sums = []
            for c in range(n_tiles):
                prod = lax.bitcast_convert_type(bufs[s][t, c] << 16, F32) * wt
                groups = [prod[g * SUBLANES:(g + 1) * SUBLANES, :] for g in range(n_sel // SUBLANES)]
                sums.append(jnp.sum(_tree_sum(groups), axis=0, keepdims=True))
            out = rows_ref[1, t] + jnp.concatenate(sums, axis=1)
            ms = jnp.mean(out * out, axis=-1, keepdims=True)
            rows_ref[2, t] = out * lax.rsqrt(ms + EPS) * lnf_ref[...]
            return carry

        lax.fori_loop(0, tg, mix_pass, 0)
        for t in range(tg):
            o_ref[s * tg + t:s * tg + t + 1, :] = rows_ref[2, t]

    group(0, idx_ref, tg)
    group(1, idxn_ref, 0)

    @pl.when(i == last)
    def _():
        wait_slot(0)


def _peer_experts(eidx, gate_cols, y, ln_ffn, ln_final, table, *, tg):
    m, d = y.shape
    n_sel = eidx.shape[1]
    n_tiles = table.shape[1]
    assert m % (2 * tg) == 0 and n_tiles * LANES == d
    n_steps = m // (2 * tg)
    row_spec = pl.BlockSpec((2 * tg, d), lambda i: (i, 0))
    return pl.pallas_call(
        functools.partial(_peer_expert_kernel, tg=tg, n_sel=n_sel),
        grid=(n_steps,),
        in_specs=[
            pl.BlockSpec((2 * tg, n_sel), lambda i: (i, 0), memory_space=pltpu.SMEM),
            pl.BlockSpec((2 * tg, n_sel), lambda i: (jnp.minimum(i + 1, n_steps - 1), 0), memory_space=pltpu.SMEM),
            pl.BlockSpec((2, n_sel, tg), lambda i: (i, 0, 0)),
            row_spec,
            pl.BlockSpec((1, d), lambda i: (0, 0)),
            pl.BlockSpec((1, d), lambda i: (0, 0)),
            pl.BlockSpec(memory_space=pl.ANY),
        ],
        out_specs=row_spec,
        out_shape=jax.ShapeDtypeStruct((m, d), F32),
        scratch_shapes=[pltpu.VMEM((tg, n_tiles, n_sel, LANES), jnp.uint32),
                        pltpu.VMEM((tg, n_tiles, n_sel, LANES), jnp.uint32), pltpu.VMEM((3, tg, 1, d), F32),
                        pltpu.VMEM((tg, n_sel, LANES), F32), pltpu.SemaphoreType.DMA((2,))],
        compiler_params=_params("arbitrary"),
        name="peer_experts",
    )(eidx, eidx, gate_cols, y, ln_ffn.reshape(1, d), ln_final.reshape(1, d), table)


def _channel_mixers(y, mk, mv, nb, p, *, tm, tq, tb, tg):
    m, d = y.shape
    q = _norm_matmul([y], [p["ln_cross"]], p["w_cq"], tm=tm, tn=MEM_WIDTH, name="cross_q_proj")
    o = _cross_attention(q.reshape(nb, m // nb, MEM_WIDTH), mk, mv, tq=tq).reshape(m, MEM_WIDTH)
    y = _norm_matmul([o], None, p["w_co"], residual=y, tm=tm, tn=512, name="cross_out_proj")
    pq = _norm_matmul([y], [p["ln_ffn"]], p["w_pq"], tm=tm, tn=512, name="peer_query_proj")
    eidx_t, gate_t = _peer_topk(pq, p["sub_k1"], p["sub_k2"], tb=tb)
    n_sel = eidx_t.shape[0]
    gate_cols = gate_t.reshape(n_sel, m // tg, tg).transpose(1, 0, 2)
    return _peer_experts(eidx_t.T, gate_cols, y, p["ln_ffn"], p["ln_final"], p["expert_table"], tg=tg)


def kernel(x_prompt, x_sample, cache_a_k, cache_a_v, cache_b_k, cache_b_v, cache_mem_k, cache_mem_v, mem_prompt, ln_mix, w_in, ln_a_out, ln_b_out, w_out, b_sinks, rel_bias, ln_cross, ln_mem, w_cq, w_ckv, w_co, ln_ffn, w_pq, sub_keys_1, sub_keys_2, expert_u, expert_v, ln_final):
    depth = w_in.shape[0]
    assert depth == 1, "the caches are laid out for a single layer"
    batch, seq, d = x_prompt.shape
    dec_batch, dec_seq, _ = x_sample.shape
    a_win = cache_a_k.shape[2]
    b_win = cache_b_k.shape[2]
    mem_len = mem_prompt.shape[1]
    assert seq == a_win == A_WIN and b_win == B_WIN and seq % BLK == 0
    l = 0
    split = 3 * A_WIDTH + B_WIDTH
    w_in_main = w_in[l, :, :split].astype(BF16)
    w_in_kvb = w_in[l, :, split:].astype(BF16)
    p = dict(ln_cross=ln_cross[l], w_cq=w_cq[l].astype(BF16), w_co=w_co[l].astype(BF16), ln_ffn=ln_ffn[l],
             w_pq=w_pq[l].astype(BF16), sub_k1=sub_keys_1[l].astype(BF16), sub_k2=sub_keys_2[l].astype(BF16),
             ln_final=ln_final, expert_table=_pack_expert_tables(expert_u[l], expert_v[l]))
    w_out_bf = w_out[l].astype(BF16)
    w_ckv_bf = w_ckv[l].astype(BF16)
    sinks = b_sinks[l].astype(F32)

    def project(x2d, tm):
        main = _norm_matmul([x2d], [ln_mix[l]], w_in_main, tm=tm, tn=512, out_split=4, name="in_proj")
        kvb = _norm_matmul([x2d], [ln_mix[l]], w_in_kvb, tm=tm, tn=B_KV_WIDTH, out_split=2, name="in_proj_kvb")
        return main, kvb

    def merge(oa, ob, resid, tm):
        return _norm_matmul([oa, ob], [ln_a_out[l], ln_b_out[l]], w_out_bf, residual=resid, tm=tm, tn=512,
                            name="mixer_out_proj")

    xp = x_prompt.reshape(batch * seq, d)
    main, kvb = project(xp, 512)
    n_delta_a = seq // BLK + 1
    bias_a = _prompt_bias_tiles(rel_bias, 0, _mixer_a_multiplicity, n_delta_a)
    window_b = lambda dist: ((dist >= 0) & (dist <= B_WIN)).astype(np.int32)
    bias_b = _prompt_bias_tiles(rel_bias, A_HEADS, window_b, B_WIN // BLK + 1)
    oa = _prompt_attention((main, 0), (main, 1), (main, 2), bias_a, None, batch=batch, seq=seq, gqa_pairs=0,
                           chunk_tiles=8)
    ob = _prompt_attention((main, 3), (kvb, 0), (kvb, 1), bias_b, sinks, batch=batch, seq=seq,
                           gqa_pairs=B_HEADS // B_KV_HEADS // 2, chunk_tiles=B_WIN // BLK + 1)
    yp = merge(oa, ob, xp, 512)
    mem_kv = _norm_matmul([mem_prompt.reshape(batch * mem_len, d)], [ln_mem[l]], w_ckv_bf, tm=512, tn=MEM_WIDTH,
                          out_split=2, name="mem_kv_proj")
    mk = mem_kv[0].reshape(batch, mem_len, MEM_WIDTH)
    mv = mem_kv[1].reshape(batch, mem_len, MEM_WIDTH)
    y_prompt = _channel_mixers(yp, mk, mv, batch, p, tm=512, tq=512, tb=128, tg=8)

    xs = x_sample.reshape(dec_batch * dec_seq, d)
    main_s, kvb_s = project(xs, 512)
    chunk = 512
    key_pos = np.arange(a_win)
    t_pos = np.arange(dec_seq)
    dist_cache = a_win + t_pos[:, None] - key_pos[None, :]
    bias_sa = _distance_bias(rel_bias, dist_cache, 0, A_HEADS, _mixer_a_multiplicity(dist_cache))
    bias_sa = bias_sa.reshape(A_HEADS * dec_seq, a_win // chunk, chunk).transpose(1, 0, 2)
    dist_new = t_pos[:, None] - np.arange(LANES)[None, :]
    mult_new = np.where(np.arange(LANES)[None, :] < dec_seq, _mixer_a_multiplicity(dist_new), 0)
    bias_sa_new = _distance_bias(rel_bias, dist_new, 0, A_HEADS, mult_new).reshape(A_HEADS * dec_seq, LANES)
    def positions_minor(x, n_pos):
        return x.reshape(dec_batch, n_pos, A_WIDTH).transpose(0, 2, 1)

    def new_tokens_tile(x):
        return jnp.pad(positions_minor(x, dec_seq), ((0, 0), (0, 0), (0, LANES - dec_seq)))

    oa_s, aks_t, avs_t = _sample_attention_a(
        main_s[0], new_tokens_tile(main_s[1]), new_tokens_tile(main_s[2]),
        positions_minor(cache_a_k[l], a_win), positions_minor(cache_a_v[l], a_win),
        bias_sa, bias_sa_new, chunk=chunk, t_new=dec_seq)
    aks = aks_t.transpose(0, 2, 1)
    avs = avs_t.transpose(0, 2, 1)
    seq_pos = np.arange(2 * b_win)
    dist_b = b_win + t_pos[:, None] - seq_pos[None, :]
    mult_b = ((dist_b >= 0) & (dist_b <= B_WIN) & (seq_pos[None, :] < b_win + dec_seq)).astype(np.int32)
    bias_sb = _distance_bias(rel_bias, dist_b, A_HEADS, B_HEADS, mult_b).reshape(B_HEADS * dec_seq, 2 * b_win)
    sink_rows = jnp.repeat(sinks, dec_seq).reshape(B_HEADS * dec_seq, 1)
    ob_s, bks, bvs = _sample_attention_b(
        main_s[3], kvb_s[0], kvb_s[1],
        cache_b_k[l].reshape(dec_batch, b_win, B_KV_WIDTH), cache_b_v[l].reshape(dec_batch, b_win, B_KV_WIDTH),
        bias_sb, sink_rows)
    ys = merge(oa_s, ob_s, xs, 512)
    mk_s = cache_mem_k[l].reshape(dec_batch, mem_len, MEM_WIDTH)
    mv_s = cache_mem_v[l].reshape(dec_batch, mem_len, MEM_WIDTH)
    y_sample = _channel_mixers(ys, mk_s, mv_s, dec_batch, p, tm=512, tq=dec_seq, tb=128, tg=8)

    def heads(x, *shape):
        return x.reshape(1, *shape)

    return (y_prompt.reshape(batch, seq, d), y_sample.reshape(dec_batch, dec_seq, d),
            heads(main[1], batch, seq, A_HEADS, HEAD_DIM), heads(main[2], batch, seq, A_HEADS, HEAD_DIM),
            heads(kvb[0].reshape(batch, seq, B_KV_WIDTH)[:, seq - b_win:], batch, b_win, B_KV_HEADS, HEAD_DIM),
            heads(kvb[1].reshape(batch, seq, B_KV_WIDTH)[:, seq - b_win:], batch, b_win, B_KV_HEADS, HEAD_DIM),
            heads(mk, batch, mem_len, MEM_HEADS, MEM_HEAD_DIM), heads(mv, batch, mem_len, MEM_HEADS, MEM_HEAD_DIM),
            heads(aks, dec_batch, a_win, A_HEADS, HEAD_DIM), heads(avs, dec_batch, a_win, A_HEADS, HEAD_DIM),
            heads(bks, dec_batch, b_win, B_KV_HEADS, HEAD_DIM), heads(bvs, dec_batch, b_win, B_KV_HEADS, HEAD_DIM))
```

```python
import functools
import math

import numpy as np
import jax
import jax.numpy as jnp
from jax import lax
from jax.experimental import pallas as pl
from jax.experimental.pallas import tpu as pltpu

F32 = jnp.float32
BF16 = jnp.bfloat16
I32 = jnp.int32

EPS = 1e-6
MASKED = -1e30

LANES = 128
SUBLANES = 8
VMEM_LIMIT = 48 * 1024 * 1024

HEAD_DIM = 64
A_HEADS = 16
A_PATTERNS = ((128, 1), (512, 4), (2048, 16))
A_WIN = 2048
B_HEADS = 16
B_KV_HEADS = 2
B_WIN = 128
A_WIDTH = A_HEADS * HEAD_DIM
B_WIDTH = B_HEADS * HEAD_DIM
B_KV_WIDTH = B_KV_HEADS * HEAD_DIM
ATTN_SCALE = HEAD_DIM ** -0.5
N_BUCKETS = 32
MAX_EXACT = N_BUCKETS // 2
MAX_DISTANCE = A_WIN
MEM_HEADS = 4
MEM_HEAD_DIM = 128
MEM_WIDTH = MEM_HEADS * MEM_HEAD_DIM
PEER_HEADS = 8
PEER_NKEYS = 128
PEER_DKEY = 256
PEER_TOPK = 16
BLK = 128
PEER_SLOTS = 4
PEER_GROUPS_AHEAD = 2


def _params(*semantics, flags=None):
    return pltpu.CompilerParams(dimension_semantics=semantics, vmem_limit_bytes=VMEM_LIMIT, flags=flags)


def _norm_matmul_kernel(*refs, n_groups, norm, residual):
    xs = refs[:n_groups]
    pos = n_groups
    gs = refs[pos:pos + n_groups] if norm else ()
    pos += n_groups if norm else 0
    w_ref = refs[pos]
    pos += 1
    r_ref = refs[pos] if residual else None
    pos += 1 if residual else 0
    o_ref, xn_ref = refs[pos], refs[pos + 1]

    @pl.when(pl.program_id(1) == 0)
    def _():
        off = 0
        for gi in range(n_groups):
            x = xs[gi][...]
            if norm:
                ms = jnp.mean(x * x, axis=-1, keepdims=True)
                x = x * lax.rsqrt(ms + EPS) * gs[gi][...]
            width = x.shape[-1]
            xn_ref[:, off:off + width] = x.astype(BF16)
            off += width

    acc = jnp.dot(xn_ref[...], w_ref[...], preferred_element_type=F32)
    if residual:
        acc = acc + r_ref[...]
    o_ref[...] = acc


def _norm_matmul(xs, gains, w, residual=None, *, tm, tn, out_split=1, name="norm_matmul"):
    m = xs[0].shape[0]
    k_total, n = w.shape
    assert sum(x.shape[1] for x in xs) == k_total and m % tm == 0 and n % (tn * out_split) == 0
    norm = gains is not None
    nj_per = n // out_split // tn
    in_specs = [pl.BlockSpec((tm, x.shape[1]), lambda i, j: (i, 0)) for x in xs]
    args = list(xs)
    if norm:
        in_specs += [pl.BlockSpec((1, g.shape[-1]), lambda i, j: (0, 0)) for g in gains]
        args += [g.reshape(1, -1) for g in gains]
    in_specs.append(pl.BlockSpec((k_total, tn), lambda i, j: (0, j)))
    args.append(w)
    if residual is not None:
        in_specs.append(pl.BlockSpec((tm, tn), lambda i, j: (i, j)))
        args.append(residual)
    if out_split == 1:
        out_shape = jax.ShapeDtypeStruct((m, n), F32)
        out_spec = pl.BlockSpec((tm, tn), lambda i, j: (i, j))
    else:
        out_shape = jax.ShapeDtypeStruct((out_split, m, n // out_split), F32)
        out_spec = pl.BlockSpec((None, tm, tn), lambda i, j: (j // nj_per, i, j % nj_per))
    return pl.pallas_call(
        functools.partial(_norm_matmul_kernel, n_groups=len(xs), norm=norm, residual=residual is not None),
        grid=(m // tm, n // tn),
        in_specs=in_specs,
        out_specs=out_spec,
        out_shape=out_shape,
        scratch_shapes=[pltpu.VMEM((tm, k_total), BF16)],
        compiler_params=_params("parallel", "arbitrary"),
        name=name,
    )(*args)


def _rel_bucket(dist):
    dist = np.maximum(np.asarray(dist), 0)
    ratio = np.log(np.maximum(dist, 1) / MAX_EXACT) / math.log(MAX_DISTANCE / MAX_EXACT)
    large = np.minimum(MAX_EXACT + (ratio * (N_BUCKETS - MAX_EXACT)).astype(np.int32), N_BUCKETS - 1)
    return np.where(dist < MAX_EXACT, dist, large).astype(np.int32)


def _mixer_a_multiplicity(dist):
    dist = np.asarray(dist)
    mult = np.zeros(dist.shape, np.int32)
    for window, dilation in A_PATTERNS:
        mult += ((dist >= 0) & (dist <= window) & (dist % dilation == 0)).astype(np.int32)
    return mult


def _distance_bias(rel_bias, dist, head_lo, n_heads, mult):
    table = rel_bias[:, head_lo:head_lo + n_heads].astype(F32).T
    vals = table[:, _rel_bucket(dist)]
    logm = np.log(np.maximum(mult, 1)).astype(np.float32)
    return jnp.where(jnp.asarray(mult > 0)[None], vals + jnp.asarray(logm)[None], MASKED)


def _pair_select(x, which):
    lane_head = lax.broadcasted_iota(I32, x.shape, 1) // HEAD_DIM
    swapped = pltpu.roll(x, HEAD_DIM, axis=1)
    return jnp.where(lane_head == which, x, swapped)


def _prompt_attn_kernel(*refs, n_delta, chunk_tiles, n_chunks, gqa_pairs, has_sink):
    if has_sink:
        q_ref, k_ref, v_ref, bias_ref, sink_ref, o_ref, m_ref, l_ref, acc_ref, kb_ref, vb_ref = refs
    else:
        q_ref, k_ref, v_ref, bias_ref, o_ref, m_ref, l_ref, acc_ref, kb_ref, vb_ref = refs
        sink_ref = None
    hp = pl.program_id(1)
    qb = pl.program_id(2)
    keys_on_lanes = n_chunks > 1

    @pl.when(qb == 0)
    def _():
        k = k_ref[...]
        v = v_ref[...]
        if gqa_pairs:
            kv_head = hp // gqa_pairs
            k = _pair_select(k, kv_head)
            v = _pair_select(v, kv_head)
        kb_ref[...] = (k.T if keys_on_lanes else k).astype(BF16)
        vb_ref[...] = v.astype(BF16)

    lane = lax.broadcasted_iota(I32, (BLK, LANES), 1)
    left = lane < HEAD_DIM
    q = q_ref[...] * ATTN_SCALE
    q2 = jnp.concatenate([jnp.where(left, q, 0.0), jnp.where(left, 0.0, q)], axis=0).astype(BF16)
    if has_sink:
        m_ref[...] = jnp.concatenate([jnp.full((BLK, 1), sink_ref[2 * hp], F32),
                                      jnp.full((BLK, 1), sink_ref[2 * hp + 1], F32)], axis=0)
        l_ref[...] = jnp.ones(l_ref.shape, F32)
    else:
        m_ref[...] = jnp.full(m_ref.shape, MASKED, F32)
        l_ref[...] = jnp.zeros(l_ref.shape, F32)
    acc_ref[...] = jnp.zeros(acc_ref.shape, F32)

    def chunk(start_tile):
        start = pl.multiple_of(start_tile * BLK, BLK)
        v = vb_ref[pl.ds(start, chunk_tiles * BLK), :]
        if keys_on_lanes:
            k = kb_ref[:, start_tile * BLK:(start_tile + chunk_tiles) * BLK]
        else:
            k = kb_ref[pl.ds(start, chunk_tiles * BLK), :]
        if keys_on_lanes:
            s = jnp.dot(q2, k, preferred_element_type=F32)
        else:
            s = lax.dot_general(q2, k, (((1,), (1,)), ((), ())), preferred_element_type=F32)
        rows = []
        for h2 in range(2):
            tiles = []
            for j in range(chunk_tiles):
                delta = qb - (start_tile + j)
                tiles.append(bias_ref[h2, jnp.where((delta >= 0) & (delta < n_delta), delta, n_delta)])
            rows.append(jnp.concatenate(tiles, axis=1))
        s = s + jnp.concatenate(rows, axis=0)
        m_old = m_ref[...]
        m_new = jnp.maximum(m_old, jnp.max(s, axis=-1, keepdims=True))
        alpha = jnp.exp(m_old - m_new)
        p = jnp.exp(s - m_new)
        l_ref[...] = alpha * l_ref[...] + jnp.sum(p, axis=-1, keepdims=True)
        m_ref[...] = m_new
        acc_ref[...] = alpha * acc_ref[...] + jnp.dot(p.astype(BF16), v, preferred_element_type=F32)

    if n_chunks == 1:
        chunk(jnp.maximum(qb - (chunk_tiles - 1), 0))
    else:
        for c in range(n_chunks):
            pl.when(c * chunk_tiles <= qb)(functools.partial(chunk, c * chunk_tiles))
    o = acc_ref[...] / l_ref[...]
    o_ref[...] = jnp.where(left, o[0:BLK], o[BLK:2 * BLK])


def _prompt_attention(q_src, k_src, v_src, bias, sinks, *, batch, seq, gqa_pairs, chunk_tiles):
    n_pairs = A_WIDTH // LANES
    n_delta = bias.shape[1] - 1
    nq = seq // BLK
    n_chunks = 1 if n_delta <= chunk_tiles else nq // chunk_tiles
    assert nq % chunk_tiles == 0
    (q_arr, q_idx), (k_arr, k_idx), (v_arr, v_idx) = q_src, k_src, v_src
    kv_map = (lambda which: (lambda b, hp, qb: (which, b, 0))) if gqa_pairs else \
        (lambda which: (lambda b, hp, qb: (which, b, hp)))
    in_specs = [
        pl.BlockSpec((None, BLK, LANES), lambda b, hp, qb: (q_idx, b * nq + qb, hp)),
        pl.BlockSpec((None, seq, LANES), kv_map(k_idx)),
        pl.BlockSpec((None, seq, LANES), kv_map(v_idx)),
        pl.BlockSpec((2, n_delta + 1, BLK, BLK), lambda b, hp, qb: (hp, 0, 0, 0)),
    ]
    args = [q_arr, k_arr, v_arr, bias]
    if sinks is not None:
        in_specs.append(pl.BlockSpec(memory_space=pltpu.SMEM))
        args.append(sinks)
    return pl.pallas_call(
        functools.partial(_prompt_attn_kernel, n_delta=n_delta, chunk_tiles=chunk_tiles, n_chunks=n_chunks,
                          gqa_pairs=gqa_pairs, has_sink=sinks is not None),
        grid=(batch, n_pairs, nq),
        in_specs=in_specs,
        out_specs=pl.BlockSpec((BLK, LANES), lambda b, hp, qb: (b * nq + qb, hp)),
        out_shape=jax.ShapeDtypeStruct((batch * seq, A_WIDTH), F32),
        scratch_shapes=[pltpu.VMEM((2 * BLK, 1), F32), pltpu.VMEM((2 * BLK, 1), F32), pltpu.VMEM((2 * BLK, LANES), F32),
                        pltpu.VMEM((LANES, seq) if n_chunks > 1 else (seq, LANES), BF16),
                        pltpu.VMEM((seq, LANES), BF16)],
        compiler_params=_params("parallel", "parallel", "arbitrary"),
        name="prompt_attn_b" if gqa_pairs else "prompt_attn_a",
    )(*args)


def _prompt_bias_tiles(rel_bias, head_lo, mult_fn, n_delta):
    length = (n_delta + 1) * BLK + BLK - 1
    dist = np.arange(length) - (BLK - 1)
    mult = mult_fn(dist)
    assert not mult[dist > (n_delta - 1) * BLK].any(), "the reach must end before the masked tile"
    by_dist = _distance_bias(rel_bias, dist, head_lo, 16, mult)
    seg_len = 2 * BLK - 1
    segs = jnp.stack([by_dist[:, t * BLK:t * BLK + seg_len] for t in range(n_delta + 1)], axis=1)
    flat = jnp.broadcast_to(segs[:, :, None, :], (16, n_delta + 1, BLK, seg_len)).reshape(16, n_delta + 1, -1)
    flat = jnp.pad(flat, ((0, 0), (0, 0), (0, BLK)))
    hankel = flat.reshape(16, n_delta + 1, BLK, seg_len + 1)[..., :BLK]
    return hankel[..., ::-1]


def _block_diag_queries(q, n_heads):
    t, width = q.shape
    rows = n_heads * t
    tiled = jnp.broadcast_to(q[None], (n_heads, t, width)).reshape(rows, width)
    row_head = lax.broadcasted_iota(I32, (rows, width), 0) // t
    lane_head = lax.broadcasted_iota(I32, (rows, width), 1) // (width // n_heads)
    return jnp.where(row_head == lane_head, tiled * ATTN_SCALE, 0.0).astype(BF16)


def _block_diag_extract(o, n_heads):
    rows, width = o.shape
    t = rows // n_heads
    row_head = lax.broadcasted_iota(I32, (rows, width), 0) // t
    lane_head = lax.broadcasted_iota(I32, (rows, width), 1) // (width // n_heads)
    return jnp.sum(jnp.where(row_head == lane_head, o, 0.0).reshape(n_heads, t, width), axis=0)


def _softmax_step(s, v_t, m_ref, l_ref, acc_ref):
    m_old = m_ref[...]
    m_new = jnp.maximum(m_old, jnp.max(s, axis=-1, keepdims=True))
    alpha = jnp.exp(m_old - m_new)
    p = jnp.exp(s - m_new)
    l_ref[...] = alpha * l_ref[...] + jnp.sum(p, axis=-1, keepdims=True)
    pv = lax.dot_general(p.astype(BF16), v_t, (((1,), (1,)), ((), ())), preferred_element_type=F32)
    acc_ref[...] = alpha * acc_ref[...] + pv
    m_ref[...] = m_new


def _sample_a_kernel(q_ref, knt_ref, vnt_ref, kc_ref, vc_ref, knext_ref, vnext_ref, bias_ref, biasn_ref,
                     o_ref, ko_ref, vo_ref, qbd_ref, m_ref, l_ref, acc_ref, *, t_new):
    c = pl.program_id(1)
    last = pl.num_programs(1) - 1
    chunk = kc_ref.shape[1]

    @pl.when(c == 0)
    def _():
        qbd_ref[...] = _block_diag_queries(q_ref[...], A_HEADS)
        m_ref[...] = jnp.full(m_ref.shape, MASKED, F32)
        l_ref[...] = jnp.zeros(l_ref.shape, F32)
        acc_ref[...] = jnp.zeros(acc_ref.shape, F32)

    kc = kc_ref[...]
    vc = vc_ref[...]
    s = jnp.dot(qbd_ref[...], kc.astype(BF16), preferred_element_type=F32)
    _softmax_step(s + bias_ref[...], vc.astype(BF16), m_ref, l_ref, acc_ref)

    at_end = c == last
    k_after = jnp.where(at_end, knt_ref[...], knext_ref[...])
    v_after = jnp.where(at_end, vnt_ref[...], vnext_ref[...])
    ko_ref[...] = jnp.concatenate([kc, k_after], axis=1)[:, t_new:t_new + chunk]
    vo_ref[...] = jnp.concatenate([vc, v_after], axis=1)[:, t_new:t_new + chunk]

    @pl.when(at_end)
    def _():
        s_new = jnp.dot(qbd_ref[...], knt_ref[...].astype(BF16), preferred_element_type=F32)
        _softmax_step(s_new + biasn_ref[...], vnt_ref[...].astype(BF16), m_ref, l_ref, acc_ref)
        o_ref[...] = _block_diag_extract(acc_ref[...] / l_ref[...], A_HEADS)


def _sample_attention_a(q, k_new_t, v_new_t, cache_k_t, cache_v_t, bias, bias_new, *, chunk, t_new):
    nb, width, win = cache_k_t.shape
    nc = win // chunk
    rows = A_HEADS * t_new
    tiles_per_chunk = chunk // LANES
    q_spec = pl.BlockSpec((t_new, width), lambda b, c: (b, 0))
    new_spec = pl.BlockSpec((None, width, LANES), lambda b, c: (b, 0, 0))
    cache_spec = pl.BlockSpec((None, width, chunk), lambda b, c: (b, 0, c))
    next_spec = pl.BlockSpec((None, width, LANES),
                             lambda b, c: (b, 0, jnp.minimum(c + 1, nc - 1) * tiles_per_chunk))
    return pl.pallas_call(
        functools.partial(_sample_a_kernel, t_new=t_new),
        grid=(nb, nc),
        in_specs=[q_spec, new_spec, new_spec, cache_spec, cache_spec, next_spec, next_spec,
                  pl.BlockSpec((None, rows, chunk), lambda b, c: (c, 0, 0)),
                  pl.BlockSpec((rows, LANES), lambda b, c: (0, 0))],
        out_specs=[q_spec, cache_spec, cache_spec],
        out_shape=[jax.ShapeDtypeStruct((nb * t_new, width), F32),
                   jax.ShapeDtypeStruct(cache_k_t.shape, F32), jax.ShapeDtypeStruct(cache_v_t.shape, F32)],
        scratch_shapes=[pltpu.VMEM((rows, width), BF16), pltpu.VMEM((rows, 1), F32), pltpu.VMEM((rows, 1), F32),
                        pltpu.VMEM((rows, width), F32)],
        compiler_params=_params("parallel", "arbitrary"),
        name="sample_attn_a",
    )(q, k_new_t, v_new_t, cache_k_t, cache_v_t, cache_k_t, cache_v_t, bias, bias_new)


def _expand_kv(x):
    first = _pair_select(x, False)
    second = _pair_select(x, True)
    reps = B_HEADS // B_KV_HEADS // 2
    return jnp.concatenate([first] * reps + [second] * reps, axis=1)


def _sample_b_kernel(q_ref, kn_ref, vn_ref, kc_ref, vc_ref, bias_ref, sink_ref, o_ref, ko_ref, vo_ref, seq_ref):
    t_new = kn_ref.shape[0]
    win = kc_ref.shape[0]
    seq_ref[...] = jnp.zeros(seq_ref.shape, F32)
    for idx, (c_ref, n_ref, out_ref) in enumerate(((kc_ref, kn_ref, ko_ref), (vc_ref, vn_ref, vo_ref))):
        seq_ref[idx, 0:win, :] = c_ref[...]
        seq_ref[idx, win:win + t_new, :] = n_ref[...]
        out_ref[...] = seq_ref[idx, t_new:win + t_new, :]
    qbd = _block_diag_queries(q_ref[...], B_HEADS)
    k = _expand_kv(seq_ref[0]).astype(BF16)
    v = _expand_kv(seq_ref[1]).astype(BF16)
    s = lax.dot_general(qbd, k, (((1,), (1,)), ((), ())), preferred_element_type=F32) + bias_ref[...]
    sink = sink_ref[...]
    m = jnp.maximum(jnp.max(s, axis=-1, keepdims=True), sink)
    p = jnp.exp(s - m)
    den = jnp.sum(p, axis=-1, keepdims=True) + jnp.exp(sink - m)
    o = jnp.dot(p.astype(BF16), v, preferred_element_type=F32) / den
    o_ref[...] = _block_diag_extract(o, B_HEADS)


def _sample_attention_b(q, k_new, v_new, cache_k, cache_v, bias, sink_rows):
    nb, win, kvw = cache_k.shape
    t = q.shape[0] // nb
    rows = B_HEADS * t
    q_spec = pl.BlockSpec((t, B_WIDTH), lambda b: (b, 0))
    new_spec = pl.BlockSpec((t, kvw), lambda b: (b, 0))
    cache_spec = pl.BlockSpec((None, win, kvw), lambda b: (b, 0, 0))
    return pl.pallas_call(
        _sample_b_kernel,
        grid=(nb,),
        in_specs=[q_spec, new_spec, new_spec, cache_spec, cache_spec,
                  pl.BlockSpec((rows, 2 * win), lambda b: (0, 0)), pl.BlockSpec((rows, 1), lambda b: (0, 0))],
        out_specs=[q_spec, cache_spec, cache_spec],
        out_shape=[jax.ShapeDtypeStruct(q.shape, F32),
                   jax.ShapeDtypeStruct(cache_k.shape, F32), jax.ShapeDtypeStruct(cache_v.shape, F32)],
        scratch_shapes=[pltpu.VMEM((2, 2 * win, kvw), F32)],
        compiler_params=_params("parallel"),
        name="sample_attn_b",
    )(q, k_new, v_new, cache_k, cache_v, bias, sink_rows)


def _cross_attn_kernel(q_ref, k_ref, v_ref, o_ref):
    scale = MEM_HEAD_DIM ** -0.5
    for h in range(MEM_HEADS):
        cols = slice(h * MEM_HEAD_DIM, (h + 1) * MEM_HEAD_DIM)
        q = q_ref[:, cols].astype(BF16)
        k = k_ref[:, cols].astype(BF16)
        v = v_ref[:, cols].astype(BF16)
        s = lax.dot_general(q, k, (((1,), (1,)), ((), ())), preferred_element_type=F32) * scale
        m = jnp.max(s, axis=-1, keepdims=True)
        e = jnp.exp(s - m)
        p = e / jnp.sum(e, axis=-1, keepdims=True)
        o_ref[:, cols] = jnp.dot(p.astype(BF16), v, preferred_element_type=F32)


def _cross_attention(q, mk, mv, *, tq):
    nb, s, width = q.shape
    mem = mk.shape[1]
    q_spec = pl.BlockSpec((None, tq, width), lambda b, i: (b, i, 0))
    m_spec = pl.BlockSpec((None, mem, width), lambda b, i: (b, 0, 0))
    return pl.pallas_call(
        _cross_attn_kernel,
        grid=(nb, s // tq),
        in_specs=[q_spec, m_spec, m_spec],
        out_specs=q_spec,
        out_shape=jax.ShapeDtypeStruct(q.shape, F32),
        compiler_params=_params("parallel", "arbitrary"),
        name="cross_attn",
    )(q, mk, mv)


def _top_rows(s, ids, k):
    n = s.shape[0]
    row = lax.broadcasted_iota(I32, s.shape, 0).astype(F32)
    vals, picked = [], []
    for _ in range(k):
        m = jnp.max(s, axis=0, keepdims=True)
        pos = jnp.min(jnp.where(s == m, row, float(n)), axis=0, keepdims=True)
        hit = row == pos
        vals.append(m)
        picked.append(pos if ids is None else jnp.max(jnp.where(hit, ids, -1.0), axis=0, keepdims=True))
        s = jnp.where(hit, -jnp.inf, s)
    return jnp.concatenate(vals, axis=0), jnp.concatenate(picked, axis=0)


def _peer_topk_kernel(q_ref, k1_ref, k2_ref, eidx_ref, gate_ref):
    half = PEER_DKEY // 2
    q = q_ref[...]
    tb = q.shape[0]
    nt = (((1,), (1,)), ((), ()))
    s1 = lax.dot_general(k1_ref[...], q[:, :half].astype(BF16), nt, preferred_element_type=F32)
    s2 = lax.dot_general(k2_ref[...], q[:, half:].astype(BF16), nt, preferred_element_type=F32)
    v1, i1 = _top_rows(s1, None, PEER_TOPK)
    v2, i2 = _top_rows(s2, None, PEER_TOPK)
    counts = [PEER_TOPK // (a + 1) for a in range(PEER_TOPK)]
    pad = -sum(counts) % SUBLANES

    def per_a(x, fill):
        rows = [jnp.broadcast_to(x[a:a + 1], (counts[a], tb)) for a in range(PEER_TOPK)]
        return jnp.concatenate(rows + [jnp.full((pad, tb), fill, F32)], axis=0)

    def per_b(x):
        return jnp.concatenate([x[0:counts[a]] for a in range(PEER_TOPK)] + [jnp.zeros((pad, tb), F32)], axis=0)

    cand = per_a(v1, -jnp.inf) + per_b(v2)
    cidx = per_a(i1, 0.0) * PEER_NKEYS + per_b(i2)
    best, eidx = _top_rows(cand, cidx, PEER_TOPK)
    e = jnp.exp(best - best[0:1])
    gate_ref[...] = e / jnp.sum(e, axis=0, keepdims=True)
    eidx_ref[...] = eidx.astype(I32)


def _peer_topk(q, sub_k1, sub_k2, *, tb):
    m = q.shape[0]
    rows = PEER_HEADS * PEER_TOPK
    key_spec = pl.BlockSpec((PEER_NKEYS, PEER_DKEY // 2), lambda i, h: (0, 0))
    out_spec = pl.BlockSpec((PEER_TOPK, tb), lambda i, h: (h, i))
    return pl.pallas_call(
        _peer_topk_kernel,
        grid=(m // tb, PEER_HEADS),
        in_specs=[pl.BlockSpec((tb, PEER_DKEY), lambda i, h: (i, h)), key_spec, key_spec],
        out_specs=[out_spec, out_spec],
        out_shape=[jax.ShapeDtypeStruct((rows, m), I32), jax.ShapeDtypeStruct((rows, m), F32)],
        compiler_params=_params("parallel", "arbitrary"),
        name="peer_topk",
    )(q, sub_k1, sub_k2)


def _tree_sum(terms):
    while len(terms) > 1:
        terms = [terms[j] + terms[j + 1] for j in range(0, len(terms) - 1, 2)] + \
            ([terms[-1]] if len(terms) % 2 else [])
    return terms[0]


def _pack_expert_tables(expert_u, expert_v):
    ub = lax.bitcast_convert_type(expert_u.astype(BF16), jnp.uint16).astype(jnp.uint32)
    vb = lax.bitcast_convert_type(expert_v.astype(BF16), jnp.uint16).astype(jnp.uint32)
    n_exp, d = expert_u.shape
    return ((ub << 16) | vb).reshape(n_exp, d // LANES, LANES)


def _peer_expert_kernel(idx_ref, idxn_ref, gate_ref, y_ref, lnx_ref, lnf_ref, tab_hbm, o_ref, *scratch, tg, n_sel, n_slots):
    bufs, (rows_ref, cols_ref, sem) = scratch[:n_slots], scratch[n_slots:]
    i = pl.program_id(0)
    last = pl.num_programs(0) - 1
    n_tiles = bufs[0].shape[1]
    high = jnp.uint32(0xFFFF0000)
    ahead = PEER_GROUPS_AHEAD

    def slab_copy(ids, row, s, t, k):
        return pltpu.make_async_copy(tab_hbm.at[ids[row, k]], bufs[s].at[t, :, k, :], sem.at[s])

    def wait_slot(s):
        pltpu.make_async_copy(bufs[s], bufs[s], sem.at[s]).wait()

    @pl.when(i == 0)
    def _():
        for g in range(ahead):
            def prime(t, carry, g=g):
                for k in range(n_sel):
                    slab_copy(idx_ref, g * tg + t, g, t, k).start()
                return carry
            lax.fori_loop(0, tg, prime, 0)

    def group(s):
        nxt = (s + ahead) % n_slots
        ids_next = idx_ref if s + ahead < n_slots else idxn_ref
        next_row0 = nxt * tg
        rows = slice(s * tg, (s + 1) * tg)
        wait_slot(s)
        y = y_ref[rows, :]
        x = y * lax.rsqrt(jnp.mean(y * y, axis=-1, keepdims=True) + EPS) * lnx_ref[...]
        for t in range(tg):
            rows_ref[0, t] = x[t:t + 1, :]
            rows_ref[1, t] = y[t:t + 1, :]
        token = lax.broadcasted_iota(I32, (n_sel, tg), 1)
        half = n_sel // 2

        per_tile = half // n_tiles

        def request(t, lo, n):
            for k in range(lo, lo + n):
                slab_copy(ids_next, next_row0 + t, nxt, t, k).start(priority=k % 2)

        def dot_pass(t, carry):
            part = None
            for c in range(n_tiles):
                request(t, c * per_tile, per_tile)
                term = lax.bitcast_convert_type(bufs[s][t, c] & high, F32) \
                    * rows_ref[0, t, :, c * LANES:(c + 1) * LANES]
                part = term if part is None else part + term
            cols_ref[t] = jnp.broadcast_to(jnp.sum(part, axis=-1, keepdims=True), (n_sel, LANES))
            return carry

        lax.fori_loop(0, tg, dot_pass, 0)
        pre = jnp.zeros((n_sel, tg), F32)
        for t in range(tg):
            pre = jnp.where(token == t, cols_ref[t][:, 0:tg], pre)
        act = 0.5 * pre * (1.0 + lax.erf(pre * (2.0 ** -0.5)))
        w = gate_ref[s] * act
        for t in range(tg):
            cols_ref[t] = jnp.broadcast_to(w[:, t:t + 1], (n_sel, LANES))

        def mix_pass(t, carry):
            wt = cols_ref[t]
            sums = []
            for c in range(n_tiles):
                request(t, half + c * per_tile, per_tile)
                prod = lax.bitcast_convert_type(bufs[s][t, c] << 16, F32) * wt
                groups = [prod[g * SUBLANES:(g + 1) * SUBLANES, :] for g in range(n_sel // SUBLANES)]
                sums.append(jnp.sum(_tree_sum(groups), axis=0, keepdims=True))
            out = rows_ref[1, t] + jnp.concatenate(sums, axis=1)
            ms = jnp.mean(out * out, axis=-1, keepdims=True)
            rows_ref[2, t] = out * lax.rsqrt(ms + EPS) * lnf_ref[...]
            return carry

        lax.fori_loop(0, tg, mix_pass, 0)
        for t in range(tg):
            o_ref[s * tg + t:s * tg + t + 1, :] = rows_ref[2, t]

    for s in range(n_slots):
        group(s)

    @pl.when(i == last)
    def _():
        for g in range(ahead):
            wait_slot(g)


def _peer_experts(eidx, gate_cols, y, ln_ffn, ln_final, table, *, tg):
    m, d = y.shape
    n_sel = eidx.shape[1]
    n_tiles = table.shape[1]
    n_slots = PEER_SLOTS
    per_step = n_slots * tg
    assert m % per_step == 0 and n_tiles * LANES == d
    n_steps = m // per_step
    row_spec = pl.BlockSpec((per_step, d), lambda i: (i, 0))
    return pl.pallas_call(
        functools.partial(_peer_expert_kernel, tg=tg, n_sel=n_sel, n_slots=n_slots),
        grid=(n_steps,),
        in_specs=[
            pl.BlockSpec((per_step, n_sel), lambda i: (i, 0), memory_space=pltpu.SMEM),
            pl.BlockSpec((per_step, n_sel), lambda i: (jnp.minimum(i + 1, n_steps - 1), 0), memory_space=pltpu.SMEM),
            pl.BlockSpec((n_slots, n_sel, tg), lambda i: (i, 0, 0)),
            row_spec,
            pl.BlockSpec((1, d), lambda i: (0, 0)),
            pl.BlockSpec((1, d), lambda i: (0, 0)),
            pl.BlockSpec(memory_space=pl.ANY),
        ],
        out_specs=row_spec,
        out_shape=jax.ShapeDtypeStruct((m, d), F32),
        scratch_shapes=[pltpu.VMEM((tg, n_tiles, n_sel, LANES), jnp.uint32)] * n_slots + [
            pltpu.VMEM((3, tg, 1, d), F32), pltpu.VMEM((tg, n_sel, LANES), F32),
            pltpu.SemaphoreType.DMA((n_slots,))],
        compiler_params=_params("arbitrary"),
        name="peer_experts",
    )(eidx, eidx, gate_cols, y, ln_ffn.reshape(1, d), ln_final.reshape(1, d), table)


def _channel_mixers(y, mk, mv, nb, p, *, tm, tq, tb, tg):
    m, d = y.shape
    q = _norm_matmul([y], [p["ln_cross"]], p["w_cq"], tm=tm, tn=MEM_WIDTH, name="cross_q_proj")
    o = _cross_attention(q.reshape(nb, m // nb, MEM_WIDTH), mk, mv, tq=tq).reshape(m, MEM_WIDTH)
    y = _norm_matmul([o], None, p["w_co"], residual=y, tm=tm, tn=512, name="cross_out_proj")
    pq = _norm_matmul([y], [p["ln_ffn"]], p["w_pq"], tm=tm, tn=512, name="peer_query_proj")
    eidx_t, gate_t = _peer_topk(pq, p["sub_k1"], p["sub_k2"], tb=tb)
    n_sel = eidx_t.shape[0]
    gate_cols = gate_t.reshape(n_sel, m // tg, tg).transpose(1, 0, 2)
    return _peer_experts(eidx_t.T, gate_cols, y, p["ln_ffn"], p["ln_final"], p["expert_table"], tg=tg)


def kernel(x_prompt, x_sample, cache_a_k, cache_a_v, cache_b_k, cache_b_v, cache_mem_k, cache_mem_v, mem_prompt, ln_mix, w_in, ln_a_out, ln_b_out, w_out, b_sinks, rel_bias, ln_cross, ln_mem, w_cq, w_ckv, w_co, ln_ffn, w_pq, sub_keys_1, sub_keys_2, expert_u, expert_v, ln_final):
    depth = w_in.shape[0]
    assert depth == 1, "the caches are laid out for a single layer"
    batch, seq, d = x_prompt.shape
    dec_batch, dec_seq, _ = x_sample.shape
    a_win = cache_a_k.shape[2]
    b_win = cache_b_k.shape[2]
    mem_len = mem_prompt.shape[1]
    assert seq == a_win == A_WIN and b_win == B_WIN and seq % BLK == 0
    l = 0
    split = 3 * A_WIDTH + B_WIDTH
    w_in_main = w_in[l, :, :split].astype(BF16)
    w_in_kvb = w_in[l, :, split:].astype(BF16)
    p = dict(ln_cross=ln_cross[l], w_cq=w_cq[l].astype(BF16), w_co=w_co[l].astype(BF16), ln_ffn=ln_ffn[l],
             w_pq=w_pq[l].astype(BF16), sub_k1=sub_keys_1[l].astype(BF16), sub_k2=sub_keys_2[l].astype(BF16),
             ln_final=ln_final, expert_table=_pack_expert_tables(expert_u[l], expert_v[l]))
    w_out_bf = w_out[l].astype(BF16)
    w_ckv_bf = w_ckv[l].astype(BF16)
    sinks = b_sinks[l].astype(F32)

    def project(x2d, tm):
        main = _norm_matmul([x2d], [ln_mix[l]], w_in_main, tm=tm, tn=512, out_split=4, name="in_proj")
        kvb = _norm_matmul([x2d], [ln_mix[l]], w_in_kvb, tm=tm, tn=B_KV_WIDTH, out_split=2, name="in_proj_kvb")
        return main, kvb

    def merge(oa, ob, resid, tm):
        return _norm_matmul([oa, ob], [ln_a_out[l], ln_b_out[l]], w_out_bf, residual=resid, tm=tm, tn=512,
                            name="mixer_out_proj")

    xp = x_prompt.reshape(batch * seq, d)
    main, kvb = project(xp, 512)
    n_delta_a = seq // BLK + 1
    bias_a = _prompt_bias_tiles(rel_bias, 0, _mixer_a_multiplicity, n_delta_a)
    window_b = lambda dist: ((dist >= 0) & (dist <= B_WIN)).astype(np.int32)
    bias_b = _prompt_bias_tiles(rel_bias, A_HEADS, window_b, B_WIN // BLK + 1)
    oa = _prompt_attention((main, 0), (main, 1), (main, 2), bias_a, None, batch=batch, seq=seq, gqa_pairs=0,
                           chunk_tiles=8)
    ob = _prompt_attention((main, 3), (kvb, 0), (kvb, 1), bias_b, sinks, batch=batch, seq=seq,
                           gqa_pairs=B_HEADS // B_KV_HEADS // 2, chunk_tiles=B_WIN // BLK + 1)
    yp = merge(oa, ob, xp, 512)
    mem_kv = _norm_matmul([mem_prompt.reshape(batch * mem_len, d)], [ln_mem[l]], w_ckv_bf, tm=512, tn=MEM_WIDTH,
                          out_split=2, name="mem_kv_proj")
    mk = mem_kv[0].reshape(batch, mem_len, MEM_WIDTH)
    mv = mem_kv[1].reshape(batch, mem_len, MEM_WIDTH)
    y_prompt = _channel_mixers(yp, mk, mv, batch, p, tm=512, tq=512, tb=128, tg=8)

    xs = x_sample.reshape(dec_batch * dec_seq, d)
    main_s, kvb_s = project(xs, 512)
    chunk = 512
    key_pos = np.arange(a_win)
    t_pos = np.arange(dec_seq)
    dist_cache = a_win + t_pos[:, None] - key_pos[None, :]
    bias_sa = _distance_bias(rel_bias, dist_cache, 0, A_HEADS, _mixer_a_multiplicity(dist_cache))
    bias_sa = bias_sa.reshape(A_HEADS * dec_seq, a_win // chunk, chunk).transpose(1, 0, 2)
    dist_new = t_pos[:, None] - np.arange(LANES)[None, :]
    mult_new = np.where(np.arange(LANES)[None, :] < dec_seq, _mixer_a_multiplicity(dist_new), 0)
    bias_sa_new = _distance_bias(rel_bias, dist_new, 0, A_HEADS, mult_new).reshape(A_HEADS * dec_seq, LANES)
    def positions_minor(x, n_pos):
        return x.reshape(dec_batch, n_pos, A_WIDTH).transpose(0, 2, 1)

    def new_tokens_tile(x):
        return jnp.pad(positions_minor(x, dec_seq), ((0, 0), (0, 0), (0, LANES - dec_seq)))

    oa_s, aks_t, avs_t = _sample_attention_a(
        main_s[0], new_tokens_tile(main_s[1]), new_tokens_tile(main_s[2]),
        positions_minor(cache_a_k[l], a_win), positions_minor(cache_a_v[l], a_win),
        bias_sa, bias_sa_new, chunk=chunk, t_new=dec_seq)
    aks = aks_t.transpose(0, 2, 1)
    avs = avs_t.transpose(0, 2, 1)
    seq_pos = np.arange(2 * b_win)
    dist_b = b_win + t_pos[:, None] - seq_pos[None, :]
    mult_b = ((dist_b >= 0) & (dist_b <= B_WIN) & (seq_pos[None, :] < b_win + dec_seq)).astype(np.int32)
    bias_sb = _distance_bias(rel_bias, dist_b, A_HEADS, B_HEADS, mult_b).reshape(B_HEADS * dec_seq, 2 * b_win)
    sink_rows = jnp.repeat(sinks, dec_seq).reshape(B_HEADS * dec_seq, 1)
    ob_s, bks, bvs = _sample_attention_b(
        main_s[3], kvb_s[0], kvb_s[1],
        cache_b_k[l].reshape(dec_batch, b_win, B_KV_WIDTH), cache_b_v[l].reshape(dec_batch, b_win, B_KV_WIDTH),
        bias_sb, sink_rows)
    ys = merge(oa_s, ob_s, xs, 512)
    mk_s = cache_mem_k[l].reshape(dec_batch, mem_len, MEM_WIDTH)
    mv_s = cache_mem_v[l].reshape(dec_batch, mem_len, MEM_WIDTH)
    y_sample = _channel_mixers(ys, mk_s, mv_s, dec_batch, p, tm=512, tq=dec_seq, tb=128, tg=8)

    def heads(x, *shape):
        return x.reshape(1, *shape)

    return (y_prompt.reshape(batch, seq, d), y_sample.reshape(dec_batch, dec_seq, d),
            heads(main[1], batch, seq, A_HEADS, HEAD_DIM), heads(main[2], batch, seq, A_HEADS, HEAD_DIM),
            heads(kvb[0].reshape(batch, seq, B_KV_WIDTH)[:, seq - b_win:], batch, b_win, B_KV_HEADS, HEAD_DIM),
            heads(kvb[1].reshape(batch, seq, B_KV_WIDTH)[:, seq - b_win:], batch, b_win, B_KV_HEADS, HEAD_DIM),
            heads(mk, batch, mem_len, MEM_HEADS, MEM_HEAD_DIM), heads(mv, batch, mem_len, MEM_HEADS, MEM_HEAD_DIM),
            heads(aks, dec_batch, a_win, A_HEADS, HEAD_DIM), heads(avs, dec_batch, a_win, A_HEADS, HEAD_DIM),
            heads(bks, dec_batch, b_win, B_KV_HEADS, HEAD_DIM), heads(bvs, dec_batch, b_win, B_KV_HEADS, HEAD_DIM))
```

```python
import functools
import math

import numpy as np
import jax
import jax.numpy as jnp
from jax import lax
from jax.experimental import pallas as pl
from jax.experimental.pallas import tpu as pltpu

F32 = jnp.float32
BF16 = jnp.bfloat16
I32 = jnp.int32

EPS = 1e-6
MASKED = -1e30

LANES = 128
SUBLANES = 8
VMEM_LIMIT = 48 * 1024 * 1024

HEAD_DIM = 64
A_HEADS = 16
A_PATTERNS = ((128, 1), (512, 4), (2048, 16))
A_WIN = 2048
B_HEADS = 16
B_KV_HEADS = 2
B_WIN = 128
A_WIDTH = A_HEADS * HEAD_DIM
B_WIDTH = B_HEADS * HEAD_DIM
B_KV_WIDTH = B_KV_HEADS * HEAD_DIM
ATTN_SCALE = HEAD_DIM ** -0.5
N_BUCKETS = 32
MAX_EXACT = N_BUCKETS // 2
MAX_DISTANCE = A_WIN
MEM_HEADS = 4
MEM_HEAD_DIM = 128
MEM_WIDTH = MEM_HEADS * MEM_HEAD_DIM
PEER_HEADS = 8
PEER_NKEYS = 128
PEER_DKEY = 256
PEER_TOPK = 16
BLK = 128
PROMPT_ROW_TILE = 1024
PROMPT_Q_TILES = 1
PEER_SLOTS = 4
PEER_GROUPS_AHEAD = 2


def _params(*semantics, flags=None):
    return pltpu.CompilerParams(dimension_semantics=semantics, vmem_limit_bytes=VMEM_LIMIT, flags=flags)


def _norm_matmul_kernel(*refs, n_groups, norm, residual):
    xs = refs[:n_groups]
    pos = n_groups
    gs = refs[pos:pos + n_groups] if norm else ()
    pos += n_groups if norm else 0
    w_ref = refs[pos]
    pos += 1
    r_ref = refs[pos] if residual else None
    pos += 1 if residual else 0
    o_ref, xn_ref = refs[pos], refs[pos + 1]

    @pl.when(pl.program_id(1) == 0)
    def _():
        off = 0
        for gi in range(n_groups):
            x = xs[gi][...]
            if norm:
                ms = jnp.mean(x * x, axis=-1, keepdims=True)
                x = x * lax.rsqrt(ms + EPS) * gs[gi][...]
            width = x.shape[-1]
            xn_ref[:, off:off + width] = x.astype(BF16)
            off += width

    acc = jnp.dot(xn_ref[...], w_ref[...], preferred_element_type=F32)
    if residual:
        acc = acc + r_ref[...]
    o_ref[...] = acc


def _norm_matmul(xs, gains, w, residual=None, *, tm, tn, out_split=1, name="norm_matmul"):
    m = xs[0].shape[0]
    k_total, n = w.shape
    assert sum(x.shape[1] for x in xs) == k_total and m % tm == 0 and n % (tn * out_split) == 0
    norm = gains is not None
    nj_per = n // out_split // tn
    in_specs = [pl.BlockSpec((tm, x.shape[1]), lambda i, j: (i, 0)) for x in xs]
    args = list(xs)
    if norm:
        in_specs += [pl.BlockSpec((1, g.shape[-1]), lambda i, j: (0, 0)) for g in gains]
        args += [g.reshape(1, -1) for g in gains]
    in_specs.append(pl.BlockSpec((k_total, tn), lambda i, j: (0, j)))
    args.append(w)
    if residual is not None:
        in_specs.append(pl.BlockSpec((tm, tn), lambda i, j: (i, j)))
        args.append(residual)
    if out_split == 1:
        out_shape = jax.ShapeDtypeStruct((m, n), F32)
        out_spec = pl.BlockSpec((tm, tn), lambda i, j: (i, j))
    else:
        out_shape = jax.ShapeDtypeStruct((out_split, m, n // out_split), F32)
        out_spec = pl.BlockSpec((None, tm, tn), lambda i, j: (j // nj_per, i, j % nj_per))
    return pl.pallas_call(
        functools.partial(_norm_matmul_kernel, n_groups=len(xs), norm=norm, residual=residual is not None),
        grid=(m // tm, n // tn),
        in_specs=in_specs,
        out_specs=out_spec,
        out_shape=out_shape,
        scratch_shapes=[pltpu.VMEM((tm, k_total), BF16)],
        compiler_params=_params("parallel", "arbitrary"),
        name=name,
    )(*args)


def _rel_bucket(dist):
    dist = np.maximum(np.asarray(dist), 0)
    ratio = np.log(np.maximum(dist, 1) / MAX_EXACT) / math.log(MAX_DISTANCE / MAX_EXACT)
    large = np.minimum(MAX_EXACT + (ratio * (N_BUCKETS - MAX_EXACT)).astype(np.int32), N_BUCKETS - 1)
    return np.where(dist < MAX_EXACT, dist, large).astype(np.int32)


def _mixer_a_multiplicity(dist):
    dist = np.asarray(dist)
    mult = np.zeros(dist.shape, np.int32)
    for window, dilation in A_PATTERNS:
        mult += ((dist >= 0) & (dist <= window) & (dist % dilation == 0)).astype(np.int32)
    return mult


def _distance_bias(rel_bias, dist, head_lo, n_heads, mult):
    table = rel_bias[:, head_lo:head_lo + n_heads].astype(F32).T
    vals = table[:, _rel_bucket(dist)]
    logm = np.log(np.maximum(mult, 1)).astype(np.float32)
    return jnp.where(jnp.asarray(mult > 0)[None], vals + jnp.asarray(logm)[None], MASKED)


def _pair_select(x, which):
    lane_head = lax.broadcasted_iota(I32, x.shape, 1) // HEAD_DIM
    swapped = pltpu.roll(x, HEAD_DIM, axis=1)
    return jnp.where(lane_head == which, x, swapped)


def _prompt_attn_kernel(*refs, n_delta, chunk_tiles, n_chunks, q_tiles, gqa_pairs, has_sink):
    if has_sink:
        q_ref, k_ref, v_ref, bias_ref, sink_ref, o_ref, m_ref, l_ref, acc_ref, kb_ref, vb_ref = refs
    else:
        q_ref, k_ref, v_ref, bias_ref, o_ref, m_ref, l_ref, acc_ref, kb_ref, vb_ref = refs
        sink_ref = None
    hp = pl.program_id(1)
    qb = pl.program_id(2)
    keys_on_lanes = n_chunks > 1

    @pl.when(qb == 0)
    def _():
        k = k_ref[...]
        v = v_ref[...]
        if gqa_pairs:
            kv_head = hp // gqa_pairs
            k = _pair_select(k, kv_head)
            v = _pair_select(v, kv_head)
        kb_ref[...] = (k.T if keys_on_lanes else k).astype(BF16)
        vb_ref[...] = v.astype(BF16)

    lane = lax.broadcasted_iota(I32, (BLK, LANES), 1)
    left = lane < HEAD_DIM
    q = q_ref[...] * ATTN_SCALE
    blocks = []
    for i in range(q_tiles):
        qi = q[i * BLK:(i + 1) * BLK]
        blocks += [jnp.where(left, qi, 0.0), jnp.where(left, 0.0, qi)]
    q2 = jnp.concatenate(blocks, axis=0).astype(BF16)
    if has_sink:
        m_ref[...] = jnp.concatenate([jnp.full((BLK, 1), sink_ref[2 * hp + h2], F32)
                                      for _ in range(q_tiles) for h2 in range(2)], axis=0)
        l_ref[...] = jnp.ones(l_ref.shape, F32)
    else:
        m_ref[...] = jnp.full(m_ref.shape, MASKED, F32)
        l_ref[...] = jnp.zeros(l_ref.shape, F32)
    acc_ref[...] = jnp.zeros(acc_ref.shape, F32)

    def chunk(start_tile):
        start = pl.multiple_of(start_tile * BLK, BLK)
        v = vb_ref[pl.ds(start, chunk_tiles * BLK), :]
        if keys_on_lanes:
            k = kb_ref[:, start_tile * BLK:(start_tile + chunk_tiles) * BLK]
        else:
            k = kb_ref[pl.ds(start, chunk_tiles * BLK), :]
        if keys_on_lanes:
            s = jnp.dot(q2, k, preferred_element_type=F32)
        else:
            s = lax.dot_general(q2, k, (((1,), (1,)), ((), ())), preferred_element_type=F32)
        rows = []
        for i in range(q_tiles):
            for h2 in range(2):
                tiles = []
                for j in range(chunk_tiles):
                    delta = qb * q_tiles + i - (start_tile + j)
                    tiles.append(bias_ref[h2, jnp.where((delta >= 0) & (delta < n_delta), delta, n_delta)])
                rows.append(jnp.concatenate(tiles, axis=1))
        s = s + jnp.concatenate(rows, axis=0)
        m_old = m_ref[...]
        m_new = jnp.maximum(m_old, jnp.max(s, axis=-1, keepdims=True))
        alpha = jnp.exp(m_old - m_new)
        p = jnp.exp(s - m_new)
        l_ref[...] = alpha * l_ref[...] + jnp.sum(p, axis=-1, keepdims=True)
        m_ref[...] = m_new
        acc_ref[...] = alpha * acc_ref[...] + jnp.dot(p.astype(BF16), v, preferred_element_type=F32)

    last_q_tile = qb * q_tiles + (q_tiles - 1)
    if n_chunks == 1:
        chunk(jnp.maximum(last_q_tile - (chunk_tiles - 1), 0))
    else:
        for c in range(n_chunks):
            pl.when(c * chunk_tiles <= last_q_tile)(functools.partial(chunk, c * chunk_tiles))
    o = acc_ref[...] / l_ref[...]
    for i in range(q_tiles):
        o_ref[i * BLK:(i + 1) * BLK, :] = jnp.where(left, o[2 * i * BLK:(2 * i + 1) * BLK],
                                                    o[(2 * i + 1) * BLK:(2 * i + 2) * BLK])


def _prompt_attention(q_src, k_src, v_src, bias, sinks, *, batch, seq, gqa_pairs, chunk_tiles, q_tiles):
    n_pairs = A_WIDTH // LANES
    n_delta = bias.shape[1] - 1
    nq = seq // BLK
    n_chunks = 1 if n_delta + q_tiles - 1 <= chunk_tiles else nq // chunk_tiles
    assert nq % q_tiles == 0 and (n_chunks == 1 or nq % chunk_tiles == 0) and chunk_tiles <= nq
    nq //= q_tiles
    q_rows = q_tiles * BLK
    (q_arr, q_idx), (k_arr, k_idx), (v_arr, v_idx) = q_src, k_src, v_src
    kv_map = (lambda which: (lambda b, hp, qb: (which, b, 0))) if gqa_pairs else \
        (lambda which: (lambda b, hp, qb: (which, b, hp)))
    in_specs = [
        pl.BlockSpec((None, q_rows, LANES), lambda b, hp, qb: (q_idx, b * nq + qb, hp)),
        pl.BlockSpec((None, seq, LANES), kv_map(k_idx)),
        pl.BlockSpec((None, seq, LANES), kv_map(v_idx)),
        pl.BlockSpec((2, n_delta + 1, BLK, BLK), lambda b, hp, qb: (hp, 0, 0, 0)),
    ]
    args = [q_arr, k_arr, v_arr, bias]
    if sinks is not None:
        in_specs.append(pl.BlockSpec(memory_space=pltpu.SMEM))
        args.append(sinks)
    return pl.pallas_call(
        functools.partial(_prompt_attn_kernel, n_delta=n_delta, chunk_tiles=chunk_tiles, n_chunks=n_chunks,
                          q_tiles=q_tiles, gqa_pairs=gqa_pairs, has_sink=sinks is not None),
        grid=(batch, n_pairs, nq),
        in_specs=in_specs,
        out_specs=pl.BlockSpec((q_rows, LANES), lambda b, hp, qb: (b * nq + qb, hp)),
        out_shape=jax.ShapeDtypeStruct((batch * seq, A_WIDTH), F32),
        scratch_shapes=[pltpu.VMEM((2 * q_rows, 1), F32), pltpu.VMEM((2 * q_rows, 1), F32),
                        pltpu.VMEM((2 * q_rows, LANES), F32),
                        pltpu.VMEM((LANES, seq) if n_chunks > 1 else (seq, LANES), BF16),
                        pltpu.VMEM((seq, LANES), BF16)],
        compiler_params=_params("parallel", "parallel", "arbitrary"),
        name="prompt_attn_b" if gqa_pairs else "prompt_attn_a",
    )(*args)


def _prompt_bias_tiles(rel_bias, head_lo, mult_fn, n_delta):
    length = (n_delta + 1) * BLK + BLK - 1
    dist = np.arange(length) - (BLK - 1)
    mult = mult_fn(dist)
    assert not mult[dist > (n_delta - 1) * BLK].any(), "the reach must end before the masked tile"
    by_dist = _distance_bias(rel_bias, dist, head_lo, 16, mult)
    seg_len = 2 * BLK - 1
    segs = jnp.stack([by_dist[:, t * BLK:t * BLK + seg_len] for t in range(n_delta + 1)], axis=1)
    x = jnp.concatenate([segs[..., BLK - 1::-1], segs[..., :BLK - 1:-1]], axis=-1)
    flat = jnp.broadcast_to(x[:, :, None, :], (16, n_delta + 1, BLK, seg_len)).reshape(16, n_delta + 1, -1)
    return flat[..., :BLK * (seg_len - 1)].reshape(16, n_delta + 1, BLK, seg_len - 1)[..., :BLK]


def _block_diag_queries(q, n_heads):
    t, width = q.shape
    rows = n_heads * t
    tiled = jnp.broadcast_to(q[None], (n_heads, t, width)).reshape(rows, width)
    row_head = lax.broadcasted_iota(I32, (rows, width), 0) // t
    lane_head = lax.broadcasted_iota(I32, (rows, width), 1) // (width // n_heads)
    return jnp.where(row_head == lane_head, tiled * ATTN_SCALE, 0.0).astype(BF16)


def _block_diag_extract(o, n_heads):
    rows, width = o.shape
    t = rows // n_heads
    row_head = lax.broadcasted_iota(I32, (rows, width), 0) // t
    lane_head = lax.broadcasted_iota(I32, (rows, width), 1) // (width // n_heads)
    return jnp.sum(jnp.where(row_head == lane_head, o, 0.0).reshape(n_heads, t, width), axis=0)


def _softmax_step(s, v_t, m_ref, l_ref, acc_ref):
    m_old = m_ref[...]
    m_new = jnp.maximum(m_old, jnp.max(s, axis=-1, keepdims=True))
    alpha = jnp.exp(m_old - m_new)
    p = jnp.exp(s - m_new)
    l_ref[...] = alpha * l_ref[...] + jnp.sum(p, axis=-1, keepdims=True)
    pv = lax.dot_general(p.astype(BF16), v_t, (((1,), (1,)), ((), ())), preferred_element_type=F32)
    acc_ref[...] = alpha * acc_ref[...] + pv
    m_ref[...] = m_new


def _sample_a_kernel(q_ref, knt_ref, vnt_ref, kc_ref, vc_ref, knext_ref, vnext_ref, bias_ref, biasn_ref,
                     o_ref, ko_ref, vo_ref, qbd_ref, m_ref, l_ref, acc_ref, *, t_new):
    c = pl.program_id(1)
    last = pl.num_programs(1) - 1
    chunk = kc_ref.shape[1]

    @pl.when(c == 0)
    def _():
        qbd_ref[...] = _block_diag_queries(q_ref[...], A_HEADS)
        m_ref[...] = jnp.full(m_ref.shape, MASKED, F32)
        l_ref[...] = jnp.zeros(l_ref.shape, F32)
        acc_ref[...] = jnp.zeros(acc_ref.shape, F32)

    kc = kc_ref[...]
    vc = vc_ref[...]
    s = jnp.dot(qbd_ref[...], kc.astype(BF16), preferred_element_type=F32)
    _softmax_step(s + bias_ref[...], vc.astype(BF16), m_ref, l_ref, acc_ref)

    at_end = c == last
    k_after = jnp.where(at_end, knt_ref[...], knext_ref[...])
    v_after = jnp.where(at_end, vnt_ref[...], vnext_ref[...])
    ko_ref[...] = jnp.concatenate([kc, k_after], axis=1)[:, t_new:t_new + chunk]
    vo_ref[...] = jnp.concatenate([vc, v_after], axis=1)[:, t_new:t_new + chunk]

    @pl.when(at_end)
    def _():
        s_new = jnp.dot(qbd_ref[...], knt_ref[...].astype(BF16), preferred_element_type=F32)
        _softmax_step(s_new + biasn_ref[...], vnt_ref[...].astype(BF16), m_ref, l_ref, acc_ref)
        o_ref[...] = _block_diag_extract(acc_ref[...] / l_ref[...], A_HEADS)


def _sample_attention_a(q, k_new_t, v_new_t, cache_k_t, cache_v_t, bias, bias_new, *, chunk, t_new):
    nb, width, win = cache_k_t.shape
    nc = win // chunk
    rows = A_HEADS * t_new
    tiles_per_chunk = chunk // LANES
    q_spec = pl.BlockSpec((t_new, width), lambda b, c: (b, 0))
    new_spec = pl.BlockSpec((None, width, LANES), lambda b, c: (b, 0, 0))
    cache_spec = pl.BlockSpec((None, width, chunk), lambda b, c: (b, 0, c))
    next_spec = pl.BlockSpec((None, width, LANES),
                             lambda b, c: (b, 0, jnp.minimum(c + 1, nc - 1) * tiles_per_chunk))
    return pl.pallas_call(
        functools.partial(_sample_a_kernel, t_new=t_new),
        grid=(nb, nc),
        in_specs=[q_spec, new_spec, new_spec, cache_spec, cache_spec, next_spec, next_spec,
                  pl.BlockSpec((None, rows, chunk), lambda b, c: (c, 0, 0)),
                  pl.BlockSpec((rows, LANES), lambda b, c: (0, 0))],
        out_specs=[q_spec, cache_spec, cache_spec],
        out_shape=[jax.ShapeDtypeStruct((nb * t_new, width), F32),
                   jax.ShapeDtypeStruct(cache_k_t.shape, F32), jax.ShapeDtypeStruct(cache_v_t.shape, F32)],
        scratch_shapes=[pltpu.VMEM((rows, width), BF16), pltpu.VMEM((rows, 1), F32), pltpu.VMEM((rows, 1), F32),
                        pltpu.VMEM((rows, width), F32)],
        compiler_params=_params("parallel", "arbitrary"),
        name="sample_attn_a",
    )(q, k_new_t, v_new_t, cache_k_t, cache_v_t, cache_k_t, cache_v_t, bias, bias_new)


def _expand_kv(x):
    first = _pair_select(x, False)
    second = _pair_select(x, True)
    reps = B_HEADS // B_KV_HEADS // 2
    return jnp.concatenate([first] * reps + [second] * reps, axis=1)


def _sample_b_kernel(q_ref, kn_ref, vn_ref, kc_ref, vc_ref, bias_ref, sink_ref, o_ref, ko_ref, vo_ref, seq_ref):
    t_new = kn_ref.shape[0]
    win = kc_ref.shape[0]
    seq_ref[...] = jnp.zeros(seq_ref.shape, F32)
    for idx, (c_ref, n_ref, out_ref) in enumerate(((kc_ref, kn_ref, ko_ref), (vc_ref, vn_ref, vo_ref))):
        seq_ref[idx, 0:win, :] = c_ref[...]
        seq_ref[idx, win:win + t_new, :] = n_ref[...]
        out_ref[...] = seq_ref[idx, t_new:win + t_new, :]
    qbd = _block_diag_queries(q_ref[...], B_HEADS)
    k = _expand_kv(seq_ref[0]).astype(BF16)
    v = _expand_kv(seq_ref[1]).astype(BF16)
    s = lax.dot_general(qbd, k, (((1,), (1,)), ((), ())), preferred_element_type=F32) + bias_ref[...]
    sink = sink_ref[...]
    m = jnp.maximum(jnp.max(s, axis=-1, keepdims=True), sink)
    p = jnp.exp(s - m)
    den = jnp.sum(p, axis=-1, keepdims=True) + jnp.exp(sink - m)
    o = jnp.dot(p.astype(BF16), v, preferred_element_type=F32) / den
    o_ref[...] = _block_diag_extract(o, B_HEADS)


def _sample_attention_b(q, k_new, v_new, cache_k, cache_v, bias, sink_rows):
    nb, win, kvw = cache_k.shape
    t = q.shape[0] // nb
    rows = B_HEADS * t
    q_spec = pl.BlockSpec((t, B_WIDTH), lambda b: (b, 0))
    new_spec = pl.BlockSpec((t, kvw), lambda b: (b, 0))
    cache_spec = pl.BlockSpec((None, win, kvw), lambda b: (b, 0, 0))
    return pl.pallas_call(
        _sample_b_kernel,
        grid=(nb,),
        in_specs=[q_spec, new_spec, new_spec, cache_spec, cache_spec,
                  pl.BlockSpec((rows, 2 * win), lambda b: (0, 0)), pl.BlockSpec((rows, 1), lambda b: (0, 0))],
        out_specs=[q_spec, cache_spec, cache_spec],
        out_shape=[jax.ShapeDtypeStruct(q.shape, F32),
                   jax.ShapeDtypeStruct(cache_k.shape, F32), jax.ShapeDtypeStruct(cache_v.shape, F32)],
        scratch_shapes=[pltpu.VMEM((2, 2 * win, kvw), F32)],
        compiler_params=_params("parallel"),
        name="sample_attn_b",
    )(q, k_new, v_new, cache_k, cache_v, bias, sink_rows)


def _cross_attn_kernel(q_ref, k_ref, v_ref, o_ref):
    scale = MEM_HEAD_DIM ** -0.5
    for h in range(MEM_HEADS):
        cols = slice(h * MEM_HEAD_DIM, (h + 1) * MEM_HEAD_DIM)
        q = q_ref[:, cols].astype(BF16)
        k = k_ref[:, cols].astype(BF16)
        v = v_ref[:, cols].astype(BF16)
        s = lax.dot_general(q, k, (((1,), (1,)), ((), ())), preferred_element_type=F32) * scale
        m = jnp.max(s, axis=-1, keepdims=True)
        e = jnp.exp(s - m)
        p = e / jnp.sum(e, axis=-1, keepdims=True)
        o_ref[:, cols] = jnp.dot(p.astype(BF16), v, preferred_element_type=F32)


def _cross_attention(q, mk, mv, *, tq):
    nb, s, width = q.shape
    mem = mk.shape[1]
    q_spec = pl.BlockSpec((None, tq, width), lambda b, i: (b, i, 0))
    m_spec = pl.BlockSpec((None, mem, width), lambda b, i: (b, 0, 0))
    return pl.pallas_call(
        _cross_attn_kernel,
        grid=(nb, s // tq),
        in_specs=[q_spec, m_spec, m_spec],
        out_specs=q_spec,
        out_shape=jax.ShapeDtypeStruct(q.shape, F32),
        compiler_params=_params("parallel", "arbitrary"),
        name="cross_attn",
    )(q, mk, mv)


def _top_rows(s, ids, k):
    n = s.shape[0]
    row = lax.broadcasted_iota(I32, s.shape, 0).astype(F32)
    vals, picked = [], []
    for _ in range(k):
        m = jnp.max(s, axis=0, keepdims=True)
        pos = jnp.min(jnp.where(s == m, row, float(n)), axis=0, keepdims=True)
        hit = row == pos
        vals.append(m)
        picked.append(pos if ids is None else jnp.max(jnp.where(hit, ids, -1.0), axis=0, keepdims=True))
        s = jnp.where(hit, -jnp.inf, s)
    return jnp.concatenate(vals, axis=0), jnp.concatenate(picked, axis=0)


def _peer_topk_kernel(q_ref, k1_ref, k2_ref, eidx_ref, gate_ref):
    half = PEER_DKEY // 2
    q = q_ref[...]
    tb = q.shape[0]
    nt = (((1,), (1,)), ((), ()))
    s1 = lax.dot_general(k1_ref[...], q[:, :half].astype(BF16), nt, preferred_element_type=F32)
    s2 = lax.dot_general(k2_ref[...], q[:, half:].astype(BF16), nt, preferred_element_type=F32)
    v1, i1 = _top_rows(s1, None, PEER_TOPK)
    v2, i2 = _top_rows(s2, None, PEER_TOPK)
    counts = [PEER_TOPK // (a + 1) for a in range(PEER_TOPK)]
    pad = -sum(counts) % SUBLANES

    def per_a(x, fill):
        rows = [jnp.broadcast_to(x[a:a + 1], (counts[a], tb)) for a in range(PEER_TOPK)]
        return jnp.concatenate(rows + [jnp.full((pad, tb), fill, F32)], axis=0)

    def per_b(x):
        return jnp.concatenate([x[0:counts[a]] for a in range(PEER_TOPK)] + [jnp.zeros((pad, tb), F32)], axis=0)

    cand = per_a(v1, -jnp.inf) + per_b(v2)
    cidx = per_a(i1, 0.0) * PEER_NKEYS + per_b(i2)
    best, eidx = _top_rows(cand, cidx, PEER_TOPK)
    e = jnp.exp(best - best[0:1])
    gate_ref[...] = e / jnp.sum(e, axis=0, keepdims=True)
    eidx_ref[...] = eidx.astype(I32)


def _peer_topk(q, sub_k1, sub_k2, *, tb):
    m = q.shape[0]
    rows = PEER_HEADS * PEER_TOPK
    key_spec = pl.BlockSpec((PEER_NKEYS, PEER_DKEY // 2), lambda i, h: (0, 0))
    out_spec = pl.BlockSpec((PEER_TOPK, tb), lambda i, h: (h, i))
    return pl.pallas_call(
        _peer_topk_kernel,
        grid=(m // tb, PEER_HEADS),
        in_specs=[pl.BlockSpec((tb, PEER_DKEY), lambda i, h: (i, h)), key_spec, key_spec],
        out_specs=[out_spec, out_spec],
        out_shape=[jax.ShapeDtypeStruct((rows, m), I32), jax.ShapeDtypeStruct((rows, m), F32)],
        compiler_params=_params("parallel", "arbitrary"),
        name="peer_topk",
    )(q, sub_k1, sub_k2)


def _tree_sum(terms):
    while len(terms) > 1:
        terms = [terms[j] + terms[j + 1] for j in range(0, len(terms) - 1, 2)] + \
            ([terms[-1]] if len(terms) % 2 else [])
    return terms[0]


def _pack_expert_tables(expert_u, expert_v):
    ub = lax.bitcast_convert_type(expert_u.astype(BF16), jnp.uint16).astype(jnp.uint32)
    vb = lax.bitcast_convert_type(expert_v.astype(BF16), jnp.uint16).astype(jnp.uint32)
    n_exp, d = expert_u.shape
    return ((ub << 16) | vb).reshape(n_exp, d // LANES, LANES)


def _peer_expert_kernel(idx_ref, idxn_ref, gate_ref, y_ref, lnx_ref, lnf_ref, tab_hbm, o_ref, *scratch, tg, n_sel, n_slots):
    bufs, (rows_ref, cols_ref, sem) = scratch[:n_slots], scratch[n_slots:]
    i = pl.program_id(0)
    last = pl.num_programs(0) - 1
    n_tiles = bufs[0].shape[1]
    high = jnp.uint32(0xFFFF0000)
    ahead = PEER_GROUPS_AHEAD

    def slab_copy(ids, row, s, t, k):
        return pltpu.make_async_copy(tab_hbm.at[ids[row, k]], bufs[s].at[t, :, k, :], sem.at[s])

    def wait_slot(s):
        pltpu.make_async_copy(bufs[s], bufs[s], sem.at[s]).wait()

    @pl.when(i == 0)
    def _():
        for g in range(ahead):
            def prime(t, carry, g=g):
                for k in range(n_sel):
                    slab_copy(idx_ref, g * tg + t, g, t, k).start()
                return carry
            lax.fori_loop(0, tg, prime, 0)

    def group(s):
        nxt = (s + ahead) % n_slots
        ids_next = idx_ref if s + ahead < n_slots else idxn_ref
        next_row0 = nxt * tg
        rows = slice(s * tg, (s + 1) * tg)
        wait_slot(s)
        y = y_ref[rows, :]
        x = y * lax.rsqrt(jnp.mean(y * y, axis=-1, keepdims=True) + EPS) * lnx_ref[...]
        for t in range(tg):
            rows_ref[0, t] = x[t:t + 1, :]
            rows_ref[1, t] = y[t:t + 1, :]
        token = lax.broadcasted_iota(I32, (n_sel, tg), 1)
        half = n_sel // 2

        per_tile = half // n_tiles

        def request(t, lo, n):
            for k in range(lo, lo + n):
                slab_copy(ids_next, next_row0 + t, nxt, t, k).start(priority=k % 2)

        def dot_pass(t, carry):
            part = None
            for c in range(n_tiles):
                request(t, c * per_tile, per_tile)
                term = lax.bitcast_convert_type(bufs[s][t, c] & high, F32) \
                    * rows_ref[0, t, :, c * LANES:(c + 1) * LANES]
                part = term if part is None else part + term
            cols_ref[t] = jnp.broadcast_to(jnp.sum(part, axis=-1, keepdims=True), (n_sel, LANES))
            return carry

        lax.fori_loop(0, tg, dot_pass, 0)
        pre = jnp.zeros((n_sel, tg), F32)
        for t in range(tg):
            pre = jnp.where(token == t, cols_ref[t][:, 0:tg], pre)
        act = 0.5 * pre * (1.0 + lax.erf(pre * (2.0 ** -0.5)))
        w = gate_ref[s] * act
        for t in range(tg):
            cols_ref[t] = jnp.broadcast_to(w[:, t:t + 1], (n_sel, LANES))

        def mix_pass(t, carry):
            wt = cols_ref[t]
            sums = []
            for c in range(n_tiles):
                request(t, half + c * per_tile, per_tile)
                prod = lax.bitcast_convert_type(bufs[s][t, c] << 16, F32) * wt
                groups = [prod[g * SUBLANES:(g + 1) * SUBLANES, :] for g in range(n_sel // SUBLANES)]
                sums.append(jnp.sum(_tree_sum(groups), axis=0, keepdims=True))
            out = rows_ref[1, t] + jnp.concatenate(sums, axis=1)
            ms = jnp.mean(out * out, axis=-1, keepdims=True)
            rows_ref[2, t] = out * lax.rsqrt(ms + EPS) * lnf_ref[...]
            return carry

        lax.fori_loop(0, tg, mix_pass, 0)
        for t in range(tg):
            o_ref[s * tg + t:s * tg + t + 1, :] = rows_ref[2, t]

    for s in range(n_slots):
        group(s)

    @pl.when(i == last)
    def _():
        for g in range(ahead):
            wait_slot(g)


def _peer_experts(eidx, gate_cols, y, ln_ffn, ln_final, table, *, tg):
    m, d = y.shape
    n_sel = eidx.shape[1]
    n_tiles = table.shape[1]
    n_slots = PEER_SLOTS
    per_step = n_slots * tg
    assert m % per_step == 0 and n_tiles * LANES == d
    n_steps = m // per_step
    row_spec = pl.BlockSpec((per_step, d), lambda i: (i, 0))
    return pl.pallas_call(
        functools.partial(_peer_expert_kernel, tg=tg, n_sel=n_sel, n_slots=n_slots),
        grid=(n_steps,),
        in_specs=[
            pl.BlockSpec((per_step, n_sel), lambda i: (i, 0), memory_space=pltpu.SMEM),
            pl.BlockSpec((per_step, n_sel), lambda i: (jnp.minimum(i + 1, n_steps - 1), 0), memory_space=pltpu.SMEM),
            pl.BlockSpec((n_slots, n_sel, tg), lambda i: (i, 0, 0)),
            row_spec,
            pl.BlockSpec((1, d), lambda i: (0, 0)),
            pl.BlockSpec((1, d), lambda i: (0, 0)),
            pl.BlockSpec(memory_space=pl.ANY),
        ],
        out_specs=row_spec,
        out_shape=jax.ShapeDtypeStruct((m, d), F32),
        scratch_shapes=[pltpu.VMEM((tg, n_tiles, n_sel, LANES), jnp.uint32)] * n_slots + [
            pltpu.VMEM((3, tg, 1, d), F32), pltpu.VMEM((tg, n_sel, LANES), F32),
            pltpu.SemaphoreType.DMA((n_slots,))],
        compiler_params=_params("arbitrary"),
        name="peer_experts",
    )(eidx, eidx, gate_cols, y, ln_ffn.reshape(1, d), ln_final.reshape(1, d), table)


def _channel_mixers(y, mk, mv, nb, p, *, tm, tq, tb, tg):
    m, d = y.shape
    q = _norm_matmul([y], [p["ln_cross"]], p["w_cq"], tm=tm, tn=MEM_WIDTH, name="cross_q_proj")
    o = _cross_attention(q.reshape(nb, m // nb, MEM_WIDTH), mk, mv, tq=tq).reshape(m, MEM_WIDTH)
    y = _norm_matmul([o], None, p["w_co"], residual=y, tm=tm, tn=512, name="cross_out_proj")
    pq = _norm_matmul([y], [p["ln_ffn"]], p["w_pq"], tm=tm, tn=512, name="peer_query_proj")
    eidx_t, gate_t = _peer_topk(pq, p["sub_k1"], p["sub_k2"], tb=tb)
    n_sel = eidx_t.shape[0]
    gate_cols = gate_t.reshape(n_sel, m // tg, tg).transpose(1, 0, 2)
    return _peer_experts(eidx_t.T, gate_cols, y, p["ln_ffn"], p["ln_final"], p["expert_table"], tg=tg)


def kernel(x_prompt, x_sample, cache_a_k, cache_a_v, cache_b_k, cache_b_v, cache_mem_k, cache_mem_v, mem_prompt, ln_mix, w_in, ln_a_out, ln_b_out, w_out, b_sinks, rel_bias, ln_cross, ln_mem, w_cq, w_ckv, w_co, ln_ffn, w_pq, sub_keys_1, sub_keys_2, expert_u, expert_v, ln_final):
    depth = w_in.shape[0]
    assert depth == 1, "the caches are laid out for a single layer"
    batch, seq, d = x_prompt.shape
    dec_batch, dec_seq, _ = x_sample.shape
    a_win = cache_a_k.shape[2]
    b_win = cache_b_k.shape[2]
    mem_len = mem_prompt.shape[1]
    assert seq == a_win == A_WIN and b_win == B_WIN and seq % BLK == 0
    l = 0
    split = 3 * A_WIDTH + B_WIDTH
    w_in_main = w_in[l, :, :split].astype(BF16)
    w_in_kvb = w_in[l, :, split:].astype(BF16)
    p = dict(ln_cross=ln_cross[l], w_cq=w_cq[l].astype(BF16), w_co=w_co[l].astype(BF16), ln_ffn=ln_ffn[l],
             w_pq=w_pq[l].astype(BF16), sub_k1=sub_keys_1[l].astype(BF16), sub_k2=sub_keys_2[l].astype(BF16),
             ln_final=ln_final, expert_table=_pack_expert_tables(expert_u[l], expert_v[l]))
    w_out_bf = w_out[l].astype(BF16)
    w_ckv_bf = w_ckv[l].astype(BF16)
    sinks = b_sinks[l].astype(F32)

    def project(x2d, tm):
        main = _norm_matmul([x2d], [ln_mix[l]], w_in_main, tm=tm, tn=512, out_split=4, name="in_proj")
        kvb = _norm_matmul([x2d], [ln_mix[l]], w_in_kvb, tm=tm, tn=B_KV_WIDTH, out_split=2, name="in_proj_kvb")
        return main, kvb

    def merge(oa, ob, resid, tm):
        return _norm_matmul([oa, ob], [ln_a_out[l], ln_b_out[l]], w_out_bf, residual=resid, tm=tm, tn=512,
                            name="mixer_out_proj")

    xp = x_prompt.reshape(batch * seq, d)
    main, kvb = project(xp, PROMPT_ROW_TILE)
    n_delta_a = seq // BLK + 1
    bias_a = _prompt_bias_tiles(rel_bias, 0, _mixer_a_multiplicity, n_delta_a)
    window_b = lambda dist: ((dist >= 0) & (dist <= B_WIN)).astype(np.int32)
    bias_b = _prompt_bias_tiles(rel_bias, A_HEADS, window_b, B_WIN // BLK + 1)
    oa = _prompt_attention((main, 0), (main, 1), (main, 2), bias_a, None, batch=batch, seq=seq, gqa_pairs=0,
                           chunk_tiles=8, q_tiles=PROMPT_Q_TILES)
    ob = _prompt_attention((main, 3), (kvb, 0), (kvb, 1), bias_b, sinks, batch=batch, seq=seq,
                           gqa_pairs=B_HEADS // B_KV_HEADS // 2, chunk_tiles=B_WIN // BLK + PROMPT_Q_TILES,
                           q_tiles=PROMPT_Q_TILES)
    yp = merge(oa, ob, xp, PROMPT_ROW_TILE)
    mem_kv = _norm_matmul([mem_prompt.reshape(batch * mem_len, d)], [ln_mem[l]], w_ckv_bf, tm=512, tn=MEM_WIDTH,
                          out_split=2, name="mem_kv_proj")
    mk = mem_kv[0].reshape(batch, mem_len, MEM_WIDTH)
    mv = mem_kv[1].reshape(batch, mem_len, MEM_WIDTH)
    y_prompt = _channel_mixers(yp, mk, mv, batch, p, tm=PROMPT_ROW_TILE, tq=512, tb=128, tg=8)

    xs = x_sample.reshape(dec_batch * dec_seq, d)
    main_s, kvb_s = project(xs, 512)
    chunk = 512
    key_pos = np.arange(a_win)
    t_pos = np.arange(dec_seq)
    dist_cache = a_win + t_pos[:, None] - key_pos[None, :]
    bias_sa = _distance_bias(rel_bias, dist_cache, 0, A_HEADS, _mixer_a_multiplicity(dist_cache))
    bias_sa = bias_sa.reshape(A_HEADS * dec_seq, a_win // chunk, chunk).transpose(1, 0, 2)
    dist_new = t_pos[:, None] - np.arange(LANES)[None, :]
    mult_new = np.where(np.arange(LANES)[None, :] < dec_seq, _mixer_a_multiplicity(dist_new), 0)
    bias_sa_new = _distance_bias(rel_bias, dist_new, 0, A_HEADS, mult_new).reshape(A_HEADS * dec_seq, LANES)
    def positions_minor(x, n_pos):
        return x.reshape(dec_batch, n_pos, A_WIDTH).transpose(0, 2, 1)

    def new_tokens_tile(x):
        return jnp.pad(positions_minor(x, dec_seq), ((0, 0), (0, 0), (0, LANES - dec_seq)))

    oa_s, aks_t, avs_t = _sample_attention_a(
        main_s[0], new_tokens_tile(main_s[1]), new_tokens_tile(main_s[2]),
        positions_minor(cache_a_k[l], a_win), positions_minor(cache_a_v[l], a_win),
        bias_sa, bias_sa_new, chunk=chunk, t_new=dec_seq)
    aks = aks_t.transpose(0, 2, 1)
    avs = avs_t.transpose(0, 2, 1)
    seq_pos = np.arange(2 * b_win)
    dist_b = b_win + t_pos[:, None] - seq_pos[None, :]
    mult_b = ((dist_b >= 0) & (dist_b <= B_WIN) & (seq_pos[None, :] < b_win + dec_seq)).astype(np.int32)
    bias_sb = _distance_bias(rel_bias, dist_b, A_HEADS, B_HEADS, mult_b).reshape(B_HEADS * dec_seq, 2 * b_win)
    sink_rows = jnp.repeat(sinks, dec_seq).reshape(B_HEADS * dec_seq, 1)
    ob_s, bks, bvs = _sample_attention_b(
        main_s[3], kvb_s[0], kvb_s[1],
        cache_b_k[l].reshape(dec_batch, b_win, B_KV_WIDTH), cache_b_v[l].reshape(dec_batch, b_win, B_KV_WIDTH),
        bias_sb, sink_rows)
    ys = merge(oa_s, ob_s, xs, 512)
    mk_s = cache_mem_k[l].reshape(dec_batch, mem_len, MEM_WIDTH)
    mv_s = cache_mem_v[l].reshape(dec_batch, mem_len, MEM_WIDTH)
    y_sample = _channel_mixers(ys, mk_s, mv_s, dec_batch, p, tm=512, tq=dec_seq, tb=128, tg=8)

    def heads(x, *shape):
        return x.reshape(1, *shape)

    return (y_prompt.reshape(batch, seq, d), y_sample.reshape(dec_batch, dec_seq, d),
            heads(main[1], batch, seq, A_HEADS, HEAD_DIM), heads(main[2], batch, seq, A_HEADS, HEAD_DIM),
            heads(kvb[0].reshape(batch, seq, B_KV_WIDTH)[:, seq - b_win:], batch, b_win, B_KV_HEADS, HEAD_DIM),
            heads(kvb[1].reshape(batch, seq, B_KV_WIDTH)[:, seq - b_win:], batch, b_win, B_KV_HEADS, HEAD_DIM),
            heads(mk, batch, mem_len, MEM_HEADS, MEM_HEAD_DIM), heads(mv, batch, mem_len, MEM_HEADS, MEM_HEAD_DIM),
            heads(aks, dec_batch, a_win, A_HEADS, HEAD_DIM), heads(avs, dec_batch, a_win, A_HEADS, HEAD_DIM),
            heads(bks, dec_batch, b_win, B_KV_HEADS, HEAD_DIM), heads(bvs, dec_batch, b_win, B_KV_HEADS, HEAD_DIM))
```

```python
import functools
import math

import numpy as np
import jax
import jax.numpy as jnp
from jax import lax
from jax.experimental import pallas as pl
from jax.experimental.pallas import tpu as pltpu

F32 = jnp.float32
BF16 = jnp.bfloat16
I32 = jnp.int32

EPS = 1e-6
MASKED = -1e30

LANES = 128
SUBLANES = 8
VMEM_LIMIT = 48 * 1024 * 1024

HEAD_DIM = 64
A_HEADS = 16
A_PATTERNS = ((128, 1), (512, 4), (2048, 16))
A_WIN = 2048
B_HEADS = 16
B_KV_HEADS = 2
B_WIN = 128
A_WIDTH = A_HEADS * HEAD_DIM
B_WIDTH = B_HEADS * HEAD_DIM
B_KV_WIDTH = B_KV_HEADS * HEAD_DIM
ATTN_SCALE = HEAD_DIM ** -0.5
N_BUCKETS = 32
MAX_EXACT = N_BUCKETS // 2
MAX_DISTANCE = A_WIN
MEM_HEADS = 4
MEM_HEAD_DIM = 128
MEM_WIDTH = MEM_HEADS * MEM_HEAD_DIM
PEER_HEADS = 8
PEER_NKEYS = 128
PEER_DKEY = 256
PEER_TOPK = 16
BLK = 128
PROMPT_ROW_TILE = 1024
PROMPT_Q_TILES = 1
PEER_SLOTS = 4
PEER_GROUPS_AHEAD = 2


def _params(*semantics, flags=None):
    return pltpu.CompilerParams(dimension_semantics=semantics, vmem_limit_bytes=VMEM_LIMIT, flags=flags)


def _norm_matmul_kernel(*refs, n_groups, norm, residual):
    xs = refs[:n_groups]
    pos = n_groups
    gs = refs[pos:pos + n_groups] if norm else ()
    pos += n_groups if norm else 0
    w_ref = refs[pos]
    pos += 1
    r_ref = refs[pos] if residual else None
    pos += 1 if residual else 0
    o_ref, xn_ref = refs[pos], refs[pos + 1]

    @pl.when(pl.program_id(1) == 0)
    def _():
        off = 0
        for gi in range(n_groups):
            x = xs[gi][...]
            if norm:
                ms = jnp.mean(x * x, axis=-1, keepdims=True)
                x = x * lax.rsqrt(ms + EPS) * gs[gi][...]
            width = x.shape[-1]
            xn_ref[:, off:off + width] = x.astype(BF16)
            off += width

    acc = jnp.dot(xn_ref[...], w_ref[...], preferred_element_type=F32)
    if residual:
        acc = acc + r_ref[...]
    o_ref[...] = acc


def _norm_matmul(xs, gains, w, residual=None, *, tm, tn, out_split=1, name="norm_matmul"):
    m = xs[0].shape[0]
    k_total, n = w.shape
    assert sum(x.shape[1] for x in xs) == k_total and m % tm == 0 and n % (tn * out_split) == 0
    norm = gains is not None
    nj_per = n // out_split // tn
    in_specs = [pl.BlockSpec((tm, x.shape[1]), lambda i, j: (i, 0)) for x in xs]
    args = list(xs)
    if norm:
        in_specs += [pl.BlockSpec((1, g.shape[-1]), lambda i, j: (0, 0)) for g in gains]
        args += [g.reshape(1, -1) for g in gains]
    in_specs.append(pl.BlockSpec((k_total, tn), lambda i, j: (0, j)))
    args.append(w)
    if residual is not None:
        in_specs.append(pl.BlockSpec((tm, tn), lambda i, j: (i, j)))
        args.append(residual)
    if out_split == 1:
        out_shape = jax.ShapeDtypeStruct((m, n), F32)
        out_spec = pl.BlockSpec((tm, tn), lambda i, j: (i, j))
    else:
        out_shape = jax.ShapeDtypeStruct((out_split, m, n // out_split), F32)
        out_spec = pl.BlockSpec((None, tm, tn), lambda i, j: (j // nj_per, i, j % nj_per))
    return pl.pallas_call(
        functools.partial(_norm_matmul_kernel, n_groups=len(xs), norm=norm, residual=residual is not None),
        grid=(m // tm, n // tn),
        in_specs=in_specs,
        out_specs=out_spec,
        out_shape=out_shape,
        scratch_shapes=[pltpu.VMEM((tm, k_total), BF16)],
        compiler_params=_params("parallel", "arbitrary"),
        name=name,
    )(*args)


def _rel_bucket(dist):
    dist = np.maximum(np.asarray(dist), 0)
    ratio = np.log(np.maximum(dist, 1) / MAX_EXACT) / math.log(MAX_DISTANCE / MAX_EXACT)
    large = np.minimum(MAX_EXACT + (ratio * (N_BUCKETS - MAX_EXACT)).astype(np.int32), N_BUCKETS - 1)
    return np.where(dist < MAX_EXACT, dist, large).astype(np.int32)


def _mixer_a_multiplicity(dist):
    dist = np.asarray(dist)
    mult = np.zeros(dist.shape, np.int32)
    for window, dilation in A_PATTERNS:
        mult += ((dist >= 0) & (dist <= window) & (dist % dilation == 0)).astype(np.int32)
    return mult


def _distance_bias(rel_bias, dist, head_lo, n_heads, mult):
    table = rel_bias[:, head_lo:head_lo + n_heads].astype(F32).T
    vals = table[:, _rel_bucket(dist)]
    logm = np.log(np.maximum(mult, 1)).astype(np.float32)
    return jnp.where(jnp.asarray(mult > 0)[None], vals + jnp.asarray(logm)[None], MASKED)


def _pair_select(x, which):
    lane_head = lax.broadcasted_iota(I32, x.shape, 1) // HEAD_DIM
    swapped = pltpu.roll(x, HEAD_DIM, axis=1)
    return jnp.where(lane_head == which, x, swapped)


def _prompt_attn_kernel(*refs, n_delta, chunk_tiles, n_chunks, q_tiles, gqa_pairs, has_sink):
    if has_sink:
        q_ref, k_ref, v_ref, bias_ref, sink_ref, o_ref, m_ref, l_ref, acc_ref, kb_ref, vb_ref = refs
    else:
        q_ref, k_ref, v_ref, bias_ref, o_ref, m_ref, l_ref, acc_ref, kb_ref, vb_ref = refs
        sink_ref = None
    hp = pl.program_id(1)
    qb = pl.program_id(2)
    keys_on_lanes = n_chunks > 1

    @pl.when(qb == 0)
    def _():
        k = k_ref[...]
        v = v_ref[...]
        if gqa_pairs:
            kv_head = hp // gqa_pairs
            k = _pair_select(k, kv_head)
            v = _pair_select(v, kv_head)
        kb_ref[...] = (k.T if keys_on_lanes else k).astype(BF16)
        vb_ref[...] = v.astype(BF16)

    lane = lax.broadcasted_iota(I32, (BLK, LANES), 1)
    left = lane < HEAD_DIM
    q = q_ref[...] * ATTN_SCALE
    blocks = []
    for i in range(q_tiles):
        qi = q[i * BLK:(i + 1) * BLK]
        blocks += [jnp.where(left, qi, 0.0), jnp.where(left, 0.0, qi)]
    q2 = jnp.concatenate(blocks, axis=0).astype(BF16)
    if has_sink:
        m_ref[...] = jnp.concatenate([jnp.full((BLK, 1), sink_ref[2 * hp + h2], F32)
                                      for _ in range(q_tiles) for h2 in range(2)], axis=0)
        l_ref[...] = jnp.ones(l_ref.shape, F32)
    else:
        m_ref[...] = jnp.full(m_ref.shape, MASKED, F32)
        l_ref[...] = jnp.zeros(l_ref.shape, F32)
    acc_ref[...] = jnp.zeros(acc_ref.shape, F32)

    def chunk(start_tile):
        start = pl.multiple_of(start_tile * BLK, BLK)
        v = vb_ref[pl.ds(start, chunk_tiles * BLK), :]
        if keys_on_lanes:
            k = kb_ref[:, start_tile * BLK:(start_tile + chunk_tiles) * BLK]
        else:
            k = kb_ref[pl.ds(start, chunk_tiles * BLK), :]
        if keys_on_lanes:
            s = jnp.dot(q2, k, preferred_element_type=F32)
        else:
            s = lax.dot_general(q2, k, (((1,), (1,)), ((), ())), preferred_element_type=F32)
        rows = []
        for i in range(q_tiles):
            for h2 in range(2):
                tiles = []
                for j in range(chunk_tiles):
                    delta = qb * q_tiles + i - (start_tile + j)
                    tiles.append(bias_ref[h2, jnp.where((delta >= 0) & (delta < n_delta), delta, n_delta)])
                rows.append(jnp.concatenate(tiles, axis=1))
        s = s + jnp.concatenate(rows, axis=0)
        m_old = m_ref[...]
        m_new = jnp.maximum(m_old, jnp.max(s, axis=-1, keepdims=True))
        alpha = jnp.exp(m_old - m_new)
        p = jnp.exp(s - m_new)
        l_ref[...] = alpha * l_ref[...] + jnp.sum(p, axis=-1, keepdims=True)
        m_ref[...] = m_new
        acc_ref[...] = alpha * acc_ref[...] + jnp.dot(p.astype(BF16), v, preferred_element_type=F32)

    last_q_tile = qb * q_tiles + (q_tiles - 1)
    if n_chunks == 1:
        chunk(jnp.maximum(last_q_tile - (chunk_tiles - 1), 0))
    else:
        for c in range(n_chunks):
            pl.when(c * chunk_tiles <= last_q_tile)(functools.partial(chunk, c * chunk_tiles))
    o = acc_ref[...] / l_ref[...]
    for i in range(q_tiles):
        o_ref[i * BLK:(i + 1) * BLK, :] = jnp.where(left, o[2 * i * BLK:(2 * i + 1) * BLK],
                                                    o[(2 * i + 1) * BLK:(2 * i + 2) * BLK])


def _prompt_attention(q_src, k_src, v_src, bias, sinks, *, batch, seq, gqa_pairs, chunk_tiles, q_tiles):
    n_pairs = A_WIDTH // LANES
    n_delta = bias.shape[1] - 1
    nq = seq // BLK
    n_chunks = 1 if n_delta + q_tiles - 1 <= chunk_tiles else nq // chunk_tiles
    assert nq % q_tiles == 0 and (n_chunks == 1 or nq % chunk_tiles == 0) and chunk_tiles <= nq
    nq //= q_tiles
    q_rows = q_tiles * BLK
    (q_arr, q_idx), (k_arr, k_idx), (v_arr, v_idx) = q_src, k_src, v_src
    kv_map = (lambda which: (lambda b, hp, qb: (which, b, 0))) if gqa_pairs else \
        (lambda which: (lambda b, hp, qb: (which, b, hp)))
    in_specs = [
        pl.BlockSpec((None, q_rows, LANES), lambda b, hp, qb: (q_idx, b * nq + qb, hp)),
        pl.BlockSpec((None, seq, LANES), kv_map(k_idx)),
        pl.BlockSpec((None, seq, LANES), kv_map(v_idx)),
        pl.BlockSpec((2, n_delta + 1, BLK, BLK), lambda b, hp, qb: (hp, 0, 0, 0)),
    ]
    args = [q_arr, k_arr, v_arr, bias]
    if sinks is not None:
        in_specs.append(pl.BlockSpec(memory_space=pltpu.SMEM))
        args.append(sinks)
    return pl.pallas_call(
        functools.partial(_prompt_attn_kernel, n_delta=n_delta, chunk_tiles=chunk_tiles, n_chunks=n_chunks,
                          q_tiles=q_tiles, gqa_pairs=gqa_pairs, has_sink=sinks is not None),
        grid=(batch, n_pairs, nq),
        in_specs=in_specs,
        out_specs=pl.BlockSpec((q_rows, LANES), lambda b, hp, qb: (b * nq + qb, hp)),
        out_shape=jax.ShapeDtypeStruct((batch * seq, A_WIDTH), F32),
        scratch_shapes=[pltpu.VMEM((2 * q_rows, 1), F32), pltpu.VMEM((2 * q_rows, 1), F32),
                        pltpu.VMEM((2 * q_rows, LANES), F32),
                        pltpu.VMEM((LANES, seq) if n_chunks > 1 else (seq, LANES), BF16),
                        pltpu.VMEM((seq, LANES), BF16)],
        compiler_params=_params("parallel", "parallel", "arbitrary"),
        name="prompt_attn_b" if gqa_pairs else "prompt_attn_a",
    )(*args)


def _prompt_bias_tiles(rel_bias, head_lo, mult_fn, n_delta):
    length = (n_delta + 1) * BLK + BLK - 1
    dist = np.arange(length) - (BLK - 1)
    mult = mult_fn(dist)
    assert not mult[dist > (n_delta - 1) * BLK].any(), "the reach must end before the masked tile"
    by_dist = _distance_bias(rel_bias, dist, head_lo, 16, mult)
    seg_len = 2 * BLK - 1
    segs = jnp.stack([by_dist[:, t * BLK:t * BLK + seg_len] for t in range(n_delta + 1)], axis=1)
    x = jnp.concatenate([segs[..., BLK - 1::-1], segs[..., :BLK - 1:-1]], axis=-1)
    flat = jnp.broadcast_to(x[:, :, None, :], (16, n_delta + 1, BLK, seg_len)).reshape(16, n_delta + 1, -1)
    return flat[..., :BLK * (seg_len - 1)].reshape(16, n_delta + 1, BLK, seg_len - 1)[..., :BLK]


def _block_diag_queries(q, n_heads):
    t, width = q.shape
    rows = n_heads * t
    tiled = jnp.broadcast_to(q[None], (n_heads, t, width)).reshape(rows, width)
    row_head = lax.broadcasted_iota(I32, (rows, width), 0) // t
    lane_head = lax.broadcasted_iota(I32, (rows, width), 1) // (width // n_heads)
    return jnp.where(row_head == lane_head, tiled * ATTN_SCALE, 0.0).astype(BF16)


def _block_diag_extract(o, n_heads):
    rows, width = o.shape
    t = rows // n_heads
    row_head = lax.broadcasted_iota(I32, (rows, width), 0) // t
    lane_head = lax.broadcasted_iota(I32, (rows, width), 1) // (width // n_heads)
    return jnp.sum(jnp.where(row_head == lane_head, o, 0.0).reshape(n_heads, t, width), axis=0)


def _softmax_step(s, v_t, m_ref, l_ref, acc_ref):
    m_old = m_ref[...]
    m_new = jnp.maximum(m_old, jnp.max(s, axis=-1, keepdims=True))
    alpha = jnp.exp(m_old - m_new)
    p = jnp.exp(s - m_new)
    l_ref[...] = alpha * l_ref[...] + jnp.sum(p, axis=-1, keepdims=True)
    pv = lax.dot_general(p.astype(BF16), v_t, (((1,), (1,)), ((), ())), preferred_element_type=F32)
    acc_ref[...] = alpha * acc_ref[...] + pv
    m_ref[...] = m_new


def _sample_a_kernel(q_ref, knt_ref, vnt_ref, kc_ref, vc_ref, bias_ref, biasn_ref,
                     o_ref, ko_ref, vo_ref, qbd_ref, m_ref, l_ref, acc_ref, kafter_ref, vafter_ref, *, t_new):
    step = pl.program_id(1)
    n_steps = pl.num_programs(1)
    chunk = kc_ref.shape[1]

    @pl.when(step == 0)
    def _():
        qbd_ref[...] = _block_diag_queries(q_ref[...], A_HEADS)
        m_ref[...] = jnp.full(m_ref.shape, MASKED, F32)
        l_ref[...] = jnp.zeros(l_ref.shape, F32)
        acc_ref[...] = jnp.zeros(acc_ref.shape, F32)
        knt = knt_ref[...]
        vnt = vnt_ref[...]
        s_new = jnp.dot(qbd_ref[...], knt.astype(BF16), preferred_element_type=F32)
        _softmax_step(s_new + biasn_ref[...], vnt.astype(BF16), m_ref, l_ref, acc_ref)
        kafter_ref[...] = knt
        vafter_ref[...] = vnt

    kc = kc_ref[...]
    vc = vc_ref[...]
    s = jnp.dot(qbd_ref[...], kc.astype(BF16), preferred_element_type=F32)
    _softmax_step(s + bias_ref[n_steps - 1 - step], vc.astype(BF16), m_ref, l_ref, acc_ref)

    ko_ref[...] = jnp.concatenate([kc, kafter_ref[...]], axis=1)[:, t_new:t_new + chunk]
    vo_ref[...] = jnp.concatenate([vc, vafter_ref[...]], axis=1)[:, t_new:t_new + chunk]
    kafter_ref[...] = kc[:, 0:LANES]
    vafter_ref[...] = vc[:, 0:LANES]

    @pl.when(step == n_steps - 1)
    def _():
        o_ref[...] = _block_diag_extract(acc_ref[...] / l_ref[...], A_HEADS)


def _sample_attention_a(q, k_new_t, v_new_t, cache_k_t, cache_v_t, bias, bias_new, *, chunk, t_new):
    nb, width, win = cache_k_t.shape
    nc = win // chunk
    rows = A_HEADS * t_new
    q_spec = pl.BlockSpec((t_new, width), lambda b, c: (b, 0))
    new_spec = pl.BlockSpec((None, width, LANES), lambda b, c: (b, 0, 0))
    cache_spec = pl.BlockSpec((None, width, chunk), lambda b, c: (b, 0, nc - 1 - c))
    return pl.pallas_call(
        functools.partial(_sample_a_kernel, t_new=t_new),
        grid=(nb, nc),
        in_specs=[q_spec, new_spec, new_spec, cache_spec, cache_spec,
                  pl.BlockSpec((nc, rows, chunk), lambda b, c: (0, 0, 0)),
                  pl.BlockSpec((rows, LANES), lambda b, c: (0, 0))],
        out_specs=[q_spec, cache_spec, cache_spec],
        out_shape=[jax.ShapeDtypeStruct((nb * t_new, width), F32),
                   jax.ShapeDtypeStruct(cache_k_t.shape, F32), jax.ShapeDtypeStruct(cache_v_t.shape, F32)],
        scratch_shapes=[pltpu.VMEM((rows, width), BF16), pltpu.VMEM((rows, 1), F32), pltpu.VMEM((rows, 1), F32),
                        pltpu.VMEM((rows, width), F32), pltpu.VMEM((width, LANES), F32),
                        pltpu.VMEM((width, LANES), F32)],
        compiler_params=_params("parallel", "arbitrary"),
        name="sample_attn_a",
    )(q, k_new_t, v_new_t, cache_k_t, cache_v_t, bias, bias_new)


def _expand_kv(x):
    first = _pair_select(x, False)
    second = _pair_select(x, True)
    reps = B_HEADS // B_KV_HEADS // 2
    return jnp.concatenate([first] * reps + [second] * reps, axis=1)


def _sample_b_kernel(q_ref, kn_ref, vn_ref, kc_ref, vc_ref, bias_ref, sink_ref, o_ref, ko_ref, vo_ref, seq_ref):
    t_new = kn_ref.shape[0]
    win = kc_ref.shape[0]
    seq_ref[...] = jnp.zeros(seq_ref.shape, F32)
    for idx, (c_ref, n_ref, out_ref) in enumerate(((kc_ref, kn_ref, ko_ref), (vc_ref, vn_ref, vo_ref))):
        seq_ref[idx, 0:win, :] = c_ref[...]
        seq_ref[idx, win:win + t_new, :] = n_ref[...]
        out_ref[...] = seq_ref[idx, t_new:win + t_new, :]
    qbd = _block_diag_queries(q_ref[...], B_HEADS)
    k = _expand_kv(seq_ref[0]).astype(BF16)
    v = _expand_kv(seq_ref[1]).astype(BF16)
    s = lax.dot_general(qbd, k, (((1,), (1,)), ((), ())), preferred_element_type=F32) + bias_ref[...]
    sink = sink_ref[...]
    m = jnp.maximum(jnp.max(s, axis=-1, keepdims=True), sink)
    p = jnp.exp(s - m)
    den = jnp.sum(p, axis=-1, keepdims=True) + jnp.exp(sink - m)
    o = jnp.dot(p.astype(BF16), v, preferred_element_type=F32) / den
    o_ref[...] = _block_diag_extract(o, B_HEADS)


def _sample_attention_b(q, k_new, v_new, cache_k, cache_v, bias, sink_rows):
    nb, win, kvw = cache_k.shape
    t = q.shape[0] // nb
    rows = B_HEADS * t
    q_spec = pl.BlockSpec((t, B_WIDTH), lambda b: (b, 0))
    new_spec = pl.BlockSpec((t, kvw), lambda b: (b, 0))
    cache_spec = pl.BlockSpec((None, win, kvw), lambda b: (b, 0, 0))
    return pl.pallas_call(
        _sample_b_kernel,
        grid=(nb,),
        in_specs=[q_spec, new_spec, new_spec, cache_spec, cache_spec,
                  pl.BlockSpec((rows, 2 * win), lambda b: (0, 0)), pl.BlockSpec((rows, 1), lambda b: (0, 0))],
        out_specs=[q_spec, cache_spec, cache_spec],
        out_shape=[jax.ShapeDtypeStruct(q.shape, F32),
                   jax.ShapeDtypeStruct(cache_k.shape, F32), jax.ShapeDtypeStruct(cache_v.shape, F32)],
        scratch_shapes=[pltpu.VMEM((2, 2 * win, kvw), F32)],
        compiler_params=_params("parallel"),
        name="sample_attn_b",
    )(q, k_new, v_new, cache_k, cache_v, bias, sink_rows)


def _cross_attn_kernel(q_ref, k_ref, v_ref, o_ref):
    scale = MEM_HEAD_DIM ** -0.5
    for h in range(MEM_HEADS):
        cols = slice(h * MEM_HEAD_DIM, (h + 1) * MEM_HEAD_DIM)
        q = q_ref[:, cols].astype(BF16)
        k = k_ref[:, cols].astype(BF16)
        v = v_ref[:, cols].astype(BF16)
        s = lax.dot_general(q, k, (((1,), (1,)), ((), ())), preferred_element_type=F32) * scale
        m = jnp.max(s, axis=-1, keepdims=True)
        e = jnp.exp(s - m)
        p = e / jnp.sum(e, axis=-1, keepdims=True)
        o_ref[:, cols] = jnp.dot(p.astype(BF16), v, preferred_element_type=F32)


def _cross_attention(q, mk, mv, *, tq):
    nb, s, width = q.shape
    mem = mk.shape[1]
    q_spec = pl.BlockSpec((None, tq, width), lambda b, i: (b, i, 0))
    m_spec = pl.BlockSpec((None, mem, width), lambda b, i: (b, 0, 0))
    return pl.pallas_call(
        _cross_attn_kernel,
        grid=(nb, s // tq),
        in_specs=[q_spec, m_spec, m_spec],
        out_specs=q_spec,
        out_shape=jax.ShapeDtypeStruct(q.shape, F32),
        compiler_params=_params("parallel", "arbitrary"),
        name="cross_attn",
    )(q, mk, mv)


def _top_rows(s, ids, k):
    n = s.shape[0]
    row = lax.broadcasted_iota(I32, s.shape, 0).astype(F32)
    vals, picked = [], []
    for _ in range(k):
        m = jnp.max(s, axis=0, keepdims=True)
        pos = jnp.min(jnp.where(s == m, row, float(n)), axis=0, keepdims=True)
        hit = row == pos
        vals.append(m)
        picked.append(pos if ids is None else jnp.max(jnp.where(hit, ids, -1.0), axis=0, keepdims=True))
        s = jnp.where(hit, -jnp.inf, s)
    return jnp.concatenate(vals, axis=0), jnp.concatenate(picked, axis=0)


def _peer_topk_kernel(q_ref, k1_ref, k2_ref, eidx_ref, gate_ref):
    half = PEER_DKEY // 2
    q = q_ref[...]
    tb = q.shape[0]
    nt = (((1,), (1,)), ((), ()))
    s1 = lax.dot_general(k1_ref[...], q[:, :half].astype(BF16), nt, preferred_element_type=F32)
    s2 = lax.dot_general(k2_ref[...], q[:, half:].astype(BF16), nt, preferred_element_type=F32)
    v1, i1 = _top_rows(s1, None, PEER_TOPK)
    v2, i2 = _top_rows(s2, None, PEER_TOPK)
    counts = [PEER_TOPK // (a + 1) for a in range(PEER_TOPK)]
    pad = -sum(counts) % SUBLANES

    def per_a(x, fill):
        rows = [jnp.broadcast_to(x[a:a + 1], (counts[a], tb)) for a in range(PEER_TOPK)]
        return jnp.concatenate(rows + [jnp.full((pad, tb), fill, F32)], axis=0)

    def per_b(x):
        return jnp.concatenate([x[0:counts[a]] for a in range(PEER_TOPK)] + [jnp.zeros((pad, tb), F32)], axis=0)

    cand = per_a(v1, -jnp.inf) + per_b(v2)
    cidx = per_a(i1, 0.0) * PEER_NKEYS + per_b(i2)
    best, eidx = _top_rows(cand, cidx, PEER_TOPK)
    e = jnp.exp(best - best[0:1])
    gate_ref[...] = e / jnp.sum(e, axis=0, keepdims=True)
    eidx_ref[...] = eidx.astype(I32)


def _peer_topk(q, sub_k1, sub_k2, *, tb):
    m = q.shape[0]
    rows = PEER_HEADS * PEER_TOPK
    key_spec = pl.BlockSpec((PEER_NKEYS, PEER_DKEY // 2), lambda i, h: (0, 0))
    out_spec = pl.BlockSpec((PEER_TOPK, tb), lambda i, h: (h, i))
    return pl.pallas_call(
        _peer_topk_kernel,
        grid=(m // tb, PEER_HEADS),
        in_specs=[pl.BlockSpec((tb, PEER_DKEY), lambda i, h: (i, h)), key_spec, key_spec],
        out_specs=[out_spec, out_spec],
        out_shape=[jax.ShapeDtypeStruct((rows, m), I32), jax.ShapeDtypeStruct((rows, m), F32)],
        compiler_params=_params("parallel", "arbitrary"),
        name="peer_topk",
    )(q, sub_k1, sub_k2)


def _tree_sum(terms):
    while len(terms) > 1:
        terms = [terms[j] + terms[j + 1] for j in range(0, len(terms) - 1, 2)] + \
            ([terms[-1]] if len(terms) % 2 else [])
    return terms[0]


def _pack_expert_tables(expert_u, expert_v):
    ub = lax.bitcast_convert_type(expert_u.astype(BF16), jnp.uint16).astype(jnp.uint32)
    vb = lax.bitcast_convert_type(expert_v.astype(BF16), jnp.uint16).astype(jnp.uint32)
    n_exp, d = expert_u.shape
    return ((ub << 16) | vb).reshape(n_exp, d // LANES, LANES)


def _peer_expert_kernel(idx_ref, idxn_ref, gate_ref, y_ref, lnx_ref, lnf_ref, tab_hbm, o_ref, *scratch, tg, n_sel, n_slots):
    bufs, (rows_ref, cols_ref, sem) = scratch[:n_slots], scratch[n_slots:]
    i = pl.program_id(0)
    last = pl.num_programs(0) - 1
    n_tiles = bufs[0].shape[1]
    high = jnp.uint32(0xFFFF0000)
    ahead = PEER_GROUPS_AHEAD

    def slab_copy(ids, row, s, t, k):
        return pltpu.make_async_copy(tab_hbm.at[ids[row, k]], bufs[s].at[t, :, k, :], sem.at[s])

    def wait_slot(s):
        pltpu.make_async_copy(bufs[s], bufs[s], sem.at[s]).wait()

    @pl.when(i == 0)
    def _():
        for g in range(ahead):
            def prime(t, carry, g=g):
                for k in range(n_sel):
                    slab_copy(idx_ref, g * tg + t, g, t, k).start()
                return carry
            lax.fori_loop(0, tg, prime, 0)

    def group(s):
        nxt = (s + ahead) % n_slots
        ids_next = idx_ref if s + ahead < n_slots else idxn_ref
        next_row0 = nxt * tg
        rows = slice(s * tg, (s + 1) * tg)
        wait_slot(s)
        y = y_ref[rows, :]
        x = y * lax.rsqrt(jnp.mean(y * y, axis=-1, keepdims=True) + EPS) * lnx_ref[...]
        for t in range(tg):
            rows_ref[0, t] = x[t:t + 1, :]
            rows_ref[1, t] = y[t:t + 1, :]
        token = lax.broadcasted_iota(I32, (n_sel, tg), 1)
        half = n_sel // 2

        per_tile = half // n_tiles

        def request(t, lo, n):
            for k in range(lo, lo + n):
                slab_copy(ids_next, next_row0 + t, nxt, t, k).start(priority=k % 2)

        def dot_pass(t, carry):
            part = None
            for c in range(n_tiles):
                request(t, c * per_tile, per_tile)
                term = lax.bitcast_convert_type(bufs[s][t, c] & high, F32) \
                    * rows_ref[0, t, :, c * LANES:(c + 1) * LANES]
                part = term if part is None else part + term
            cols_ref[t] = jnp.broadcast_to(jnp.sum(part, axis=-1, keepdims=True), (n_sel, LANES))
            return carry

        lax.fori_loop(0, tg, dot_pass, 0)
        pre = jnp.zeros((n_sel, tg), F32)
        for t in range(tg):
            pre = jnp.where(token == t, cols_ref[t][:, 0:tg], pre)
        act = 0.5 * pre * (1.0 + lax.erf(pre * (2.0 ** -0.5)))
        w = gate_ref[s] * act
        for t in range(tg):
            cols_ref[t] = jnp.broadcast_to(w[:, t:t + 1], (n_sel, LANES))

        def mix_pass(t, carry):
            wt = cols_ref[t]
            sums = []
            for c in range(n_tiles):
                request(t, half + c * per_tile, per_tile)
                prod = lax.bitcast_convert_type(bufs[s][t, c] << 16, F32) * wt
                groups = [prod[g * SUBLANES:(g + 1) * SUBLANES, :] for g in range(n_sel // SUBLANES)]
                sums.append(jnp.sum(_tree_sum(groups), axis=0, keepdims=True))
            out = rows_ref[1, t] + jnp.concatenate(sums, axis=1)
            ms = jnp.mean(out * out, axis=-1, keepdims=True)
            rows_ref[2, t] = out * lax.rsqrt(ms + EPS) * lnf_ref[...]
            return carry

        lax.fori_loop(0, tg, mix_pass, 0)
        for t in range(tg):
            o_ref[s * tg + t:s * tg + t + 1, :] = rows_ref[2, t]

    for s in range(n_slots):
        group(s)

    @pl.when(i == last)
    def _():
        for g in range(ahead):
            wait_slot(g)


def _peer_experts(eidx, gate_cols, y, ln_ffn, ln_final, table, *, tg):
    m, d = y.shape
    n_sel = eidx.shape[1]
    n_tiles = table.shape[1]
    n_slots = PEER_SLOTS
    per_step = n_slots * tg
    assert m % per_step == 0 and n_tiles * LANES == d
    n_steps = m // per_step
    row_spec = pl.BlockSpec((per_step, d), lambda i: (i, 0))
    return pl.pallas_call(
        functools.partial(_peer_expert_kernel, tg=tg, n_sel=n_sel, n_slots=n_slots),
        grid=(n_steps,),
        in_specs=[
            pl.BlockSpec((per_step, n_sel), lambda i: (i, 0), memory_space=pltpu.SMEM),
            pl.BlockSpec((per_step, n_sel), lambda i: (jnp.minimum(i + 1, n_steps - 1), 0), memory_space=pltpu.SMEM),
            pl.BlockSpec((n_slots, n_sel, tg), lambda i: (i, 0, 0)),
            row_spec,
            pl.BlockSpec((1, d), lambda i: (0, 0)),
            pl.BlockSpec((1, d), lambda i: (0, 0)),
            pl.BlockSpec(memory_space=pl.ANY),
        ],
        out_specs=row_spec,
        out_shape=jax.ShapeDtypeStruct((m, d), F32),
        scratch_shapes=[pltpu.VMEM((tg, n_tiles, n_sel, LANES), jnp.uint32)] * n_slots + [
            pltpu.VMEM((3, tg, 1, d), F32), pltpu.VMEM((tg, n_sel, LANES), F32),
            pltpu.SemaphoreType.DMA((n_slots,))],
        compiler_params=_params("arbitrary"),
        name="peer_experts",
    )(eidx, eidx, gate_cols, y, ln_ffn.reshape(1, d), ln_final.reshape(1, d), table)


def _channel_mixers(y, mk, mv, nb, p, *, tm, tq, tb, tg):
    m, d = y.shape
    q = _norm_matmul([y], [p["ln_cross"]], p["w_cq"], tm=tm, tn=MEM_WIDTH, name="cross_q_proj")
    o = _cross_attention(q.reshape(nb, m // nb, MEM_WIDTH), mk, mv, tq=tq).reshape(m, MEM_WIDTH)
    y = _norm_matmul([o], None, p["w_co"], residual=y, tm=tm, tn=512, name="cross_out_proj")
    pq = _norm_matmul([y], [p["ln_ffn"]], p["w_pq"], tm=tm, tn=512, name="peer_query_proj")
    eidx_t, gate_t = _peer_topk(pq, p["sub_k1"], p["sub_k2"], tb=tb)
    n_sel = eidx_t.shape[0]
    gate_cols = gate_t.reshape(n_sel, m // tg, tg).transpose(1, 0, 2)
    return _peer_experts(eidx_t.T, gate_cols, y, p["ln_ffn"], p["ln_final"], p["expert_table"], tg=tg)


def kernel(x_prompt, x_sample, cache_a_k, cache_a_v, cache_b_k, cache_b_v, cache_mem_k, cache_mem_v, mem_prompt, ln_mix, w_in, ln_a_out, ln_b_out, w_out, b_sinks, rel_bias, ln_cross, ln_mem, w_cq, w_ckv, w_co, ln_ffn, w_pq, sub_keys_1, sub_keys_2, expert_u, expert_v, ln_final):
    depth = w_in.shape[0]
    assert depth == 1, "the caches are laid out for a single layer"
    batch, seq, d = x_prompt.shape
    dec_batch, dec_seq, _ = x_sample.shape
    a_win = cache_a_k.shape[2]
    b_win = cache_b_k.shape[2]
    mem_len = mem_prompt.shape[1]
    assert seq == a_win == A_WIN and b_win == B_WIN and seq % BLK == 0
    l = 0
    split = 3 * A_WIDTH + B_WIDTH
    w_in_main = w_in[l, :, :split].astype(BF16)
    w_in_kvb = w_in[l, :, split:].astype(BF16)
    p = dict(ln_cross=ln_cross[l], w_cq=w_cq[l].astype(BF16), w_co=w_co[l].astype(BF16), ln_ffn=ln_ffn[l],
             w_pq=w_pq[l].astype(BF16), sub_k1=sub_keys_1[l].astype(BF16), sub_k2=sub_keys_2[l].astype(BF16),
             ln_final=ln_final, expert_table=_pack_expert_tables(expert_u[l], expert_v[l]))
    w_out_bf = w_out[l].astype(BF16)
    w_ckv_bf = w_ckv[l].astype(BF16)
    sinks = b_sinks[l].astype(F32)

    def project(x2d, tm):
        main = _norm_matmul([x2d], [ln_mix[l]], w_in_main, tm=tm, tn=512, out_split=4, name="in_proj")
        kvb = _norm_matmul([x2d], [ln_mix[l]], w_in_kvb, tm=tm, tn=B_KV_WIDTH, out_split=2, name="in_proj_kvb")
        return main, kvb

    def merge(oa, ob, resid, tm):
        return _norm_matmul([oa, ob], [ln_a_out[l], ln_b_out[l]], w_out_bf, residual=resid, tm=tm, tn=512,
                            name="mixer_out_proj")

    xp = x_prompt.reshape(batch * seq, d)
    main, kvb = project(xp, PROMPT_ROW_TILE)
    n_delta_a = seq // BLK + 1
    bias_a = _prompt_bias_tiles(rel_bias, 0, _mixer_a_multiplicity, n_delta_a)
    window_b = lambda dist: ((dist >= 0) & (dist <= B_WIN)).astype(np.int32)
    bias_b = _prompt_bias_tiles(rel_bias, A_HEADS, window_b, B_WIN // BLK + 1)
    oa = _prompt_attention((main, 0), (main, 1), (main, 2), bias_a, None, batch=batch, seq=seq, gqa_pairs=0,
                           chunk_tiles=8, q_tiles=PROMPT_Q_TILES)
    ob = _prompt_attention((main, 3), (kvb, 0), (kvb, 1), bias_b, sinks, batch=batch, seq=seq,
                           gqa_pairs=B_HEADS // B_KV_HEADS // 2, chunk_tiles=B_WIN // BLK + PROMPT_Q_TILES,
                           q_tiles=PROMPT_Q_TILES)
    yp = merge(oa, ob, xp, PROMPT_ROW_TILE)
    mem_kv = _norm_matmul([mem_prompt.reshape(batch * mem_len, d)], [ln_mem[l]], w_ckv_bf, tm=512, tn=MEM_WIDTH,
                          out_split=2, name="mem_kv_proj")
    mk = mem_kv[0].reshape(batch, mem_len, MEM_WIDTH)
    mv = mem_kv[1].reshape(batch, mem_len, MEM_WIDTH)
    y_prompt = _channel_mixers(yp, mk, mv, batch, p, tm=PROMPT_ROW_TILE, tq=512, tb=128, tg=8)

    xs = x_sample.reshape(dec_batch * dec_seq, d)
    main_s, kvb_s = project(xs, 512)
    chunk = 512
    key_pos = np.arange(a_win)
    t_pos = np.arange(dec_seq)
    dist_cache = a_win + t_pos[:, None] - key_pos[None, :]
    bias_sa = _distance_bias(rel_bias, dist_cache, 0, A_HEADS, _mixer_a_multiplicity(dist_cache))
    bias_sa = bias_sa.reshape(A_HEADS * dec_seq, a_win // chunk, chunk).transpose(1, 0, 2)
    dist_new = t_pos[:, None] - np.arange(LANES)[None, :]
    mult_new = np.where(np.arange(LANES)[None, :] < dec_seq, _mixer_a_multiplicity(dist_new), 0)
    bias_sa_new = _distance_bias(rel_bias, dist_new, 0, A_HEADS, mult_new).reshape(A_HEADS * dec_seq, LANES)
    def positions_minor(x, n_pos):
        return x.reshape(dec_batch, n_pos, A_WIDTH).transpose(0, 2, 1)

    def new_tokens_tile(x):
        return jnp.pad(positions_minor(x, dec_seq), ((0, 0), (0, 0), (0, LANES - dec_seq)))

    oa_s, aks_t, avs_t = _sample_attention_a(
        main_s[0], new_tokens_tile(main_s[1]), new_tokens_tile(main_s[2]),
        positions_minor(cache_a_k[l], a_win), positions_minor(cache_a_v[l], a_win),
        bias_sa, bias_sa_new, chunk=chunk, t_new=dec_seq)
    aks = aks_t.transpose(0, 2, 1)
    avs = avs_t.transpose(0, 2, 1)
    seq_pos = np.arange(2 * b_win)
    dist_b = b_win + t_pos[:, None] - seq_pos[None, :]
    mult_b = ((dist_b >= 0) & (dist_b <= B_WIN) & (seq_pos[None, :] < b_win + dec_seq)).astype(np.int32)
    bias_sb = _distance_bias(rel_bias, dist_b, A_HEADS, B_HEADS, mult_b).reshape(B_HEADS * dec_seq, 2 * b_win)
    sink_rows = jnp.repeat(sinks, dec_seq).reshape(B_HEADS * dec_seq, 1)
    ob_s, bks, bvs = _sample_attention_b(
        main_s[3], kvb_s[0], kvb_s[1],
        cache_b_k[l].reshape(dec_batch, b_win, B_KV_WIDTH), cache_b_v[l].reshape(dec_batch, b_win, B_KV_WIDTH),
        bias_sb, sink_rows)
    ys = merge(oa_s, ob_s, xs, 512)
    mk_s = cache_mem_k[l].reshape(dec_batch, mem_len, MEM_WIDTH)
    mv_s = cache_mem_v[l].reshape(dec_batch, mem_len, MEM_WIDTH)
    y_sample = _channel_mixers(ys, mk_s, mv_s, dec_batch, p, tm=512, tq=dec_seq, tb=128, tg=8)

    def heads(x, *shape):
        return x.reshape(1, *shape)

    return (y_prompt.reshape(batch, seq, d), y_sample.reshape(dec_batch, dec_seq, d),
            heads(main[1], batch, seq, A_HEADS, HEAD_DIM), heads(main[2], batch, seq, A_HEADS, HEAD_DIM),
            heads(kvb[0].reshape(batch, seq, B_KV_WIDTH)[:, seq - b_win:], batch, b_win, B_KV_HEADS, HEAD_DIM),
            heads(kvb[1].reshape(batch, seq, B_KV_WIDTH)[:, seq - b_win:], batch, b_win, B_KV_HEADS, HEAD_DIM),
            heads(mk, batch, mem_len, MEM_HEADS, MEM_HEAD_DIM), heads(mv, batch, mem_len, MEM_HEADS, MEM_HEAD_DIM),
            heads(aks, dec_batch, a_win, A_HEADS, HEAD_DIM), heads(avs, dec_batch, a_win, A_HEADS, HEAD_DIM),
            heads(bks, dec_batch, b_win, B_KV_HEADS, HEAD_DIM), heads(bvs, dec_batch, b_win, B_KV_HEADS, HEAD_DIM))
```

```python
import functools
import math

import numpy as np
import jax
import jax.numpy as jnp
from jax import lax
from jax.experimental import pallas as pl
from jax.experimental.pallas import tpu as pltpu

F32 = jnp.float32
BF16 = jnp.bfloat16
I32 = jnp.int32

EPS = 1e-6
MASKED = -1e30

LANES = 128
SUBLANES = 8
VMEM_LIMIT = 48 * 1024 * 1024

HEAD_DIM = 64
A_HEADS = 16
A_PATTERNS = ((128, 1), (512, 4), (2048, 16))
A_WIN = 2048
B_HEADS = 16
B_KV_HEADS = 2
B_WIN = 128
A_WIDTH = A_HEADS * HEAD_DIM
B_WIDTH = B_HEADS * HEAD_DIM
B_KV_WIDTH = B_KV_HEADS * HEAD_DIM
ATTN_SCALE = HEAD_DIM ** -0.5
N_BUCKETS = 32
MAX_EXACT = N_BUCKETS // 2
MAX_DISTANCE = A_WIN
MEM_HEADS = 4
MEM_HEAD_DIM = 128
MEM_WIDTH = MEM_HEADS * MEM_HEAD_DIM
PEER_HEADS = 8
PEER_NKEYS = 128
PEER_DKEY = 256
PEER_TOPK = 16
BLK = 128
PROMPT_ROW_TILE = 1024
PROMPT_Q_TILES = 1
PEER_TOPK_HEADS_PER_STEP = 8
PEER_SLOTS = 4
PEER_GROUPS_AHEAD = 2


def _params(*semantics, flags=None):
    return pltpu.CompilerParams(dimension_semantics=semantics, vmem_limit_bytes=VMEM_LIMIT, flags=flags)


def _norm_matmul_kernel(*refs, n_groups, norm, residual):
    xs = refs[:n_groups]
    pos = n_groups
    gs = refs[pos:pos + n_groups] if norm else ()
    pos += n_groups if norm else 0
    w_ref = refs[pos]
    pos += 1
    r_ref = refs[pos] if residual else None
    pos += 1 if residual else 0
    o_ref, xn_ref = refs[pos], refs[pos + 1]

    @pl.when(pl.program_id(1) == 0)
    def _():
        off = 0
        for gi in range(n_groups):
            x = xs[gi][...]
            if norm:
                ms = jnp.mean(x * x, axis=-1, keepdims=True)
                x = x * lax.rsqrt(ms + EPS) * gs[gi][...]
            width = x.shape[-1]
            xn_ref[:, off:off + width] = x.astype(BF16)
            off += width

    acc = jnp.dot(xn_ref[...], w_ref[...], preferred_element_type=F32)
    if residual:
        acc = acc + r_ref[...]
    o_ref[...] = acc


def _norm_matmul(xs, gains, w, residual=None, *, tm, tn, out_split=1, name="norm_matmul"):
    m = xs[0].shape[0]
    k_total, n = w.shape
    assert sum(x.shape[1] for x in xs) == k_total and m % tm == 0 and n % (tn * out_split) == 0
    norm = gains is not None
    nj_per = n // out_split // tn
    in_specs = [pl.BlockSpec((tm, x.shape[1]), lambda i, j: (i, 0)) for x in xs]
    args = list(xs)
    if norm:
        in_specs += [pl.BlockSpec((1, g.shape[-1]), lambda i, j: (0, 0)) for g in gains]
        args += [g.reshape(1, -1) for g in gains]
    in_specs.append(pl.BlockSpec((k_total, tn), lambda i, j: (0, j)))
    args.append(w)
    if residual is not None:
        in_specs.append(pl.BlockSpec((tm, tn), lambda i, j: (i, j)))
        args.append(residual)
    if out_split == 1:
        out_shape = jax.ShapeDtypeStruct((m, n), F32)
        out_spec = pl.BlockSpec((tm, tn), lambda i, j: (i, j))
    else:
        out_shape = jax.ShapeDtypeStruct((out_split, m, n // out_split), F32)
        out_spec = pl.BlockSpec((None, tm, tn), lambda i, j: (j // nj_per, i, j % nj_per))
    return pl.pallas_call(
        functools.partial(_norm_matmul_kernel, n_groups=len(xs), norm=norm, residual=residual is not None),
        grid=(m // tm, n // tn),
        in_specs=in_specs,
        out_specs=out_spec,
        out_shape=out_shape,
        scratch_shapes=[pltpu.VMEM((tm, k_total), BF16)],
        compiler_params=_params("parallel", "arbitrary"),
        name=name,
    )(*args)


def _rel_bucket(dist):
    dist = np.maximum(np.asarray(dist), 0)
    ratio = np.log(np.maximum(dist, 1) / MAX_EXACT) / math.log(MAX_DISTANCE / MAX_EXACT)
    large = np.minimum(MAX_EXACT + (ratio * (N_BUCKETS - MAX_EXACT)).astype(np.int32), N_BUCKETS - 1)
    return np.where(dist < MAX_EXACT, dist, large).astype(np.int32)


def _mixer_a_multiplicity(dist):
    dist = np.asarray(dist)
    mult = np.zeros(dist.shape, np.int32)
    for window, dilation in A_PATTERNS:
        mult += ((dist >= 0) & (dist <= window) & (dist % dilation == 0)).astype(np.int32)
    return mult


def _distance_bias(rel_bias, dist, head_lo, n_heads, mult):
    table = rel_bias[:, head_lo:head_lo + n_heads].astype(F32).T
    vals = table[:, _rel_bucket(dist)]
    logm = np.log(np.maximum(mult, 1)).astype(np.float32)
    return jnp.where(jnp.asarray(mult > 0)[None], vals + jnp.asarray(logm)[None], MASKED)


def _pair_select(x, which):
    lane_head = lax.broadcasted_iota(I32, x.shape, 1) // HEAD_DIM
    swapped = pltpu.roll(x, HEAD_DIM, axis=1)
    return jnp.where(lane_head == which, x, swapped)


def _prompt_attn_kernel(*refs, n_delta, chunk_tiles, n_chunks, q_tiles, gqa_pairs, has_sink):
    if has_sink:
        q_ref, k_ref, v_ref, bias_ref, sink_ref, o_ref, m_ref, l_ref, acc_ref, kb_ref, vb_ref = refs
    else:
        q_ref, k_ref, v_ref, bias_ref, o_ref, m_ref, l_ref, acc_ref, kb_ref, vb_ref = refs
        sink_ref = None
    hp = pl.program_id(1)
    qb = pl.program_id(2)
    keys_on_lanes = n_chunks > 1

    @pl.when(qb == 0)
    def _():
        k = k_ref[...]
        v = v_ref[...]
        if gqa_pairs:
            kv_head = hp // gqa_pairs
            k = _pair_select(k, kv_head)
            v = _pair_select(v, kv_head)
        kb_ref[...] = (k.T if keys_on_lanes else k).astype(BF16)
        vb_ref[...] = v.astype(BF16)

    lane = lax.broadcasted_iota(I32, (BLK, LANES), 1)
    left = lane < HEAD_DIM
    q = q_ref[...] * ATTN_SCALE
    blocks = []
    for i in range(q_tiles):
        qi = q[i * BLK:(i + 1) * BLK]
        blocks += [jnp.where(left, qi, 0.0), jnp.where(left, 0.0, qi)]
    q2 = jnp.concatenate(blocks, axis=0).astype(BF16)
    if has_sink:
        m_ref[...] = jnp.concatenate([jnp.full((BLK, 1), sink_ref[2 * hp + h2], F32)
                                      for _ in range(q_tiles) for h2 in range(2)], axis=0)
        l_ref[...] = jnp.ones(l_ref.shape, F32)
    else:
        m_ref[...] = jnp.full(m_ref.shape, MASKED, F32)
        l_ref[...] = jnp.zeros(l_ref.shape, F32)
    acc_ref[...] = jnp.zeros(acc_ref.shape, F32)

    def chunk(start_tile):
        start = pl.multiple_of(start_tile * BLK, BLK)
        v = vb_ref[pl.ds(start, chunk_tiles * BLK), :]
        if keys_on_lanes:
            k = kb_ref[:, start_tile * BLK:(start_tile + chunk_tiles) * BLK]
        else:
            k = kb_ref[pl.ds(start, chunk_tiles * BLK), :]
        if keys_on_lanes:
            s = jnp.dot(q2, k, preferred_element_type=F32)
        else:
            s = lax.dot_general(q2, k, (((1,), (1,)), ((), ())), preferred_element_type=F32)
        rows = []
        for i in range(q_tiles):
            for h2 in range(2):
                tiles = []
                for j in range(chunk_tiles):
                    delta = qb * q_tiles + i - (start_tile + j)
                    tiles.append(bias_ref[h2, jnp.where((delta >= 0) & (delta < n_delta), delta, n_delta)])
                rows.append(jnp.concatenate(tiles, axis=1))
        s = s + jnp.concatenate(rows, axis=0)
        m_old = m_ref[...]
        m_new = jnp.maximum(m_old, jnp.max(s, axis=-1, keepdims=True))
        alpha = jnp.exp(m_old - m_new)
        p = jnp.exp(s - m_new)
        l_ref[...] = alpha * l_ref[...] + jnp.sum(p, axis=-1, keepdims=True)
        m_ref[...] = m_new
        acc_ref[...] = alpha * acc_ref[...] + jnp.dot(p.astype(BF16), v, preferred_element_type=F32)

    last_q_tile = qb * q_tiles + (q_tiles - 1)
    if n_chunks == 1:
        chunk(jnp.maximum(last_q_tile - (chunk_tiles - 1), 0))
    else:
        for c in range(n_chunks):
            pl.when(c * chunk_tiles <= last_q_tile)(functools.partial(chunk, c * chunk_tiles))
    o = acc_ref[...] / l_ref[...]
    for i in range(q_tiles):
        o_ref[i * BLK:(i + 1) * BLK, :] = jnp.where(left, o[2 * i * BLK:(2 * i + 1) * BLK],
                                                    o[(2 * i + 1) * BLK:(2 * i + 2) * BLK])


def _prompt_attention(q_src, k_src, v_src, bias, sinks, *, batch, seq, gqa_pairs, chunk_tiles, q_tiles):
    n_pairs = A_WIDTH // LANES
    n_delta = bias.shape[1] - 1
    nq = seq // BLK
    n_chunks = 1 if n_delta + q_tiles - 1 <= chunk_tiles else nq // chunk_tiles
    assert nq % q_tiles == 0 and (n_chunks == 1 or nq % chunk_tiles == 0) and chunk_tiles <= nq
    nq //= q_tiles
    q_rows = q_tiles * BLK
    (q_arr, q_idx), (k_arr, k_idx), (v_arr, v_idx) = q_src, k_src, v_src
    kv_map = (lambda which: (lambda b, hp, qb: (which, b, 0))) if gqa_pairs else \
        (lambda which: (lambda b, hp, qb: (which, b, hp)))
    in_specs = [
        pl.BlockSpec((None, q_rows, LANES), lambda b, hp, qb: (q_idx, b * nq + qb, hp)),
        pl.BlockSpec((None, seq, LANES), kv_map(k_idx)),
        pl.BlockSpec((None, seq, LANES), kv_map(v_idx)),
        pl.BlockSpec((2, n_delta + 1, BLK, BLK), lambda b, hp, qb: (hp, 0, 0, 0)),
    ]
    args = [q_arr, k_arr, v_arr, bias]
    if sinks is not None:
        in_specs.append(pl.BlockSpec(memory_space=pltpu.SMEM))
        args.append(sinks)
    return pl.pallas_call(
        functools.partial(_prompt_attn_kernel, n_delta=n_delta, chunk_tiles=chunk_tiles, n_chunks=n_chunks,
                          q_tiles=q_tiles, gqa_pairs=gqa_pairs, has_sink=sinks is not None),
        grid=(batch, n_pairs, nq),
        in_specs=in_specs,
        out_specs=pl.BlockSpec((q_rows, LANES), lambda b, hp, qb: (b * nq + qb, hp)),
        out_shape=jax.ShapeDtypeStruct((batch * seq, A_WIDTH), F32),
        scratch_shapes=[pltpu.VMEM((2 * q_rows, 1), F32), pltpu.VMEM((2 * q_rows, 1), F32),
                        pltpu.VMEM((2 * q_rows, LANES), F32),
                        pltpu.VMEM((LANES, seq) if n_chunks > 1 else (seq, LANES), BF16),
                        pltpu.VMEM((seq, LANES), BF16)],
        compiler_params=_params("parallel", "parallel", "arbitrary"),
        name="prompt_attn_b" if gqa_pairs else "prompt_attn_a",
    )(*args)


def _prompt_bias_tiles(rel_bias, head_lo, mult_fn, n_delta):
    length = (n_delta + 1) * BLK + BLK - 1
    dist = np.arange(length) - (BLK - 1)
    mult = mult_fn(dist)
    assert not mult[dist > (n_delta - 1) * BLK].any(), "the reach must end before the masked tile"
    by_dist = _distance_bias(rel_bias, dist, head_lo, 16, mult)
    seg_len = 2 * BLK - 1
    segs = jnp.stack([by_dist[:, t * BLK:t * BLK + seg_len] for t in range(n_delta + 1)], axis=1)
    x = jnp.concatenate([segs[..., BLK - 1::-1], segs[..., :BLK - 1:-1]], axis=-1)
    flat = jnp.broadcast_to(x[:, :, None, :], (16, n_delta + 1, BLK, seg_len)).reshape(16, n_delta + 1, -1)
    return flat[..., :BLK * (seg_len - 1)].reshape(16, n_delta + 1, BLK, seg_len - 1)[..., :BLK]


def _block_diag_queries(q, n_heads):
    t, width = q.shape
    rows = n_heads * t
    tiled = jnp.broadcast_to(q[None], (n_heads, t, width)).reshape(rows, width)
    row_head = lax.broadcasted_iota(I32, (rows, width), 0) // t
    lane_head = lax.broadcasted_iota(I32, (rows, width), 1) // (width // n_heads)
    return jnp.where(row_head == lane_head, tiled * ATTN_SCALE, 0.0).astype(BF16)


def _block_diag_extract(o, n_heads):
    rows, width = o.shape
    t = rows // n_heads
    row_head = lax.broadcasted_iota(I32, (rows, width), 0) // t
    lane_head = lax.broadcasted_iota(I32, (rows, width), 1) // (width // n_heads)
    return jnp.sum(jnp.where(row_head == lane_head, o, 0.0).reshape(n_heads, t, width), axis=0)


def _softmax_step(s, v_t, m_ref, l_ref, acc_ref):
    m_old = m_ref[...]
    m_new = jnp.maximum(m_old, jnp.max(s, axis=-1, keepdims=True))
    alpha = jnp.exp(m_old - m_new)
    p = jnp.exp(s - m_new)
    l_ref[...] = alpha * l_ref[...] + jnp.sum(p, axis=-1, keepdims=True)
    pv = lax.dot_general(p.astype(BF16), v_t, (((1,), (1,)), ((), ())), preferred_element_type=F32)
    acc_ref[...] = alpha * acc_ref[...] + pv
    m_ref[...] = m_new


def _sample_a_kernel(q_ref, knt_ref, vnt_ref, kc_ref, vc_ref, knext_ref, vnext_ref, bias_ref, biasn_ref,
                     o_ref, ko_ref, vo_ref, qbd_ref, m_ref, l_ref, acc_ref, *, t_new):
    c = pl.program_id(1)
    last = pl.num_programs(1) - 1
    chunk = kc_ref.shape[1]

    @pl.when(c == 0)
    def _():
        qbd_ref[...] = _block_diag_queries(q_ref[...], A_HEADS)
        m_ref[...] = jnp.full(m_ref.shape, MASKED, F32)
        l_ref[...] = jnp.zeros(l_ref.shape, F32)
        acc_ref[...] = jnp.zeros(acc_ref.shape, F32)

    kc = kc_ref[...]
    vc = vc_ref[...]
    s = jnp.dot(qbd_ref[...], kc.astype(BF16), preferred_element_type=F32)
    _softmax_step(s + bias_ref[...], vc.astype(BF16), m_ref, l_ref, acc_ref)

    at_end = c == last
    k_after = jnp.where(at_end, knt_ref[...], knext_ref[...])
    v_after = jnp.where(at_end, vnt_ref[...], vnext_ref[...])
    ko_ref[...] = jnp.concatenate([kc, k_after], axis=1)[:, t_new:t_new + chunk]
    vo_ref[...] = jnp.concatenate([vc, v_after], axis=1)[:, t_new:t_new + chunk]

    @pl.when(at_end)
    def _():
        s_new = jnp.dot(qbd_ref[...], knt_ref[...].astype(BF16), preferred_element_type=F32)
        _softmax_step(s_new + biasn_ref[...], vnt_ref[...].astype(BF16), m_ref, l_ref, acc_ref)
        o_ref[...] = _block_diag_extract(acc_ref[...] / l_ref[...], A_HEADS)


def _sample_attention_a(q, k_new_t, v_new_t, cache_k_t, cache_v_t, bias, bias_new, *, chunk, t_new):
    nb, width, win = cache_k_t.shape
    nc = win // chunk
    rows = A_HEADS * t_new
    tiles_per_chunk = chunk // LANES
    q_spec = pl.BlockSpec((t_new, width), lambda b, c: (b, 0))
    new_spec = pl.BlockSpec((None, width, LANES), lambda b, c: (b, 0, 0))
    cache_spec = pl.BlockSpec((None, width, chunk), lambda b, c: (b, 0, c))
    next_spec = pl.BlockSpec((None, width, LANES),
                             lambda b, c: (b, 0, jnp.minimum(c + 1, nc - 1) * tiles_per_chunk))
    return pl.pallas_call(
        functools.partial(_sample_a_kernel, t_new=t_new),
        grid=(nb, nc),
        in_specs=[q_spec, new_spec, new_spec, cache_spec, cache_spec, next_spec, next_spec,
                  pl.BlockSpec((None, rows, chunk), lambda b, c: (c, 0, 0)),
                  pl.BlockSpec((rows, LANES), lambda b, c: (0, 0))],
        out_specs=[q_spec, cache_spec, cache_spec],
        out_shape=[jax.ShapeDtypeStruct((nb * t_new, width), F32),
                   jax.ShapeDtypeStruct(cache_k_t.shape, F32), jax.ShapeDtypeStruct(cache_v_t.shape, F32)],
        scratch_shapes=[pltpu.VMEM((rows, width), BF16), pltpu.VMEM((rows, 1), F32), pltpu.VMEM((rows, 1), F32),
                        pltpu.VMEM((rows, width), F32)],
        compiler_params=_params("parallel", "arbitrary"),
        name="sample_attn_a",
    )(q, k_new_t, v_new_t, cache_k_t, cache_v_t, cache_k_t, cache_v_t, bias, bias_new)


def _expand_kv(x):
    first = _pair_select(x, False)
    second = _pair_select(x, True)
    reps = B_HEADS // B_KV_HEADS // 2
    return jnp.concatenate([first] * reps + [second] * reps, axis=1)


def _sample_b_kernel(q_ref, kn_ref, vn_ref, kc_ref, vc_ref, bias_ref, sink_ref, o_ref, ko_ref, vo_ref, seq_ref):
    t_new = kn_ref.shape[0]
    win = kc_ref.shape[0]
    seq_ref[...] = jnp.zeros(seq_ref.shape, F32)
    for idx, (c_ref, n_ref, out_ref) in enumerate(((kc_ref, kn_ref, ko_ref), (vc_ref, vn_ref, vo_ref))):
        seq_ref[idx, 0:win, :] = c_ref[...]
        seq_ref[idx, win:win + t_new, :] = n_ref[...]
        out_ref[...] = seq_ref[idx, t_new:win + t_new, :]
    qbd = _block_diag_queries(q_ref[...], B_HEADS)
    k = _expand_kv(seq_ref[0]).astype(BF16)
    v = _expand_kv(seq_ref[1]).astype(BF16)
    s = lax.dot_general(qbd, k, (((1,), (1,)), ((), ())), preferred_element_type=F32) + bias_ref[...]
    sink = sink_ref[...]
    m = jnp.maximum(jnp.max(s, axis=-1, keepdims=True), sink)
    p = jnp.exp(s - m)
    den = jnp.sum(p, axis=-1, keepdims=True) + jnp.exp(sink - m)
    o = jnp.dot(p.astype(BF16), v, preferred_element_type=F32) / den
    o_ref[...] = _block_diag_extract(o, B_HEADS)


def _sample_attention_b(q, k_new, v_new, cache_k, cache_v, bias, sink_rows):
    nb, win, kvw = cache_k.shape
    t = q.shape[0] // nb
    rows = B_HEADS * t
    q_spec = pl.BlockSpec((t, B_WIDTH), lambda b: (b, 0))
    new_spec = pl.BlockSpec((t, kvw), lambda b: (b, 0))
    cache_spec = pl.BlockSpec((None, win, kvw), lambda b: (b, 0, 0))
    return pl.pallas_call(
        _sample_b_kernel,
        grid=(nb,),
        in_specs=[q_spec, new_spec, new_spec, cache_spec, cache_spec,
                  pl.BlockSpec((rows, 2 * win), lambda b: (0, 0)), pl.BlockSpec((rows, 1), lambda b: (0, 0))],
        out_specs=[q_spec, cache_spec, cache_spec],
        out_shape=[jax.ShapeDtypeStruct(q.shape, F32),
                   jax.ShapeDtypeStruct(cache_k.shape, F32), jax.ShapeDtypeStruct(cache_v.shape, F32)],
        scratch_shapes=[pltpu.VMEM((2, 2 * win, kvw), F32)],
        compiler_params=_params("parallel"),
        name="sample_attn_b",
    )(q, k_new, v_new, cache_k, cache_v, bias, sink_rows)


def _cross_attn_kernel(q_ref, k_ref, v_ref, o_ref):
    scale = MEM_HEAD_DIM ** -0.5
    for h in range(MEM_HEADS):
        cols = slice(h * MEM_HEAD_DIM, (h + 1) * MEM_HEAD_DIM)
        q = q_ref[:, cols].astype(BF16)
        k = k_ref[:, cols].astype(BF16)
        v = v_ref[:, cols].astype(BF16)
        s = lax.dot_general(q, k, (((1,), (1,)), ((), ())), preferred_element_type=F32) * scale
        m = jnp.max(s, axis=-1, keepdims=True)
        e = jnp.exp(s - m)
        p = e / jnp.sum(e, axis=-1, keepdims=True)
        o_ref[:, cols] = jnp.dot(p.astype(BF16), v, preferred_element_type=F32)


def _cross_attention(q, mk, mv, *, tq):
    nb, s, width = q.shape
    mem = mk.shape[1]
    q_spec = pl.BlockSpec((None, tq, width), lambda b, i: (b, i, 0))
    m_spec = pl.BlockSpec((None, mem, width), lambda b, i: (b, 0, 0))
    return pl.pallas_call(
        _cross_attn_kernel,
        grid=(nb, s // tq),
        in_specs=[q_spec, m_spec, m_spec],
        out_specs=q_spec,
        out_shape=jax.ShapeDtypeStruct(q.shape, F32),
        compiler_params=_params("parallel", "arbitrary"),
        name="cross_attn",
    )(q, mk, mv)


def _top_rows(s, ids, k):
    n = s.shape[0]
    row = lax.broadcasted_iota(I32, s.shape, 0).astype(F32)
    vals, picked = [], []
    for _ in range(k):
        m = jnp.max(s, axis=0, keepdims=True)
        pos = jnp.min(jnp.where(s == m, row, float(n)), axis=0, keepdims=True)
        hit = row == pos
        vals.append(m)
        picked.append(pos if ids is None else jnp.max(jnp.where(hit, ids, -1.0), axis=0, keepdims=True))
        s = jnp.where(hit, -jnp.inf, s)
    return jnp.concatenate(vals, axis=0), jnp.concatenate(picked, axis=0)


def _peer_topk_kernel(q_ref, k1_ref, k2_ref, eidx_ref, gate_ref, *, heads_per_step):
    for h in range(heads_per_step):
        eidx, gate = _peer_topk_head(q_ref[:, h * PEER_DKEY:(h + 1) * PEER_DKEY], k1_ref[...], k2_ref[...])
        gate_ref[h * PEER_TOPK:(h + 1) * PEER_TOPK, :] = gate
        eidx_ref[h * PEER_TOPK:(h + 1) * PEER_TOPK, :] = eidx


def _peer_topk_head(q, k1, k2):
    half = PEER_DKEY // 2
    tb = q.shape[0]
    nt = (((1,), (1,)), ((), ()))
    s1 = lax.dot_general(k1, q[:, :half].astype(BF16), nt, preferred_element_type=F32)
    s2 = lax.dot_general(k2, q[:, half:].astype(BF16), nt, preferred_element_type=F32)
    v1, i1 = _top_rows(s1, None, PEER_TOPK)
    v2, i2 = _top_rows(s2, None, PEER_TOPK)
    counts = [PEER_TOPK // (a + 1) for a in range(PEER_TOPK)]
    pad = -sum(counts) % SUBLANES

    def per_a(x, fill):
        rows = [jnp.broadcast_to(x[a:a + 1], (counts[a], tb)) for a in range(PEER_TOPK)]
        return jnp.concatenate(rows + [jnp.full((pad, tb), fill, F32)], axis=0)

    def per_b(x):
        return jnp.concatenate([x[0:counts[a]] for a in range(PEER_TOPK)] + [jnp.zeros((pad, tb), F32)], axis=0)

    cand = per_a(v1, -jnp.inf) + per_b(v2)
    cidx = per_a(i1, 0.0) * PEER_NKEYS + per_b(i2)
    best, eidx = _top_rows(cand, cidx, PEER_TOPK)
    e = jnp.exp(best - best[0:1])
    return eidx.astype(I32), e / jnp.sum(e, axis=0, keepdims=True)


def _peer_topk(q, sub_k1, sub_k2, *, tb):
    m = q.shape[0]
    rows = PEER_HEADS * PEER_TOPK
    hps = PEER_TOPK_HEADS_PER_STEP
    key_spec = pl.BlockSpec((PEER_NKEYS, PEER_DKEY // 2), lambda i, h: (0, 0))
    out_spec = pl.BlockSpec((hps * PEER_TOPK, tb), lambda i, h: (h, i))
    return pl.pallas_call(
        functools.partial(_peer_topk_kernel, heads_per_step=hps),
        grid=(m // tb, PEER_HEADS // hps),
        in_specs=[pl.BlockSpec((tb, hps * PEER_DKEY), lambda i, h: (i, h)), key_spec, key_spec],
        out_specs=[out_spec, out_spec],
        out_shape=[jax.ShapeDtypeStruct((rows, m), I32), jax.ShapeDtypeStruct((rows, m), F32)],
        compiler_params=_params("parallel", "arbitrary"),
        name="peer_topk",
    )(q, sub_k1, sub_k2)


def _tree_sum(terms):
    while len(terms) > 1:
        terms = [terms[j] + terms[j + 1] for j in range(0, len(terms) - 1, 2)] + \
            ([terms[-1]] if len(terms) % 2 else [])
    return terms[0]


def _pack_expert_tables(expert_u, expert_v):
    ub = lax.bitcast_convert_type(expert_u.astype(BF16), jnp.uint16).astype(jnp.uint32)
    vb = lax.bitcast_convert_type(expert_v.astype(BF16), jnp.uint16).astype(jnp.uint32)
    n_exp, d = expert_u.shape
    return ((ub << 16) | vb).reshape(n_exp, d // LANES, LANES)


def _peer_expert_kernel(idx_ref, idxn_ref, gate_ref, y_ref, lnx_ref, lnf_ref, tab_hbm, o_ref, *scratch, tg, n_sel, n_slots):
    bufs, (rows_ref, cols_ref, sem) = scratch[:n_slots], scratch[n_slots:]
    i = pl.program_id(0)
    last = pl.num_programs(0) - 1
    n_tiles = bufs[0].shape[1]
    high = jnp.uint32(0xFFFF0000)
    ahead = PEER_GROUPS_AHEAD

    def slab_copy(ids, row, s, t, k):
        return pltpu.make_async_copy(tab_hbm.at[ids[row, k]], bufs[s].at[t, :, k, :], sem.at[s])

    def wait_slot(s):
        pltpu.make_async_copy(bufs[s], bufs[s], sem.at[s]).wait()

    @pl.when(i == 0)
    def _():
        for g in range(ahead):
            def prime(t, carry, g=g):
                for k in range(n_sel):
                    slab_copy(idx_ref, g * tg + t, g, t, k).start()
                return carry
            lax.fori_loop(0, tg, prime, 0)

    def group(s):
        nxt = (s + ahead) % n_slots
        ids_next = idx_ref if s + ahead < n_slots else idxn_ref
        next_row0 = nxt * tg
        rows = slice(s * tg, (s + 1) * tg)
        wait_slot(s)
        y = y_ref[rows, :]
        x = y * lax.rsqrt(jnp.mean(y * y, axis=-1, keepdims=True) + EPS) * lnx_ref[...]
        for t in range(tg):
            rows_ref[0, t] = x[t:t + 1, :]
            rows_ref[1, t] = y[t:t + 1, :]
        token = lax.broadcasted_iota(I32, (n_sel, tg), 1)
        half = n_sel // 2

        per_tile = half // n_tiles

        def request(t, lo, n):
            for k in range(lo, lo + n):
                slab_copy(ids_next, next_row0 + t, nxt, t, k).start(priority=k % 2)

        def dot_pass(t, carry):
            part = None
            for c in range(n_tiles):
                request(t, c * per_tile, per_tile)
                term = lax.bitcast_convert_type(bufs[s][t, c] & high, F32) \
                    * rows_ref[0, t, :, c * LANES:(c + 1) * LANES]
                part = term if part is None else part + term
            cols_ref[t] = jnp.broadcast_to(jnp.sum(part, axis=-1, keepdims=True), (n_sel, LANES))
            return carry

        lax.fori_loop(0, tg, dot_pass, 0)
        pre = jnp.zeros((n_sel, tg), F32)
        for t in range(tg):
            pre = jnp.where(token == t, cols_ref[t][:, 0:tg], pre)
        act = 0.5 * pre * (1.0 + lax.erf(pre * (2.0 ** -0.5)))
        w = gate_ref[s] * act
        for t in range(tg):
            cols_ref[t] = jnp.broadcast_to(w[:, t:t + 1], (n_sel, LANES))

        def mix_pass(t, carry):
            wt = cols_ref[t]
            sums = []
            for c in range(n_tiles):
                request(t, half + c * per_tile, per_tile)
                prod = lax.bitcast_convert_type(bufs[s][t, c] << 16, F32) * wt
                groups = [prod[g * SUBLANES:(g + 1) * SUBLANES, :] for g in range(n_sel // SUBLANES)]
                sums.append(jnp.sum(_tree_sum(groups), axis=0, keepdims=True))
            out = rows_ref[1, t] + jnp.concatenate(sums, axis=1)
            ms = jnp.mean(out * out, axis=-1, keepdims=True)
            rows_ref[2, t] = out * lax.rsqrt(ms + EPS) * lnf_ref[...]
            return carry

        lax.fori_loop(0, tg, mix_pass, 0)
        for t in range(tg):
            o_ref[s * tg + t:s * tg + t + 1, :] = rows_ref[2, t]

    for s in range(n_slots):
        group(s)

    @pl.when(i == last)
    def _():
        for g in range(ahead):
            wait_slot(g)


def _peer_experts(eidx, gate_cols, y, ln_ffn, ln_final, table, *, tg):
    m, d = y.shape
    n_sel = eidx.shape[1]
    n_tiles = table.shape[1]
    n_slots = PEER_SLOTS
    per_step = n_slots * tg
    assert m % per_step == 0 and n_tiles * LANES == d
    n_steps = m // per_step
    row_spec = pl.BlockSpec((per_step, d), lambda i: (i, 0))
    return pl.pallas_call(
        functools.partial(_peer_expert_kernel, tg=tg, n_sel=n_sel, n_slots=n_slots),
        grid=(n_steps,),
        in_specs=[
            pl.BlockSpec((per_step, n_sel), lambda i: (i, 0), memory_space=pltpu.SMEM),
            pl.BlockSpec((per_step, n_sel), lambda i: (jnp.minimum(i + 1, n_steps - 1), 0), memory_space=pltpu.SMEM),
            pl.BlockSpec((n_slots, n_sel, tg), lambda i: (i, 0, 0)),
            row_spec,
            pl.BlockSpec((1, d), lambda i: (0, 0)),
            pl.BlockSpec((1, d), lambda i: (0, 0)),
            pl.BlockSpec(memory_space=pl.ANY),
        ],
        out_specs=row_spec,
        out_shape=jax.ShapeDtypeStruct((m, d), F32),
        scratch_shapes=[pltpu.VMEM((tg, n_tiles, n_sel, LANES), jnp.uint32)] * n_slots + [
            pltpu.VMEM((3, tg, 1, d), F32), pltpu.VMEM((tg, n_sel, LANES), F32),
            pltpu.SemaphoreType.DMA((n_slots,))],
        compiler_params=_params("arbitrary"),
        name="peer_experts",
    )(eidx, eidx, gate_cols, y, ln_ffn.reshape(1, d), ln_final.reshape(1, d), table)


def _channel_mixers(y, mk, mv, nb, p, *, tm, tq, tb, tg):
    m, d = y.shape
    q = _norm_matmul([y], [p["ln_cross"]], p["w_cq"], tm=tm, tn=MEM_WIDTH, name="cross_q_proj")
    o = _cross_attention(q.reshape(nb, m // nb, MEM_WIDTH), mk, mv, tq=tq).reshape(m, MEM_WIDTH)
    y = _norm_matmul([o], None, p["w_co"], residual=y, tm=tm, tn=512, name="cross_out_proj")
    pq = _norm_matmul([y], [p["ln_ffn"]], p["w_pq"], tm=tm, tn=512, name="peer_query_proj")
    eidx_t, gate_t = _peer_topk(pq, p["sub_k1"], p["sub_k2"], tb=tb)
    n_sel = eidx_t.shape[0]
    gate_cols = gate_t.reshape(n_sel, m // tg, tg).transpose(1, 0, 2)
    return _peer_experts(eidx_t.T, gate_cols, y, p["ln_ffn"], p["ln_final"], p["expert_table"], tg=tg)


def kernel(x_prompt, x_sample, cache_a_k, cache_a_v, cache_b_k, cache_b_v, cache_mem_k, cache_mem_v, mem_prompt, ln_mix, w_in, ln_a_out, ln_b_out, w_out, b_sinks, rel_bias, ln_cross, ln_mem, w_cq, w_ckv, w_co, ln_ffn, w_pq, sub_keys_1, sub_keys_2, expert_u, expert_v, ln_final):
    depth = w_in.shape[0]
    assert depth == 1, "the caches are laid out for a single layer"
    batch, seq, d = x_prompt.shape
    dec_batch, dec_seq, _ = x_sample.shape
    a_win = cache_a_k.shape[2]
    b_win = cache_b_k.shape[2]
    mem_len = mem_prompt.shape[1]
    assert seq == a_win == A_WIN and b_win == B_WIN and seq % BLK == 0
    l = 0
    split = 3 * A_WIDTH + B_WIDTH
    w_in_main = w_in[l, :, :split].astype(BF16)
    w_in_kvb = w_in[l, :, split:].astype(BF16)
    p = dict(ln_cross=ln_cross[l], w_cq=w_cq[l].astype(BF16), w_co=w_co[l].astype(BF16), ln_ffn=ln_ffn[l],
             w_pq=w_pq[l].astype(BF16), sub_k1=sub_keys_1[l].astype(BF16), sub_k2=sub_keys_2[l].astype(BF16),
             ln_final=ln_final, expert_table=_pack_expert_tables(expert_u[l], expert_v[l]))
    w_out_bf = w_out[l].astype(BF16)
    w_ckv_bf = w_ckv[l].astype(BF16)
    sinks = b_sinks[l].astype(F32)

    def project(x2d, tm):
        main = _norm_matmul([x2d], [ln_mix[l]], w_in_main, tm=tm, tn=512, out_split=4, name="in_proj")
        kvb = _norm_matmul([x2d], [ln_mix[l]], w_in_kvb, tm=tm, tn=B_KV_WIDTH, out_split=2, name="in_proj_kvb")
        return main, kvb

    def merge(oa, ob, resid, tm):
        return _norm_matmul([oa, ob], [ln_a_out[l], ln_b_out[l]], w_out_bf, residual=resid, tm=tm, tn=512,
                            name="mixer_out_proj")

    xp = x_prompt.reshape(batch * seq, d)
    main, kvb = project(xp, PROMPT_ROW_TILE)
    n_delta_a = seq // BLK + 1
    bias_a = _prompt_bias_tiles(rel_bias, 0, _mixer_a_multiplicity, n_delta_a)
    window_b = lambda dist: ((dist >= 0) & (dist <= B_WIN)).astype(np.int32)
    bias_b = _prompt_bias_tiles(rel_bias, A_HEADS, window_b, B_WIN // BLK + 1)
    oa = _prompt_attention((main, 0), (main, 1), (main, 2), bias_a, None, batch=batch, seq=seq, gqa_pairs=0,
                           chunk_tiles=8, q_tiles=PROMPT_Q_TILES)
    ob = _prompt_attention((main, 3), (kvb, 0), (kvb, 1), bias_b, sinks, batch=batch, seq=seq,
                           gqa_pairs=B_HEADS // B_KV_HEADS // 2, chunk_tiles=B_WIN // BLK + PROMPT_Q_TILES,
                           q_tiles=PROMPT_Q_TILES)
    yp = merge(oa, ob, xp, PROMPT_ROW_TILE)
    mem_kv = _norm_matmul([mem_prompt.reshape(batch * mem_len, d)], [ln_mem[l]], w_ckv_bf, tm=512, tn=MEM_WIDTH,
                          out_split=2, name="mem_kv_proj")
    mk = mem_kv[0].reshape(batch, mem_len, MEM_WIDTH)
    mv = mem_kv[1].reshape(batch, mem_len, MEM_WIDTH)
    y_prompt = _channel_mixers(yp, mk, mv, batch, p, tm=PROMPT_ROW_TILE, tq=512, tb=128, tg=8)

    xs = x_sample.reshape(dec_batch * dec_seq, d)
    main_s, kvb_s = project(xs, 512)
    chunk = 512
    key_pos = np.arange(a_win)
    t_pos = np.arange(dec_seq)
    dist_cache = a_win + t_pos[:, None] - key_pos[None, :]
    bias_sa = _distance_bias(rel_bias, dist_cache, 0, A_HEADS, _mixer_a_multiplicity(dist_cache))
    bias_sa = bias_sa.reshape(A_HEADS * dec_seq, a_win // chunk, chunk).transpose(1, 0, 2)
    dist_new = t_pos[:, None] - np.arange(LANES)[None, :]
    mult_new = np.where(np.arange(LANES)[None, :] < dec_seq, _mixer_a_multiplicity(dist_new), 0)
    bias_sa_new = _distance_bias(rel_bias, dist_new, 0, A_HEADS, mult_new).reshape(A_HEADS * dec_seq, LANES)
    def positions_minor(x, n_pos):
        return x.reshape(dec_batch, n_pos, A_WIDTH).transpose(0, 2, 1)

    def new_tokens_tile(x):
        return jnp.pad(positions_minor(x, dec_seq), ((0, 0), (0, 0), (0, LANES - dec_seq)))

    oa_s, aks_t, avs_t = _sample_attention_a(
        main_s[0], new_tokens_tile(main_s[1]), new_tokens_tile(main_s[2]),
        positions_minor(cache_a_k[l], a_win), positions_minor(cache_a_v[l], a_win),
        bias_sa, bias_sa_new, chunk=chunk, t_new=dec_seq)
    aks = aks_t.transpose(0, 2, 1)
    avs = avs_t.transpose(0, 2, 1)
    seq_pos = np.arange(2 * b_win)
    dist_b = b_win + t_pos[:, None] - seq_pos[None, :]
    mult_b = ((dist_b >= 0) & (dist_b <= B_WIN) & (seq_pos[None, :] < b_win + dec_seq)).astype(np.int32)
    bias_sb = _distance_bias(rel_bias, dist_b, A_HEADS, B_HEADS, mult_b).reshape(B_HEADS * dec_seq, 2 * b_win)
    sink_rows = jnp.repeat(sinks, dec_seq).reshape(B_HEADS * dec_seq, 1)
    ob_s, bks, bvs = _sample_attention_b(
        main_s[3], kvb_s[0], kvb_s[1],
        cache_b_k[l].reshape(dec_batch, b_win, B_KV_WIDTH), cache_b_v[l].reshape(dec_batch, b_win, B_KV_WIDTH),
        bias_sb, sink_rows)
    ys = merge(oa_s, ob_s, xs, 512)
    mk_s = cache_mem_k[l].reshape(dec_batch, mem_len, MEM_WIDTH)
    mv_s = cache_mem_v[l].reshape(dec_batch, mem_len, MEM_WIDTH)
    y_sample = _channel_mixers(ys, mk_s, mv_s, dec_batch, p, tm=512, tq=dec_seq, tb=128, tg=8)

    def heads(x, *shape):
        return x.reshape(1, *shape)

    return (y_prompt.reshape(batch, seq, d), y_sample.reshape(dec_batch, dec_seq, d),
            heads(main[1], batch, seq, A_HEADS, HEAD_DIM), heads(main[2], batch, seq, A_HEADS, HEAD_DIM),
            heads(kvb[0].reshape(batch, seq, B_KV_WIDTH)[:, seq - b_win:], batch, b_win, B_KV_HEADS, HEAD_DIM),
            heads(kvb[1].reshape(batch, seq, B_KV_WIDTH)[:, seq - b_win:], batch, b_win, B_KV_HEADS, HEAD_DIM),
            heads(mk, batch, mem_len, MEM_HEADS, MEM_HEAD_DIM), heads(mv, batch, mem_len, MEM_HEADS, MEM_HEAD_DIM),
            heads(aks, dec_batch, a_win, A_HEADS, HEAD_DIM), heads(avs, dec_batch, a_win, A_HEADS, HEAD_DIM),
            heads(bks, dec_batch, b_win, B_KV_HEADS, HEAD_DIM), heads(bvs, dec_batch, b_win, B_KV_HEADS, HEAD_DIM))
```

```python
import functools
import math

import numpy as np
import jax
import jax.numpy as jnp
from jax import lax
from jax.experimental import pallas as pl
from jax.experimental.pallas import tpu as pltpu

F32 = jnp.float32
BF16 = jnp.bfloat16
I32 = jnp.int32

EPS = 1e-6
MASKED = -1e30

LANES = 128
SUBLANES = 8
VMEM_LIMIT = 48 * 1024 * 1024

HEAD_DIM = 64
A_HEADS = 16
A_PATTERNS = ((128, 1), (512, 4), (2048, 16))
A_WIN = 2048
B_HEADS = 16
B_KV_HEADS = 2
B_WIN = 128
A_WIDTH = A_HEADS * HEAD_DIM
B_WIDTH = B_HEADS * HEAD_DIM
B_KV_WIDTH = B_KV_HEADS * HEAD_DIM
ATTN_SCALE = HEAD_DIM ** -0.5
N_BUCKETS = 32
MAX_EXACT = N_BUCKETS // 2
MAX_DISTANCE = A_WIN
MEM_HEADS = 4
MEM_HEAD_DIM = 128
MEM_WIDTH = MEM_HEADS * MEM_HEAD_DIM
PEER_HEADS = 8
PEER_NKEYS = 128
PEER_DKEY = 256
PEER_TOPK = 16
BLK = 128
PROMPT_ROW_TILE = 1024
PROMPT_Q_TILES = 1
PROMPT_PAIRS_PER_STEP = 2
PEER_TOPK_HEADS_PER_STEP = 8
PEER_SLOTS = 4
PEER_GROUPS_AHEAD = 2


def _params(*semantics, flags=None):
    return pltpu.CompilerParams(dimension_semantics=semantics, vmem_limit_bytes=VMEM_LIMIT, flags=flags)


def _norm_matmul_kernel(*refs, n_groups, norm, residual):
    xs = refs[:n_groups]
    pos = n_groups
    gs = refs[pos:pos + n_groups] if norm else ()
    pos += n_groups if norm else 0
    w_ref = refs[pos]
    pos += 1
    r_ref = refs[pos] if residual else None
    pos += 1 if residual else 0
    o_ref, xn_ref = refs[pos], refs[pos + 1]

    @pl.when(pl.program_id(1) == 0)
    def _():
        off = 0
        for gi in range(n_groups):
            x = xs[gi][...]
            if norm:
                ms = jnp.mean(x * x, axis=-1, keepdims=True)
                x = x * lax.rsqrt(ms + EPS) * gs[gi][...]
            width = x.shape[-1]
            xn_ref[:, off:off + width] = x.astype(BF16)
            off += width

    acc = jnp.dot(xn_ref[...], w_ref[...], preferred_element_type=F32)
    if residual:
        acc = acc + r_ref[...]
    o_ref[...] = acc


def _norm_matmul(xs, gains, w, residual=None, *, tm, tn, out_split=1, name="norm_matmul"):
    m = xs[0].shape[0]
    k_total, n = w.shape
    assert sum(x.shape[1] for x in xs) == k_total and m % tm == 0 and n % (tn * out_split) == 0
    norm = gains is not None
    nj_per = n // out_split // tn
    in_specs = [pl.BlockSpec((tm, x.shape[1]), lambda i, j: (i, 0)) for x in xs]
    args = list(xs)
    if norm:
        in_specs += [pl.BlockSpec((1, g.shape[-1]), lambda i, j: (0, 0)) for g in gains]
        args += [g.reshape(1, -1) for g in gains]
    in_specs.append(pl.BlockSpec((k_total, tn), lambda i, j: (0, j)))
    args.append(w)
    if residual is not None:
        in_specs.append(pl.BlockSpec((tm, tn), lambda i, j: (i, j)))
        args.append(residual)
    if out_split == 1:
        out_shape = jax.ShapeDtypeStruct((m, n), F32)
        out_spec = pl.BlockSpec((tm, tn), lambda i, j: (i, j))
    else:
        out_shape = jax.ShapeDtypeStruct((out_split, m, n // out_split), F32)
        out_spec = pl.BlockSpec((None, tm, tn), lambda i, j: (j // nj_per, i, j % nj_per))
    return pl.pallas_call(
        functools.partial(_norm_matmul_kernel, n_groups=len(xs), norm=norm, residual=residual is not None),
        grid=(m // tm, n // tn),
        in_specs=in_specs,
        out_specs=out_spec,
        out_shape=out_shape,
        scratch_shapes=[pltpu.VMEM((tm, k_total), BF16)],
        compiler_params=_params("parallel", "arbitrary"),
        name=name,
    )(*args)


def _rel_bucket(dist):
    dist = np.maximum(np.asarray(dist), 0)
    ratio = np.log(np.maximum(dist, 1) / MAX_EXACT) / math.log(MAX_DISTANCE / MAX_EXACT)
    large = np.minimum(MAX_EXACT + (ratio * (N_BUCKETS - MAX_EXACT)).astype(np.int32), N_BUCKETS - 1)
    return np.where(dist < MAX_EXACT, dist, large).astype(np.int32)


def _mixer_a_multiplicity(dist):
    dist = np.asarray(dist)
    mult = np.zeros(dist.shape, np.int32)
    for window, dilation in A_PATTERNS:
        mult += ((dist >= 0) & (dist <= window) & (dist % dilation == 0)).astype(np.int32)
    return mult


def _distance_bias(rel_bias, dist, head_lo, n_heads, mult):
    table = rel_bias[:, head_lo:head_lo + n_heads].astype(F32).T
    vals = table[:, _rel_bucket(dist)]
    logm = np.log(np.maximum(mult, 1)).astype(np.float32)
    return jnp.where(jnp.asarray(mult > 0)[None], vals + jnp.asarray(logm)[None], MASKED)


def _pair_select(x, which):
    lane_head = lax.broadcasted_iota(I32, x.shape, 1) // HEAD_DIM
    swapped = pltpu.roll(x, HEAD_DIM, axis=1)
    return jnp.where(lane_head == which, x, swapped)


def _prompt_attn_kernel(*refs, n_delta, chunk_tiles, n_chunks, q_tiles, pairs, gqa_pairs, has_sink):
    if has_sink:
        q_ref, k_ref, v_ref, bias_ref, sink_ref, o_ref, m_ref, l_ref, acc_ref, kb_ref, vb_ref = refs
    else:
        q_ref, k_ref, v_ref, bias_ref, o_ref, m_ref, l_ref, acc_ref, kb_ref, vb_ref = refs
        sink_ref = None
    hpg = pl.program_id(1)
    qb = pl.program_id(2)
    keys_on_lanes = n_chunks > 1

    def lanes_of(j):
        return slice(j * LANES, (j + 1) * LANES)

    @pl.when(qb == 0)
    def _():
        for j in range(pairs):
            if gqa_pairs:
                kv_head = (hpg * pairs + j) // gqa_pairs
                k = _pair_select(k_ref[...], kv_head)
                v = _pair_select(v_ref[...], kv_head)
            else:
                k = k_ref[:, lanes_of(j)]
                v = v_ref[:, lanes_of(j)]
            kb_ref[j] = (k.T if keys_on_lanes else k).astype(BF16)
            vb_ref[j] = v.astype(BF16)

    lane = lax.broadcasted_iota(I32, (BLK, LANES), 1)
    left = lane < HEAD_DIM
    q2s = []
    for j in range(pairs):
        q = q_ref[:, lanes_of(j)] * ATTN_SCALE
        blocks = []
        for i in range(q_tiles):
            qi = q[i * BLK:(i + 1) * BLK]
            blocks += [jnp.where(left, qi, 0.0), jnp.where(left, 0.0, qi)]
        q2s.append(jnp.concatenate(blocks, axis=0).astype(BF16))
        if has_sink:
            m_ref[j] = jnp.concatenate([jnp.full((BLK, 1), sink_ref[2 * (hpg * pairs + j) + h2], F32)
                                        for _ in range(q_tiles) for h2 in range(2)], axis=0)
            l_ref[j] = jnp.ones(l_ref.shape[1:], F32)
        else:
            m_ref[j] = jnp.full(m_ref.shape[1:], MASKED, F32)
            l_ref[j] = jnp.zeros(l_ref.shape[1:], F32)
        acc_ref[j] = jnp.zeros(acc_ref.shape[1:], F32)

    def chunk(start_tile):
        start = pl.multiple_of(start_tile * BLK, BLK)
        for jp in range(pairs):
            v = vb_ref[jp, pl.ds(start, chunk_tiles * BLK), :]
            if keys_on_lanes:
                k = kb_ref[jp, :, start_tile * BLK:(start_tile + chunk_tiles) * BLK]
                s = jnp.dot(q2s[jp], k, preferred_element_type=F32)
            else:
                k = kb_ref[jp, pl.ds(start, chunk_tiles * BLK), :]
                s = lax.dot_general(q2s[jp], k, (((1,), (1,)), ((), ())), preferred_element_type=F32)
            rows = []
            for i in range(q_tiles):
                for h2 in range(2):
                    tiles = []
                    for j in range(chunk_tiles):
                        delta = qb * q_tiles + i - (start_tile + j)
                        tiles.append(bias_ref[2 * jp + h2,
                                              jnp.where((delta >= 0) & (delta < n_delta), delta, n_delta)])
                    rows.append(jnp.concatenate(tiles, axis=1))
            s = s + jnp.concatenate(rows, axis=0)
            m_old = m_ref[jp]
            m_new = jnp.maximum(m_old, jnp.max(s, axis=-1, keepdims=True))
            alpha = jnp.exp(m_old - m_new)
            p = jnp.exp(s - m_new)
            l_ref[jp] = alpha * l_ref[jp] + jnp.sum(p, axis=-1, keepdims=True)
            m_ref[jp] = m_new
            acc_ref[jp] = alpha * acc_ref[jp] + jnp.dot(p.astype(BF16), v, preferred_element_type=F32)

    last_q_tile = qb * q_tiles + (q_tiles - 1)
    if n_chunks == 1:
        chunk(jnp.maximum(last_q_tile - (chunk_tiles - 1), 0))
    else:
        for c in range(n_chunks):
            pl.when(c * chunk_tiles <= last_q_tile)(functools.partial(chunk, c * chunk_tiles))
    for jp in range(pairs):
        o = acc_ref[jp] / l_ref[jp]
        for i in range(q_tiles):
            o_ref[i * BLK:(i + 1) * BLK, lanes_of(jp)] = jnp.where(left, o[2 * i * BLK:(2 * i + 1) * BLK],
                                                                   o[(2 * i + 1) * BLK:(2 * i + 2) * BLK])


def _prompt_attention(q_src, k_src, v_src, bias, sinks, *, batch, seq, gqa_pairs, chunk_tiles, q_tiles):
    n_pairs = A_WIDTH // LANES
    n_delta = bias.shape[1] - 1
    nq = seq // BLK
    n_chunks = 1 if n_delta + q_tiles - 1 <= chunk_tiles else nq // chunk_tiles
    assert nq % q_tiles == 0 and (n_chunks == 1 or nq % chunk_tiles == 0) and chunk_tiles <= nq
    nq //= q_tiles
    q_rows = q_tiles * BLK
    (q_arr, q_idx), (k_arr, k_idx), (v_arr, v_idx) = q_src, k_src, v_src
    pairs = PROMPT_PAIRS_PER_STEP
    assert n_pairs % pairs == 0
    kv_lanes = LANES if gqa_pairs else pairs * LANES
    kv_map = (lambda which: (lambda b, hp, qb: (which, b, 0))) if gqa_pairs else \
        (lambda which: (lambda b, hp, qb: (which, b, hp)))
    in_specs = [
        pl.BlockSpec((None, q_rows, pairs * LANES), lambda b, hp, qb: (q_idx, b * nq + qb, hp)),
        pl.BlockSpec((None, seq, kv_lanes), kv_map(k_idx)),
        pl.BlockSpec((None, seq, kv_lanes), kv_map(v_idx)),
        pl.BlockSpec((2 * pairs, n_delta + 1, BLK, BLK), lambda b, hp, qb: (hp, 0, 0, 0)),
    ]
    args = [q_arr, k_arr, v_arr, bias]
    if sinks is not None:
        in_specs.append(pl.BlockSpec(memory_space=pltpu.SMEM))
        args.append(sinks)
    return pl.pallas_call(
        functools.partial(_prompt_attn_kernel, n_delta=n_delta, chunk_tiles=chunk_tiles, n_chunks=n_chunks,
                          q_tiles=q_tiles, pairs=pairs, gqa_pairs=gqa_pairs, has_sink=sinks is not None),
        grid=(batch, n_pairs // pairs, nq),
        in_specs=in_specs,
        out_specs=pl.BlockSpec((q_rows, pairs * LANES), lambda b, hp, qb: (b * nq + qb, hp)),
        out_shape=jax.ShapeDtypeStruct((batch * seq, A_WIDTH), F32),
        scratch_shapes=[pltpu.VMEM((pairs, 2 * q_rows, 1), F32), pltpu.VMEM((pairs, 2 * q_rows, 1), F32),
                        pltpu.VMEM((pairs, 2 * q_rows, LANES), F32),
                        pltpu.VMEM((pairs, LANES, seq) if n_chunks > 1 else (pairs, seq, LANES), BF16),
                        pltpu.VMEM((pairs, seq, LANES), BF16)],
        compiler_params=_params("parallel", "parallel", "arbitrary"),
        name="prompt_attn_b" if gqa_pairs else "prompt_attn_a",
    )(*args)


def _prompt_bias_tiles(rel_bias, head_lo, mult_fn, n_delta):
    length = (n_delta + 1) * BLK + BLK - 1
    dist = np.arange(length) - (BLK - 1)
    mult = mult_fn(dist)
    assert not mult[dist > (n_delta - 1) * BLK].any(), "the reach must end before the masked tile"
    by_dist = _distance_bias(rel_bias, dist, head_lo, 16, mult)
    seg_len = 2 * BLK - 1
    segs = jnp.stack([by_dist[:, t * BLK:t * BLK + seg_len] for t in range(n_delta + 1)], axis=1)
    x = jnp.concatenate([segs[..., BLK - 1::-1], segs[..., :BLK - 1:-1]], axis=-1)
    flat = jnp.broadcast_to(x[:, :, None, :], (16, n_delta + 1, BLK, seg_len)).reshape(16, n_delta + 1, -1)
    return flat[..., :BLK * (seg_len - 1)].reshape(16, n_delta + 1, BLK, seg_len - 1)[..., :BLK]


def _block_diag_queries(q, n_heads):
    t, width = q.shape
    rows = n_heads * t
    tiled = jnp.broadcast_to(q[None], (n_heads, t, width)).reshape(rows, width)
    row_head = lax.broadcasted_iota(I32, (rows, width), 0) // t
    lane_head = lax.broadcasted_iota(I32, (rows, width), 1) // (width // n_heads)
    return jnp.where(row_head == lane_head, tiled * ATTN_SCALE, 0.0).astype(BF16)


def _block_diag_extract(o, n_heads):
    rows, width = o.shape
    t = rows // n_heads
    row_head = lax.broadcasted_iota(I32, (rows, width), 0) // t
    lane_head = lax.broadcasted_iota(I32, (rows, width), 1) // (width // n_heads)
    return jnp.sum(jnp.where(row_head == lane_head, o, 0.0).reshape(n_heads, t, width), axis=0)


def _softmax_step(s, v_t, m_ref, l_ref, acc_ref):
    m_old = m_ref[...]
    m_new = jnp.maximum(m_old, jnp.max(s, axis=-1, keepdims=True))
    alpha = jnp.exp(m_old - m_new)
    p = jnp.exp(s - m_new)
    l_ref[...] = alpha * l_ref[...] + jnp.sum(p, axis=-1, keepdims=True)
    pv = lax.dot_general(p.astype(BF16), v_t, (((1,), (1,)), ((), ())), preferred_element_type=F32)
    acc_ref[...] = alpha * acc_ref[...] + pv
    m_ref[...] = m_new


def _sample_a_kernel(q_ref, knt_ref, vnt_ref, kc_ref, vc_ref, knext_ref, vnext_ref, bias_ref, biasn_ref,
                     o_ref, ko_ref, vo_ref, qbd_ref, m_ref, l_ref, acc_ref, *, t_new):
    c = pl.program_id(1)
    last = pl.num_programs(1) - 1
    chunk = kc_ref.shape[1]

    @pl.when(c == 0)
    def _():
        qbd_ref[...] = _block_diag_queries(q_ref[...], A_HEADS)
        m_ref[...] = jnp.full(m_ref.shape, MASKED, F32)
        l_ref[...] = jnp.zeros(l_ref.shape, F32)
        acc_ref[...] = jnp.zeros(acc_ref.shape, F32)

    kc = kc_ref[...]
    vc = vc_ref[...]
    s = jnp.dot(qbd_ref[...], kc.astype(BF16), preferred_element_type=F32)
    _softmax_step(s + bias_ref[...], vc.astype(BF16), m_ref, l_ref, acc_ref)

    at_end = c == last
    k_after = jnp.where(at_end, knt_ref[...], knext_ref[...])
    v_after = jnp.where(at_end, vnt_ref[...], vnext_ref[...])
    ko_ref[...] = jnp.concatenate([kc, k_after], axis=1)[:, t_new:t_new + chunk]
    vo_ref[...] = jnp.concatenate([vc, v_after], axis=1)[:, t_new:t_new + chunk]

    @pl.when(at_end)
    def _():
        s_new = jnp.dot(qbd_ref[...], knt_ref[...].astype(BF16), preferred_element_type=F32)
        _softmax_step(s_new + biasn_ref[...], vnt_ref[...].astype(BF16), m_ref, l_ref, acc_ref)
        o_ref[...] = _block_diag_extract(acc_ref[...] / l_ref[...], A_HEADS)


def _sample_attention_a(q, k_new_t, v_new_t, cache_k_t, cache_v_t, bias, bias_new, *, chunk, t_new):
    nb, width, win = cache_k_t.shape
    nc = win // chunk
    rows = A_HEADS * t_new
    tiles_per_chunk = chunk // LANES
    q_spec = pl.BlockSpec((t_new, width), lambda b, c: (b, 0))
    new_spec = pl.BlockSpec((None, width, LANES), lambda b, c: (b, 0, 0))
    cache_spec = pl.BlockSpec((None, width, chunk), lambda b, c: (b, 0, c))
    next_spec = pl.BlockSpec((None, width, LANES),
                             lambda b, c: (b, 0, jnp.minimum(c + 1, nc - 1) * tiles_per_chunk))
    return pl.pallas_call(
        functools.partial(_sample_a_kernel, t_new=t_new),
        grid=(nb, nc),
        in_specs=[q_spec, new_spec, new_spec, cache_spec, cache_spec, next_spec, next_spec,
                  pl.BlockSpec((None, rows, chunk), lambda b, c: (c, 0, 0)),
                  pl.BlockSpec((rows, LANES), lambda b, c: (0, 0))],
        out_specs=[q_spec, cache_spec, cache_spec],
        out_shape=[jax.ShapeDtypeStruct((nb * t_new, width), F32),
                   jax.ShapeDtypeStruct(cache_k_t.shape, F32), jax.ShapeDtypeStruct(cache_v_t.shape, F32)],
        scratch_shapes=[pltpu.VMEM((rows, width), BF16), pltpu.VMEM((rows, 1), F32), pltpu.VMEM((rows, 1), F32),
                        pltpu.VMEM((rows, width), F32)],
        compiler_params=_params("parallel", "arbitrary"),
        name="sample_attn_a",
    )(q, k_new_t, v_new_t, cache_k_t, cache_v_t, cache_k_t, cache_v_t, bias, bias_new)


def _expand_kv(x):
    first = _pair_select(x, False)
    second = _pair_select(x, True)
    reps = B_HEADS // B_KV_HEADS // 2
    return jnp.concatenate([first] * reps + [second] * reps, axis=1)


def _sample_b_kernel(q_ref, kn_ref, vn_ref, kc_ref, vc_ref, bias_ref, sink_ref, o_ref, ko_ref, vo_ref, seq_ref):
    t_new = kn_ref.shape[0]
    win = kc_ref.shape[0]
    seq_ref[...] = jnp.zeros(seq_ref.shape, F32)
    for idx, (c_ref, n_ref, out_ref) in enumerate(((kc_ref, kn_ref, ko_ref), (vc_ref, vn_ref, vo_ref))):
        seq_ref[idx, 0:win, :] = c_ref[...]
        seq_ref[idx, win:win + t_new, :] = n_ref[...]
        out_ref[...] = seq_ref[idx, t_new:win + t_new, :]
    qbd = _block_diag_queries(q_ref[...], B_HEADS)
    k = _expand_kv(seq_ref[0]).astype(BF16)
    v = _expand_kv(seq_ref[1]).astype(BF16)
    s = lax.dot_general(qbd, k, (((1,), (1,)), ((), ())), preferred_element_type=F32) + bias_ref[...]
    sink = sink_ref[...]
    m = jnp.maximum(jnp.max(s, axis=-1, keepdims=True), sink)
    p = jnp.exp(s - m)
    den = jnp.sum(p, axis=-1, keepdims=True) + jnp.exp(sink - m)
    o = jnp.dot(p.astype(BF16), v, preferred_element_type=F32) / den
    o_ref[...] = _block_diag_extract(o, B_HEADS)


def _sample_attention_b(q, k_new, v_new, cache_k, cache_v, bias, sink_rows):
    nb, win, kvw = cache_k.shape
    t = q.shape[0] // nb
    rows = B_HEADS * t
    q_spec = pl.BlockSpec((t, B_WIDTH), lambda b: (b, 0))
    new_spec = pl.BlockSpec((t, kvw), lambda b: (b, 0))
    cache_spec = pl.BlockSpec((None, win, kvw), lambda b: (b, 0, 0))
    return pl.pallas_call(
        _sample_b_kernel,
        grid=(nb,),
        in_specs=[q_spec, new_spec, new_spec, cache_spec, cache_spec,
                  pl.BlockSpec((rows, 2 * win), lambda b: (0, 0)), pl.BlockSpec((rows, 1), lambda b: (0, 0))],
        out_specs=[q_spec, cache_spec, cache_spec],
        out_shape=[jax.ShapeDtypeStruct(q.shape, F32),
                   jax.ShapeDtypeStruct(cache_k.shape, F32), jax.ShapeDtypeStruct(cache_v.shape, F32)],
        scratch_shapes=[pltpu.VMEM((2, 2 * win, kvw), F32)],
        compiler_params=_params("parallel"),
        name="sample_attn_b",
    )(q, k_new, v_new, cache_k, cache_v, bias, sink_rows)


def _cross_attn_kernel(q_ref, k_ref, v_ref, o_ref):
    scale = MEM_HEAD_DIM ** -0.5
    for h in range(MEM_HEADS):
        cols = slice(h * MEM_HEAD_DIM, (h + 1) * MEM_HEAD_DIM)
        q = q_ref[:, cols].astype(BF16)
        k = k_ref[:, cols].astype(BF16)
        v = v_ref[:, cols].astype(BF16)
        s = lax.dot_general(q, k, (((1,), (1,)), ((), ())), preferred_element_type=F32) * scale
        m = jnp.max(s, axis=-1, keepdims=True)
        e = jnp.exp(s - m)
        p = e / jnp.sum(e, axis=-1, keepdims=True)
        o_ref[:, cols] = jnp.dot(p.astype(BF16), v, preferred_element_type=F32)


def _cross_attention(q, mk, mv, *, tq):
    nb, s, width = q.shape
    mem = mk.shape[1]
    q_spec = pl.BlockSpec((None, tq, width), lambda b, i: (b, i, 0))
    m_spec = pl.BlockSpec((None, mem, width), lambda b, i: (b, 0, 0))
    return pl.pallas_call(
        _cross_attn_kernel,
        grid=(nb, s // tq),
        in_specs=[q_spec, m_spec, m_spec],
        out_specs=q_spec,
        out_shape=jax.ShapeDtypeStruct(q.shape, F32),
        compiler_params=_params("parallel", "arbitrary"),
        name="cross_attn",
    )(q, mk, mv)


def _top_rows(s, ids, k):
    n = s.shape[0]
    row = lax.broadcasted_iota(I32, s.shape, 0).astype(F32)
    vals, picked = [], []
    for _ in range(k):
        m = jnp.max(s, axis=0, keepdims=True)
        pos = jnp.min(jnp.where(s == m, row, float(n)), axis=0, keepdims=True)
        hit = row == pos
        vals.append(m)
        picked.append(pos if ids is None else jnp.max(jnp.where(hit, ids, -1.0), axis=0, keepdims=True))
        s = jnp.where(hit, -jnp.inf, s)
    return jnp.concatenate(vals, axis=0), jnp.concatenate(picked, axis=0)


def _peer_topk_kernel(q_ref, k1_ref, k2_ref, eidx_ref, gate_ref, *, heads_per_step):
    for h in range(heads_per_step):
        eidx, gate = _peer_topk_head(q_ref[:, h * PEER_DKEY:(h + 1) * PEER_DKEY], k1_ref[...], k2_ref[...])
        gate_ref[h * PEER_TOPK:(h + 1) * PEER_TOPK, :] = gate
        eidx_ref[h * PEER_TOPK:(h + 1) * PEER_TOPK, :] = eidx


def _peer_topk_head(q, k1, k2):
    half = PEER_DKEY // 2
    tb = q.shape[0]
    nt = (((1,), (1,)), ((), ()))
    s1 = lax.dot_general(k1, q[:, :half].astype(BF16), nt, preferred_element_type=F32)
    s2 = lax.dot_general(k2, q[:, half:].astype(BF16), nt, preferred_element_type=F32)
    v1, i1 = _top_rows(s1, None, PEER_TOPK)
    v2, i2 = _top_rows(s2, None, PEER_TOPK)
    counts = [PEER_TOPK // (a + 1) for a in range(PEER_TOPK)]
    pad = -sum(counts) % SUBLANES

    def per_a(x, fill):
        rows = [jnp.broadcast_to(x[a:a + 1], (counts[a], tb)) for a in range(PEER_TOPK)]
        return jnp.concatenate(rows + [jnp.full((pad, tb), fill, F32)], axis=0)

    def per_b(x):
        return jnp.concatenate([x[0:counts[a]] for a in range(PEER_TOPK)] + [jnp.zeros((pad, tb), F32)], axis=0)

    cand = per_a(v1, -jnp.inf) + per_b(v2)
    cidx = per_a(i1, 0.0) * PEER_NKEYS + per_b(i2)
    best, eidx = _top_rows(cand, cidx, PEER_TOPK)
    e = jnp.exp(best - best[0:1])
    return eidx.astype(I32), e / jnp.sum(e, axis=0, keepdims=True)


def _peer_topk(q, sub_k1, sub_k2, *, tb):
    m = q.shape[0]
    rows = PEER_HEADS * PEER_TOPK
    hps = PEER_TOPK_HEADS_PER_STEP
    key_spec = pl.BlockSpec((PEER_NKEYS, PEER_DKEY // 2), lambda i, h: (0, 0))
    out_spec = pl.BlockSpec((hps * PEER_TOPK, tb), lambda i, h: (h, i))
    return pl.pallas_call(
        functools.partial(_peer_topk_kernel, heads_per_step=hps),
        grid=(m // tb, PEER_HEADS // hps),
        in_specs=[pl.BlockSpec((tb, hps * PEER_DKEY), lambda i, h: (i, h)), key_spec, key_spec],
        out_specs=[out_spec, out_spec],
        out_shape=[jax.ShapeDtypeStruct((rows, m), I32), jax.ShapeDtypeStruct((rows, m), F32)],
        compiler_params=_params("parallel", "arbitrary"),
        name="peer_topk",
    )(q, sub_k1, sub_k2)


def _tree_sum(terms):
    while len(terms) > 1:
        terms = [terms[j] + terms[j + 1] for j in range(0, len(terms) - 1, 2)] + \
            ([terms[-1]] if len(terms) % 2 else [])
    return terms[0]


def _pack_expert_tables(expert_u, expert_v):
    ub = lax.bitcast_convert_type(expert_u.astype(BF16), jnp.uint16).astype(jnp.uint32)
    vb = lax.bitcast_convert_type(expert_v.astype(BF16), jnp.uint16).astype(jnp.uint32)
    n_exp, d = expert_u.shape
    return ((ub << 16) | vb).reshape(n_exp, d // LANES, LANES)


def _peer_expert_kernel(idx_ref, idxn_ref, gate_ref, y_ref, lnx_ref, lnf_ref, tab_hbm, o_ref, *scratch, tg, n_sel, n_slots):
    bufs, (rows_ref, cols_ref, sem) = scratch[:n_slots], scratch[n_slots:]
    i = pl.program_id(0)
    last = pl.num_programs(0) - 1
    n_tiles = bufs[0].shape[1]
    high = jnp.uint32(0xFFFF0000)
    ahead = PEER_GROUPS_AHEAD

    def slab_copy(ids, row, s, t, k):
        return pltpu.make_async_copy(tab_hbm.at[ids[row, k]], bufs[s].at[t, :, k, :], sem.at[s])

    def wait_slot(s):
        pltpu.make_async_copy(bufs[s], bufs[s], sem.at[s]).wait()

    @pl.when(i == 0)
    def _():
        for g in range(ahead):
            def prime(t, carry, g=g):
                for k in range(n_sel):
                    slab_copy(idx_ref, g * tg + t, g, t, k).start()
                return carry
            lax.fori_loop(0, tg, prime, 0)

    def group(s):
        nxt = (s + ahead) % n_slots
        ids_next = idx_ref if s + ahead < n_slots else idxn_ref
        next_row0 = nxt * tg
        rows = slice(s * tg, (s + 1) * tg)
        wait_slot(s)
        y = y_ref[rows, :]
        x = y * lax.rsqrt(jnp.mean(y * y, axis=-1, keepdims=True) + EPS) * lnx_ref[...]
        for t in range(tg):
            rows_ref[0, t] = x[t:t + 1, :]
            rows_ref[1, t] = y[t:t + 1, :]
        token = lax.broadcasted_iota(I32, (n_sel, tg), 1)
        half = n_sel // 2

        per_tile = half // n_tiles

        def request(t, lo, n):
            for k in range(lo, lo + n):
                slab_copy(ids_next, next_row0 + t, nxt, t, k).start(priority=k % 2)

        def dot_pass(t, carry):
            part = None
            for c in range(n_tiles):
                request(t, c * per_tile, per_tile)
                term = lax.bitcast_convert_type(bufs[s][t, c] & high, F32) \
                    * rows_ref[0, t, :, c * LANES:(c + 1) * LANES]
                part = term if part is None else part + term
            cols_ref[t] = jnp.broadcast_to(jnp.sum(part, axis=-1, keepdims=True), (n_sel, LANES))
            return carry

        lax.fori_loop(0, tg, dot_pass, 0)
        pre = jnp.zeros((n_sel, tg), F32)
        for t in range(tg):
            pre = jnp.where(token == t, cols_ref[t][:, 0:tg], pre)
        act = 0.5 * pre * (1.0 + lax.erf(pre * (2.0 ** -0.5)))
        w = gate_ref[s] * act
        for t in range(tg):
            cols_ref[t] = jnp.broadcast_to(w[:, t:t + 1], (n_sel, LANES))

        def mix_pass(t, carry):
            wt = cols_ref[t]
            sums = []
            for c in range(n_tiles):
                request(t, half + c * per_tile, per_tile)
                prod = lax.bitcast_convert_type(bufs[s][t, c] << 16, F32) * wt
                groups = [prod[g * SUBLANES:(g + 1) * SUBLANES, :] for g in range(n_sel // SUBLANES)]
                sums.append(jnp.sum(_tree_sum(groups), axis=0, keepdims=True))
            out = rows_ref[1, t] + jnp.concatenate(sums, axis=1)
            ms = jnp.mean(out * out, axis=-1, keepdims=True)
            rows_ref[2, t] = out * lax.rsqrt(ms + EPS) * lnf_ref[...]
            return carry

        lax.fori_loop(0, tg, mix_pass, 0)
        for t in range(tg):
            o_ref[s * tg + t:s * tg + t + 1, :] = rows_ref[2, t]

    for s in range(n_slots):
        group(s)

    @pl.when(i == last)
    def _():
        for g in range(ahead):
            wait_slot(g)


def _peer_experts(eidx, gate_cols, y, ln_ffn, ln_final, table, *, tg):
    m, d = y.shape
    n_sel = eidx.shape[1]
    n_tiles = table.shape[1]
    n_slots = PEER_SLOTS
    per_step = n_slots * tg
    assert m % per_step == 0 and n_tiles * LANES == d
    n_steps = m // per_step
    row_spec = pl.BlockSpec((per_step, d), lambda i: (i, 0))
    return pl.pallas_call(
        functools.partial(_peer_expert_kernel, tg=tg, n_sel=n_sel, n_slots=n_slots),
        grid=(n_steps,),
        in_specs=[
            pl.BlockSpec((per_step, n_sel), lambda i: (i, 0), memory_space=pltpu.SMEM),
            pl.BlockSpec((per_step, n_sel), lambda i: (jnp.minimum(i + 1, n_steps - 1), 0), memory_space=pltpu.SMEM),
            pl.BlockSpec((n_slots, n_sel, tg), lambda i: (i, 0, 0)),
            row_spec,
            pl.BlockSpec((1, d), lambda i: (0, 0)),
            pl.BlockSpec((1, d), lambda i: (0, 0)),
            pl.BlockSpec(memory_space=pl.ANY),
        ],
        out_specs=row_spec,
        out_shape=jax.ShapeDtypeStruct((m, d), F32),
        scratch_shapes=[pltpu.VMEM((tg, n_tiles, n_sel, LANES), jnp.uint32)] * n_slots + [
            pltpu.VMEM((3, tg, 1, d), F32), pltpu.VMEM((tg, n_sel, LANES), F32),
            pltpu.SemaphoreType.DMA((n_slots,))],
        compiler_params=_params("arbitrary"),
        name="peer_experts",
    )(eidx, eidx, gate_cols, y, ln_ffn.reshape(1, d), ln_final.reshape(1, d), table)


def _channel_mixers(y, mk, mv, nb, p, *, tm, tq, tb, tg):
    m, d = y.shape
    q = _norm_matmul([y], [p["ln_cross"]], p["w_cq"], tm=tm, tn=MEM_WIDTH, name="cross_q_proj")
    o = _cross_attention(q.reshape(nb, m // nb, MEM_WIDTH), mk, mv, tq=tq).reshape(m, MEM_WIDTH)
    y = _norm_matmul([o], None, p["w_co"], residual=y, tm=tm, tn=512, name="cross_out_proj")
    pq = _norm_matmul([y], [p["ln_ffn"]], p["w_pq"], tm=tm, tn=512, name="peer_query_proj")
    eidx_t, gate_t = _peer_topk(pq, p["sub_k1"], p["sub_k2"], tb=tb)
    n_sel = eidx_t.shape[0]
    gate_cols = gate_t.reshape(n_sel, m // tg, tg).transpose(1, 0, 2)
    return _peer_experts(eidx_t.T, gate_cols, y, p["ln_ffn"], p["ln_final"], p["expert_table"], tg=tg)


def kernel(x_prompt, x_sample, cache_a_k, cache_a_v, cache_b_k, cache_b_v, cache_mem_k, cache_mem_v, mem_prompt, ln_mix, w_in, ln_a_out, ln_b_out, w_out, b_sinks, rel_bias, ln_cross, ln_mem, w_cq, w_ckv, w_co, ln_ffn, w_pq, sub_keys_1, sub_keys_2, expert_u, expert_v, ln_final):
    depth = w_in.shape[0]
    assert depth == 1, "the caches are laid out for a single layer"
    batch, seq, d = x_prompt.shape
    dec_batch, dec_seq, _ = x_sample.shape
    a_win = cache_a_k.shape[2]
    b_win = cache_b_k.shape[2]
    mem_len = mem_prompt.shape[1]
    assert seq == a_win == A_WIN and b_win == B_WIN and seq % BLK == 0
    l = 0
    split = 3 * A_WIDTH + B_WIDTH
    w_in_main = w_in[l, :, :split].astype(BF16)
    w_in_kvb = w_in[l, :, split:].astype(BF16)
    p = dict(ln_cross=ln_cross[l], w_cq=w_cq[l].astype(BF16), w_co=w_co[l].astype(BF16), ln_ffn=ln_ffn[l],
             w_pq=w_pq[l].astype(BF16), sub_k1=sub_keys_1[l].astype(BF16), sub_k2=sub_keys_2[l].astype(BF16),
             ln_final=ln_final, expert_table=_pack_expert_tables(expert_u[l], expert_v[l]))
    w_out_bf = w_out[l].astype(BF16)
    w_ckv_bf = w_ckv[l].astype(BF16)
    sinks = b_sinks[l].astype(F32)

    def project(x2d, tm):
        main = _norm_matmul([x2d], [ln_mix[l]], w_in_main, tm=tm, tn=512, out_split=4, name="in_proj")
        kvb = _norm_matmul([x2d], [ln_mix[l]], w_in_kvb, tm=tm, tn=B_KV_WIDTH, out_split=2, name="in_proj_kvb")
        return main, kvb

    def merge(oa, ob, resid, tm):
        return _norm_matmul([oa, ob], [ln_a_out[l], ln_b_out[l]], w_out_bf, residual=resid, tm=tm, tn=512,
                            name="mixer_out_proj")

    xp = x_prompt.reshape(batch * seq, d)
    main, kvb = project(xp, PROMPT_ROW_TILE)
    n_delta_a = seq // BLK + 1
    bias_a = _prompt_bias_tiles(rel_bias, 0, _mixer_a_multiplicity, n_delta_a)
    window_b = lambda dist: ((dist >= 0) & (dist <= B_WIN)).astype(np.int32)
    bias_b = _prompt_bias_tiles(rel_bias, A_HEADS, window_b, B_WIN // BLK + 1)
    oa = _prompt_attention((main, 0), (main, 1), (main, 2), bias_a, None, batch=batch, seq=seq, gqa_pairs=0,
                           chunk_tiles=8, q_tiles=PROMPT_Q_TILES)
    ob = _prompt_attention((main, 3), (kvb, 0), (kvb, 1), bias_b, sinks, batch=batch, seq=seq,
                           gqa_pairs=B_HEADS // B_KV_HEADS // 2, chunk_tiles=B_WIN // BLK + PROMPT_Q_TILES,
                           q_tiles=PROMPT_Q_TILES)
    yp = merge(oa, ob, xp, PROMPT_ROW_TILE)
    mem_kv = _norm_matmul([mem_prompt.reshape(batch * mem_len, d)], [ln_mem[l]], w_ckv_bf, tm=512, tn=MEM_WIDTH,
                          out_split=2, name="mem_kv_proj")
    mk = mem_kv[0].reshape(batch, mem_len, MEM_WIDTH)
    mv = mem_kv[1].reshape(batch, mem_len, MEM_WIDTH)
    y_prompt = _channel_mixers(yp, mk, mv, batch, p, tm=PROMPT_ROW_TILE, tq=512, tb=128, tg=8)

    xs = x_sample.reshape(dec_batch * dec_seq, d)
    main_s, kvb_s = project(xs, 512)
    chunk = 512
    key_pos = np.arange(a_win)
    t_pos = np.arange(dec_seq)
    dist_cache = a_win + t_pos[:, None] - key_pos[None, :]
    bias_sa = _distance_bias(rel_bias, dist_cache, 0, A_HEADS, _mixer_a_multiplicity(dist_cache))
    bias_sa = bias_sa.reshape(A_HEADS * dec_seq, a_win // chunk, chunk).transpose(1, 0, 2)
    dist_new = t_pos[:, None] - np.arange(LANES)[None, :]
    mult_new = np.where(np.arange(LANES)[None, :] < dec_seq, _mixer_a_multiplicity(dist_new), 0)
    bias_sa_new = _distance_bias(rel_bias, dist_new, 0, A_HEADS, mult_new).reshape(A_HEADS * dec_seq, LANES)
    def positions_minor(x, n_pos):
        return x.reshape(dec_batch, n_pos, A_WIDTH).transpose(0, 2, 1)

    def new_tokens_tile(x):
        return jnp.pad(positions_minor(x, dec_seq), ((0, 0), (0, 0), (0, LANES - dec_seq)))

    oa_s, aks_t, avs_t = _sample_attention_a(
        main_s[0], new_tokens_tile(main_s[1]), new_tokens_tile(main_s[2]),
        positions_minor(cache_a_k[l], a_win), positions_minor(cache_a_v[l], a_win),
        bias_sa, bias_sa_new, chunk=chunk, t_new=dec_seq)
    aks = aks_t.transpose(0, 2, 1)
    avs = avs_t.transpose(0, 2, 1)
    seq_pos = np.arange(2 * b_win)
    dist_b = b_win + t_pos[:, None] - seq_pos[None, :]
    mult_b = ((dist_b >= 0) & (dist_b <= B_WIN) & (seq_pos[None, :] < b_win + dec_seq)).astype(np.int32)
    bias_sb = _distance_bias(rel_bias, dist_b, A_HEADS, B_HEADS, mult_b).reshape(B_HEADS * dec_seq, 2 * b_win)
    sink_rows = jnp.repeat(sinks, dec_seq).reshape(B_HEADS * dec_seq, 1)
    ob_s, bks, bvs = _sample_attention_b(
        main_s[3], kvb_s[0], kvb_s[1],
        cache_b_k[l].reshape(dec_batch, b_win, B_KV_WIDTH), cache_b_v[l].reshape(dec_batch, b_win, B_KV_WIDTH),
        bias_sb, sink_rows)
    ys = merge(oa_s, ob_s, xs, 512)
    mk_s = cache_mem_k[l].reshape(dec_batch, mem_len, MEM_WIDTH)
    mv_s = cache_mem_v[l].reshape(dec_batch, mem_len, MEM_WIDTH)
    y_sample = _channel_mixers(ys, mk_s, mv_s, dec_batch, p, tm=512, tq=dec_seq, tb=128, tg=8)

    def heads(x, *shape):
        return x.reshape(1, *shape)

    return (y_prompt.reshape(batch, seq, d), y_sample.reshape(dec_batch, dec_seq, d),
            heads(main[1], batch, seq, A_HEADS, HEAD_DIM), heads(main[2], batch, seq, A_HEADS, HEAD_DIM),
            heads(kvb[0].reshape(batch, seq, B_KV_WIDTH)[:, seq - b_win:], batch, b_win, B_KV_HEADS, HEAD_DIM),
            heads(kvb[1].reshape(batch, seq, B_KV_WIDTH)[:, seq - b_win:], batch, b_win, B_KV_HEADS, HEAD_DIM),
            heads(mk, batch, mem_len, MEM_HEADS, MEM_HEAD_DIM), heads(mv, batch, mem_len, MEM_HEADS, MEM_HEAD_DIM),
            heads(aks, dec_batch, a_win, A_HEADS, HEAD_DIM), heads(avs, dec_batch, a_win, A_HEADS, HEAD_DIM),
            heads(bks, dec_batch, b_win, B_KV_HEADS, HEAD_DIM), heads(bvs, dec_batch, b_win, B_KV_HEADS, HEAD_DIM))
```

```python
import functools
import math

import numpy as np
import jax
import jax.numpy as jnp
from jax import lax
from jax.experimental import pallas as pl
from jax.experimental.pallas import tpu as pltpu

F32 = jnp.float32
BF16 = jnp.bfloat16
I32 = jnp.int32

EPS = 1e-6
MASKED = -1e30

LANES = 128
SUBLANES = 8
VMEM_LIMIT = 48 * 1024 * 1024

HEAD_DIM = 64
A_HEADS = 16
A_PATTERNS = ((128, 1), (512, 4), (2048, 16))
A_WIN = 2048
B_HEADS = 16
B_KV_HEADS = 2
B_WIN = 128
A_WIDTH = A_HEADS * HEAD_DIM
B_WIDTH = B_HEADS * HEAD_DIM
B_KV_WIDTH = B_KV_HEADS * HEAD_DIM
ATTN_SCALE = HEAD_DIM ** -0.5
N_BUCKETS = 32
MAX_EXACT = N_BUCKETS // 2
MAX_DISTANCE = A_WIN
MEM_HEADS = 4
MEM_HEAD_DIM = 128
MEM_WIDTH = MEM_HEADS * MEM_HEAD_DIM
PEER_HEADS = 8
PEER_NKEYS = 128
PEER_DKEY = 256
PEER_TOPK = 16
BLK = 128
PROMPT_ROW_TILE = 1024
PROMPT_Q_TILES = 1
PROMPT_PAIRS_PER_STEP = 2
PEER_TOPK_HEADS_PER_STEP = 8
PEER_SLOTS = 4
PEER_GROUPS_AHEAD = 2


def _params(*semantics, flags=None):
    return pltpu.CompilerParams(dimension_semantics=semantics, vmem_limit_bytes=VMEM_LIMIT, flags=flags)


def _norm_matmul_kernel(*refs, n_groups, norm, residual):
    xs = refs[:n_groups]
    pos = n_groups
    gs = refs[pos:pos + n_groups] if norm else ()
    pos += n_groups if norm else 0
    w_ref = refs[pos]
    pos += 1
    r_ref = refs[pos] if residual else None
    pos += 1 if residual else 0
    o_ref, xn_ref = refs[pos], refs[pos + 1]

    @pl.when(pl.program_id(1) == 0)
    def _():
        off = 0
        for gi in range(n_groups):
            x = xs[gi][...]
            if norm:
                ms = jnp.mean(x * x, axis=-1, keepdims=True)
                x = x * lax.rsqrt(ms + EPS) * gs[gi][...]
            width = x.shape[-1]
            xn_ref[:, off:off + width] = x.astype(BF16)
            off += width

    acc = jnp.dot(xn_ref[...], w_ref[...], preferred_element_type=F32)
    if residual:
        acc = acc + r_ref[...]
    o_ref[...] = acc


def _norm_matmul(xs, gains, w, residual=None, *, tm, tn, out_split=1, name="norm_matmul"):
    m = xs[0].shape[0]
    k_total, n = w.shape
    assert sum(x.shape[1] for x in xs) == k_total and m % tm == 0 and n % (tn * out_split) == 0
    norm = gains is not None
    nj_per = n // out_split // tn
    in_specs = [pl.BlockSpec((tm, x.shape[1]), lambda i, j: (i, 0)) for x in xs]
    args = list(xs)
    if norm:
        in_specs += [pl.BlockSpec((1, g.shape[-1]), lambda i, j: (0, 0)) for g in gains]
        args += [g.reshape(1, -1) for g in gains]
    in_specs.append(pl.BlockSpec((k_total, tn), lambda i, j: (0, j)))
    args.append(w)
    if residual is not None:
        in_specs.append(pl.BlockSpec((tm, tn), lambda i, j: (i, j)))
        args.append(residual)
    if out_split == 1:
        out_shape = jax.ShapeDtypeStruct((m, n), F32)
        out_spec = pl.BlockSpec((tm, tn), lambda i, j: (i, j))
    else:
        out_shape = jax.ShapeDtypeStruct((out_split, m, n // out_split), F32)
        out_spec = pl.BlockSpec((None, tm, tn), lambda i, j: (j // nj_per, i, j % nj_per))
    return pl.pallas_call(
        functools.partial(_norm_matmul_kernel, n_groups=len(xs), norm=norm, residual=residual is not None),
        grid=(m // tm, n // tn),
        in_specs=in_specs,
        out_specs=out_spec,
        out_shape=out_shape,
        scratch_shapes=[pltpu.VMEM((tm, k_total), BF16)],
        compiler_params=_params("parallel", "arbitrary"),
        name=name,
    )(*args)


def _rel_bucket(dist):
    dist = np.maximum(np.asarray(dist), 0)
    ratio = np.log(np.maximum(dist, 1) / MAX_EXACT) / math.log(MAX_DISTANCE / MAX_EXACT)
    large = np.minimum(MAX_EXACT + (ratio * (N_BUCKETS - MAX_EXACT)).astype(np.int32), N_BUCKETS - 1)
    return np.where(dist < MAX_EXACT, dist, large).astype(np.int32)


def _mixer_a_multiplicity(dist):
    dist = np.asarray(dist)
    mult = np.zeros(dist.shape, np.int32)
    for window, dilation in A_PATTERNS:
        mult += ((dist >= 0) & (dist <= window) & (dist % dilation == 0)).astype(np.int32)
    return mult


def _distance_bias(rel_bias, dist, head_lo, n_heads, mult):
    table = rel_bias[:, head_lo:head_lo + n_heads].astype(F32).T
    vals = table[:, _rel_bucket(dist)]
    logm = np.log(np.maximum(mult, 1)).astype(np.float32)
    return jnp.where(jnp.asarray(mult > 0)[None], vals + jnp.asarray(logm)[None], MASKED)


def _pair_select(x, which):
    lane_head = lax.broadcasted_iota(I32, x.shape, 1) // HEAD_DIM
    swapped = pltpu.roll(x, HEAD_DIM, axis=1)
    return jnp.where(lane_head == which, x, swapped)


def _prompt_attn_kernel(*refs, n_delta, chunk_tiles, n_chunks, q_tiles, pairs, gqa_pairs, has_sink):
    if has_sink:
        q_ref, k_ref, v_ref, bias_ref, sink_ref, o_ref, m_ref, l_ref, acc_ref, kb_ref, vb_ref = refs
    else:
        q_ref, k_ref, v_ref, bias_ref, o_ref, m_ref, l_ref, acc_ref, kb_ref, vb_ref = refs
        sink_ref = None
    hpg = pl.program_id(1)
    qb = pl.program_id(2)
    keys_on_lanes = n_chunks > 1

    def lanes_of(j):
        return slice(j * LANES, (j + 1) * LANES)

    @pl.when(qb == 0)
    def _():
        for j in range(pairs):
            if gqa_pairs:
                kv_head = (hpg * pairs + j) // gqa_pairs
                k = _pair_select(k_ref[...], kv_head)
                v = _pair_select(v_ref[...], kv_head)
            else:
                k = k_ref[:, lanes_of(j)]
                v = v_ref[:, lanes_of(j)]
            kb_ref[j] = (k.T if keys_on_lanes else k).astype(BF16)
            vb_ref[j] = v.astype(BF16)

    lane = lax.broadcasted_iota(I32, (BLK, LANES), 1)
    left = lane < HEAD_DIM
    q2s = []
    for j in range(pairs):
        q = q_ref[:, lanes_of(j)] * ATTN_SCALE
        blocks = []
        for i in range(q_tiles):
            qi = q[i * BLK:(i + 1) * BLK]
            blocks += [jnp.where(left, qi, 0.0), jnp.where(left, 0.0, qi)]
        q2s.append(jnp.concatenate(blocks, axis=0).astype(BF16))
        if has_sink:
            m_ref[j] = jnp.concatenate([jnp.full((BLK, 1), sink_ref[2 * (hpg * pairs + j) + h2], F32)
                                        for _ in range(q_tiles) for h2 in range(2)], axis=0)
            l_ref[j] = jnp.ones(l_ref.shape[1:], F32)
        else:
            m_ref[j] = jnp.full(m_ref.shape[1:], MASKED, F32)
            l_ref[j] = jnp.zeros(l_ref.shape[1:], F32)
        acc_ref[j] = jnp.zeros(acc_ref.shape[1:], F32)

    def chunk(start_tile):
        start = pl.multiple_of(start_tile * BLK, BLK)
        for jp in range(pairs):
            v = vb_ref[jp, pl.ds(start, chunk_tiles * BLK), :]
            if keys_on_lanes:
                k = kb_ref[jp, :, start_tile * BLK:(start_tile + chunk_tiles) * BLK]
                s = jnp.dot(q2s[jp], k, preferred_element_type=F32)
            else:
                k = kb_ref[jp, pl.ds(start, chunk_tiles * BLK), :]
                s = lax.dot_general(q2s[jp], k, (((1,), (1,)), ((), ())), preferred_element_type=F32)
            rows = []
            for i in range(q_tiles):
                for h2 in range(2):
                    tiles = []
                    for j in range(chunk_tiles):
                        delta = qb * q_tiles + i - (start_tile + j)
                        tiles.append(bias_ref[2 * jp + h2,
                                              jnp.where((delta >= 0) & (delta < n_delta), delta, n_delta)])
                    rows.append(jnp.concatenate(tiles, axis=1))
            s = s + jnp.concatenate(rows, axis=0)
            m_old = m_ref[jp]
            m_new = jnp.maximum(m_old, jnp.max(s, axis=-1, keepdims=True))
            alpha = jnp.exp(m_old - m_new)
            p = jnp.exp(s - m_new)
            l_ref[jp] = alpha * l_ref[jp] + jnp.sum(p, axis=-1, keepdims=True)
            m_ref[jp] = m_new
            acc_ref[jp] = alpha * acc_ref[jp] + jnp.dot(p.astype(BF16), v, preferred_element_type=F32)

    last_q_tile = qb * q_tiles + (q_tiles - 1)
    if n_chunks == 1:
        chunk(jnp.maximum(last_q_tile - (chunk_tiles - 1), 0))
    else:
        for c in range(n_chunks):
            pl.when(c * chunk_tiles <= last_q_tile)(functools.partial(chunk, c * chunk_tiles))
    for jp in range(pairs):
        o = acc_ref[jp] / l_ref[jp]
        for i in range(q_tiles):
            o_ref[i * BLK:(i + 1) * BLK, lanes_of(jp)] = jnp.where(left, o[2 * i * BLK:(2 * i + 1) * BLK],
                                                                   o[(2 * i + 1) * BLK:(2 * i + 2) * BLK])


def _prompt_attention(q_src, k_src, v_src, bias, sinks, *, batch, seq, gqa_pairs, chunk_tiles, q_tiles):
    n_pairs = A_WIDTH // LANES
    n_delta = bias.shape[1] - 1
    nq = seq // BLK
    n_chunks = 1 if n_delta + q_tiles - 1 <= chunk_tiles else nq // chunk_tiles
    assert nq % q_tiles == 0 and (n_chunks == 1 or nq % chunk_tiles == 0) and chunk_tiles <= nq
    nq //= q_tiles
    q_rows = q_tiles * BLK
    (q_arr, q_idx), (k_arr, k_idx), (v_arr, v_idx) = q_src, k_src, v_src
    pairs = PROMPT_PAIRS_PER_STEP
    assert n_pairs % pairs == 0
    kv_lanes = LANES if gqa_pairs else pairs * LANES
    kv_map = (lambda which: (lambda b, hp, qb: (which, b, 0))) if gqa_pairs else \
        (lambda which: (lambda b, hp, qb: (which, b, hp)))
    in_specs = [
        pl.BlockSpec((None, q_rows, pairs * LANES), lambda b, hp, qb: (q_idx, b * nq + qb, hp)),
        pl.BlockSpec((None, seq, kv_lanes), kv_map(k_idx)),
        pl.BlockSpec((None, seq, kv_lanes), kv_map(v_idx)),
        pl.BlockSpec((2 * pairs, n_delta + 1, BLK, BLK), lambda b, hp, qb: (hp, 0, 0, 0)),
    ]
    args = [q_arr, k_arr, v_arr, bias]
    if sinks is not None:
        in_specs.append(pl.BlockSpec(memory_space=pltpu.SMEM))
        args.append(sinks)
    return pl.pallas_call(
        functools.partial(_prompt_attn_kernel, n_delta=n_delta, chunk_tiles=chunk_tiles, n_chunks=n_chunks,
                          q_tiles=q_tiles, pairs=pairs, gqa_pairs=gqa_pairs, has_sink=sinks is not None),
        grid=(batch, n_pairs // pairs, nq),
        in_specs=in_specs,
        out_specs=pl.BlockSpec((q_rows, pairs * LANES), lambda b, hp, qb: (b * nq + qb, hp)),
        out_shape=jax.ShapeDtypeStruct((batch * seq, A_WIDTH), F32),
        scratch_shapes=[pltpu.VMEM((pairs, 2 * q_rows, 1), F32), pltpu.VMEM((pairs, 2 * q_rows, 1), F32),
                        pltpu.VMEM((pairs, 2 * q_rows, LANES), F32),
                        pltpu.VMEM((pairs, LANES, seq) if n_chunks > 1 else (pairs, seq, LANES), BF16),
                        pltpu.VMEM((pairs, seq, LANES), BF16)],
        compiler_params=_params("parallel", "parallel", "arbitrary"),
        name="prompt_attn_b" if gqa_pairs else "prompt_attn_a",
    )(*args)


def _prompt_bias_tiles(rel_bias, head_lo, mult_fn, n_delta):
    length = (n_delta + 1) * BLK + BLK - 1
    dist = np.arange(length) - (BLK - 1)
    mult = mult_fn(dist)
    assert not mult[dist > (n_delta - 1) * BLK].any(), "the reach must end before the masked tile"
    by_dist = _distance_bias(rel_bias, dist, head_lo, 16, mult)
    seg_len = 2 * BLK - 1
    segs = jnp.stack([by_dist[:, t * BLK:t * BLK + seg_len] for t in range(n_delta + 1)], axis=1)
    x = jnp.concatenate([segs[..., BLK - 1::-1], segs[..., :BLK - 1:-1]], axis=-1)
    flat = jnp.broadcast_to(x[:, :, None, :], (16, n_delta + 1, BLK, seg_len)).reshape(16, n_delta + 1, -1)
    return flat[..., :BLK * (seg_len - 1)].reshape(16, n_delta + 1, BLK, seg_len - 1)[..., :BLK]


def _block_diag_queries(q, n_heads):
    t, width = q.shape
    rows = n_heads * t
    tiled = jnp.broadcast_to(q[None], (n_heads, t, width)).reshape(rows, width)
    row_head = lax.broadcasted_iota(I32, (rows, width), 0) // t
    lane_head = lax.broadcasted_iota(I32, (rows, width), 1) // (width // n_heads)
    return jnp.where(row_head == lane_head, tiled * ATTN_SCALE, 0.0).astype(BF16)


def _block_diag_extract(o, n_heads):
    rows, width = o.shape
    t = rows // n_heads
    row_head = lax.broadcasted_iota(I32, (rows, width), 0) // t
    lane_head = lax.broadcasted_iota(I32, (rows, width), 1) // (width // n_heads)
    return jnp.sum(jnp.where(row_head == lane_head, o, 0.0).reshape(n_heads, t, width), axis=0)


def _softmax_step(s, v_t, m_ref, l_ref, acc_ref):
    m_old = m_ref[...]
    m_new = jnp.maximum(m_old, jnp.max(s, axis=-1, keepdims=True))
    alpha = jnp.exp(m_old - m_new)
    p = jnp.exp(s - m_new)
    l_ref[...] = alpha * l_ref[...] + jnp.sum(p, axis=-1, keepdims=True)
    pv = lax.dot_general(p.astype(BF16), v_t, (((1,), (1,)), ((), ())), preferred_element_type=F32)
    acc_ref[...] = alpha * acc_ref[...] + pv
    m_ref[...] = m_new


def _sample_a_kernel(q_ref, knt_ref, vnt_ref, kc_ref, vc_ref, knext_ref, vnext_ref, bias_ref, biasn_ref,
                     o_ref, ko_ref, vo_ref, qbd_ref, m_ref, l_ref, acc_ref, *, t_new):
    c = pl.program_id(1)
    last = pl.num_programs(1) - 1
    chunk = kc_ref.shape[1]

    @pl.when(c == 0)
    def _():
        qbd_ref[...] = _block_diag_queries(q_ref[...], A_HEADS)
        m_ref[...] = jnp.full(m_ref.shape, MASKED, F32)
        l_ref[...] = jnp.zeros(l_ref.shape, F32)
        acc_ref[...] = jnp.zeros(acc_ref.shape, F32)

    kc = kc_ref[...]
    vc = vc_ref[...]
    s = jnp.dot(qbd_ref[...], kc.astype(BF16), preferred_element_type=F32)
    _softmax_step(s + bias_ref[...], vc.astype(BF16), m_ref, l_ref, acc_ref)

    at_end = c == last
    k_after = jnp.where(at_end, knt_ref[...], knext_ref[...])
    v_after = jnp.where(at_end, vnt_ref[...], vnext_ref[...])
    ko_ref[...] = jnp.concatenate([kc, k_after], axis=1)[:, t_new:t_new + chunk]
    vo_ref[...] = jnp.concatenate([vc, v_after], axis=1)[:, t_new:t_new + chunk]

    @pl.when(at_end)
    def _():
        s_new = jnp.dot(qbd_ref[...], knt_ref[...].astype(BF16), preferred_element_type=F32)
        _softmax_step(s_new + biasn_ref[...], vnt_ref[...].astype(BF16), m_ref, l_ref, acc_ref)
        o_ref[...] = _block_diag_extract(acc_ref[...] / l_ref[...], A_HEADS)


def _sample_attention_a(q, k_new_t, v_new_t, cache_k_t, cache_v_t, bias, bias_new, *, chunk, t_new):
    nb, width, win = cache_k_t.shape
    nc = win // chunk
    rows = A_HEADS * t_new
    tiles_per_chunk = chunk // LANES
    q_spec = pl.BlockSpec((t_new, width), lambda b, c: (b, 0))
    new_spec = pl.BlockSpec((None, width, LANES), lambda b, c: (b, 0, 0))
    cache_spec = pl.BlockSpec((None, width, chunk), lambda b, c: (b, 0, c))
    next_spec = pl.BlockSpec((None, width, LANES),
                             lambda b, c: (b, 0, jnp.minimum(c + 1, nc - 1) * tiles_per_chunk))
    return pl.pallas_call(
        functools.partial(_sample_a_kernel, t_new=t_new),
        grid=(nb, nc),
        in_specs=[q_spec, new_spec, new_spec, cache_spec, cache_spec, next_spec, next_spec,
                  pl.BlockSpec((None, rows, chunk), lambda b, c: (c, 0, 0)),
                  pl.BlockSpec((rows, LANES), lambda b, c: (0, 0))],
        out_specs=[q_spec, cache_spec, cache_spec],
        out_shape=[jax.ShapeDtypeStruct((nb * t_new, width), F32),
                   jax.ShapeDtypeStruct(cache_k_t.shape, F32), jax.ShapeDtypeStruct(cache_v_t.shape, F32)],
        scratch_shapes=[pltpu.VMEM((rows, width), BF16), pltpu.VMEM((rows, 1), F32), pltpu.VMEM((rows, 1), F32),
                        pltpu.VMEM((rows, width), F32)],
        compiler_params=_params("parallel", "arbitrary"),
        name="sample_attn_a",
    )(q, k_new_t, v_new_t, cache_k_t, cache_v_t, cache_k_t, cache_v_t, bias, bias_new)


def _expand_kv(x):
    first = _pair_select(x, False)
    second = _pair_select(x, True)
    reps = B_HEADS // B_KV_HEADS // 2
    return jnp.concatenate([first] * reps + [second] * reps, axis=1)


def _sample_b_kernel(q_ref, kn_ref, vn_ref, kc_ref, vc_ref, bias_ref, sink_ref, o_ref, ko_ref, vo_ref, seq_ref):
    t_new = kn_ref.shape[0]
    win = kc_ref.shape[0]
    seq_ref[...] = jnp.zeros(seq_ref.shape, F32)
    for idx, (c_ref, n_ref, out_ref) in enumerate(((kc_ref, kn_ref, ko_ref), (vc_ref, vn_ref, vo_ref))):
        seq_ref[idx, 0:win, :] = c_ref[...]
        seq_ref[idx, win:win + t_new, :] = n_ref[...]
        out_ref[...] = seq_ref[idx, t_new:win + t_new, :]
    qbd = _block_diag_queries(q_ref[...], B_HEADS)
    k = _expand_kv(seq_ref[0]).astype(BF16)
    v = _expand_kv(seq_ref[1]).astype(BF16)
    s = lax.dot_general(qbd, k, (((1,), (1,)), ((), ())), preferred_element_type=F32) + bias_ref[...]
    sink = sink_ref[...]
    m = jnp.maximum(jnp.max(s, axis=-1, keepdims=True), sink)
    p = jnp.exp(s - m)
    den = jnp.sum(p, axis=-1, keepdims=True) + jnp.exp(sink - m)
    o = jnp.dot(p.astype(BF16), v, preferred_element_type=F32) / den
    o_ref[...] = _block_diag_extract(o, B_HEADS)


def _sample_attention_b(q, k_new, v_new, cache_k, cache_v, bias, sink_rows):
    nb, win, kvw = cache_k.shape
    t = q.shape[0] // nb
    rows = B_HEADS * t
    q_spec = pl.BlockSpec((t, B_WIDTH), lambda b: (b, 0))
    new_spec = pl.BlockSpec((t, kvw), lambda b: (b, 0))
    cache_spec = pl.BlockSpec((None, win, kvw), lambda b: (b, 0, 0))
    return pl.pallas_call(
        _sample_b_kernel,
        grid=(nb,),
        in_specs=[q_spec, new_spec, new_spec, cache_spec, cache_spec,
                  pl.BlockSpec((rows, 2 * win), lambda b: (0, 0)), pl.BlockSpec((rows, 1), lambda b: (0, 0))],
        out_specs=[q_spec, cache_spec, cache_spec],
        out_shape=[jax.ShapeDtypeStruct(q.shape, F32),
                   jax.ShapeDtypeStruct(cache_k.shape, F32), jax.ShapeDtypeStruct(cache_v.shape, F32)],
        scratch_shapes=[pltpu.VMEM((2, 2 * win, kvw), F32)],
        compiler_params=_params("parallel"),
        name="sample_attn_b",
    )(q, k_new, v_new, cache_k, cache_v, bias, sink_rows)


def _cross_attn_kernel(q_ref, k_ref, v_ref, o_ref):
    scale = MEM_HEAD_DIM ** -0.5
    for h in range(MEM_HEADS):
        cols = slice(h * MEM_HEAD_DIM, (h + 1) * MEM_HEAD_DIM)
        q = q_ref[:, cols].astype(BF16)
        if len(k_ref.shape) == 3:
            k = k_ref[:, h, :].astype(BF16)
            v = v_ref[:, h, :].astype(BF16)
        else:
            k = k_ref[:, cols].astype(BF16)
            v = v_ref[:, cols].astype(BF16)
        s = lax.dot_general(q, k, (((1,), (1,)), ((), ())), preferred_element_type=F32) * scale
        m = jnp.max(s, axis=-1, keepdims=True)
        e = jnp.exp(s - m)
        p = e / jnp.sum(e, axis=-1, keepdims=True)
        o_ref[:, cols] = jnp.dot(p.astype(BF16), v, preferred_element_type=F32)


def _cross_attention(q, mk, mv, *, tq):
    nb, s, width = q.shape
    mem = mk.shape[1]
    q_spec = pl.BlockSpec((None, tq, width), lambda b, i: (b, i, 0))
    if mk.ndim == 4:
        m_spec = pl.BlockSpec((None, mem, MEM_HEADS, MEM_HEAD_DIM), lambda b, i: (b, 0, 0, 0))
    else:
        m_spec = pl.BlockSpec((None, mem, width), lambda b, i: (b, 0, 0))
    return pl.pallas_call(
        _cross_attn_kernel,
        grid=(nb, s // tq),
        in_specs=[q_spec, m_spec, m_spec],
        out_specs=q_spec,
        out_shape=jax.ShapeDtypeStruct(q.shape, F32),
        compiler_params=_params("parallel", "arbitrary"),
        name="cross_attn",
    )(q, mk, mv)


def _top_rows(s, ids, k):
    n = s.shape[0]
    row = lax.broadcasted_iota(I32, s.shape, 0).astype(F32)
    vals, picked = [], []
    for _ in range(k):
        m = jnp.max(s, axis=0, keepdims=True)
        pos = jnp.min(jnp.where(s == m, row, float(n)), axis=0, keepdims=True)
        hit = row == pos
        vals.append(m)
        picked.append(pos if ids is None else jnp.max(jnp.where(hit, ids, -1.0), axis=0, keepdims=True))
        s = jnp.where(hit, -jnp.inf, s)
    return jnp.concatenate(vals, axis=0), jnp.concatenate(picked, axis=0)


def _peer_topk_kernel(q_ref, k1_ref, k2_ref, eidx_ref, gate_ref, *, heads_per_step):
    for h in range(heads_per_step):
        eidx, gate = _peer_topk_head(q_ref[:, h * PEER_DKEY:(h + 1) * PEER_DKEY], k1_ref[...], k2_ref[...])
        gate_ref[h * PEER_TOPK:(h + 1) * PEER_TOPK, :] = gate
        eidx_ref[h * PEER_TOPK:(h + 1) * PEER_TOPK, :] = eidx


def _peer_topk_head(q, k1, k2):
    half = PEER_DKEY // 2
    tb = q.shape[0]
    nt = (((1,), (1,)), ((), ()))
    s1 = lax.dot_general(k1, q[:, :half].astype(BF16), nt, preferred_element_type=F32)
    s2 = lax.dot_general(k2, q[:, half:].astype(BF16), nt, preferred_element_type=F32)
    v1, i1 = _top_rows(s1, None, PEER_TOPK)
    v2, i2 = _top_rows(s2, None, PEER_TOPK)
    counts = [PEER_TOPK // (a + 1) for a in range(PEER_TOPK)]
    pad = -sum(counts) % SUBLANES

    def per_a(x, fill):
        rows = [jnp.broadcast_to(x[a:a + 1], (counts[a], tb)) for a in range(PEER_TOPK)]
        return jnp.concatenate(rows + [jnp.full((pad, tb), fill, F32)], axis=0)

    def per_b(x):
        return jnp.concatenate([x[0:counts[a]] for a in range(PEER_TOPK)] + [jnp.zeros((pad, tb), F32)], axis=0)

    cand = per_a(v1, -jnp.inf) + per_b(v2)
    cidx = per_a(i1, 0.0) * PEER_NKEYS + per_b(i2)
    best, eidx = _top_rows(cand, cidx, PEER_TOPK)
    e = jnp.exp(best - best[0:1])
    return eidx.astype(I32), e / jnp.sum(e, axis=0, keepdims=True)


def _peer_topk(q, sub_k1, sub_k2, *, tb):
    m = q.shape[0]
    rows = PEER_HEADS * PEER_TOPK
    hps = PEER_TOPK_HEADS_PER_STEP
    key_spec = pl.BlockSpec((PEER_NKEYS, PEER_DKEY // 2), lambda i, h: (0, 0))
    out_spec = pl.BlockSpec((hps * PEER_TOPK, tb), lambda i, h: (h, i))
    return pl.pallas_call(
        functools.partial(_peer_topk_kernel, heads_per_step=hps),
        grid=(m // tb, PEER_HEADS // hps),
        in_specs=[pl.BlockSpec((tb, hps * PEER_DKEY), lambda i, h: (i, h)), key_spec, key_spec],
        out_specs=[out_spec, out_spec],
        out_shape=[jax.ShapeDtypeStruct((rows, m), I32), jax.ShapeDtypeStruct((rows, m), F32)],
        compiler_params=_params("parallel", "arbitrary"),
        name="peer_topk",
    )(q, sub_k1, sub_k2)


def _tree_sum(terms):
    while len(terms) > 1:
        terms = [terms[j] + terms[j + 1] for j in range(0, len(terms) - 1, 2)] + \
            ([terms[-1]] if len(terms) % 2 else [])
    return terms[0]


def _pack_expert_tables(expert_u, expert_v):
    ub = lax.bitcast_convert_type(expert_u.astype(BF16), jnp.uint16).astype(jnp.uint32)
    vb = lax.bitcast_convert_type(expert_v.astype(BF16), jnp.uint16).astype(jnp.uint32)
    n_exp, d = expert_u.shape
    return ((ub << 16) | vb).reshape(n_exp, d // LANES, LANES)


def _peer_expert_kernel(idx_ref, idxn_ref, gate_ref, y_ref, lnx_ref, lnf_ref, tab_hbm, o_ref, *scratch, tg, n_sel, n_slots):
    bufs, (rows_ref, cols_ref, sem) = scratch[:n_slots], scratch[n_slots:]
    i = pl.program_id(0)
    last = pl.num_programs(0) - 1
    n_tiles = bufs[0].shape[1]
    high = jnp.uint32(0xFFFF0000)
    ahead = PEER_GROUPS_AHEAD

    def slab_copy(ids, row, s, t, k):
        return pltpu.make_async_copy(tab_hbm.at[ids[row, k]], bufs[s].at[t, :, k, :], sem.at[s])

    def wait_slot(s):
        pltpu.make_async_copy(bufs[s], bufs[s], sem.at[s]).wait()

    @pl.when(i == 0)
    def _():
        for g in range(ahead):
            def prime(t, carry, g=g):
                for k in range(n_sel):
                    slab_copy(idx_ref, g * tg + t, g, t, k).start()
                return carry
            lax.fori_loop(0, tg, prime, 0)

    def group(s):
        nxt = (s + ahead) % n_slots
        ids_next = idx_ref if s + ahead < n_slots else idxn_ref
        next_row0 = nxt * tg
        rows = slice(s * tg, (s + 1) * tg)
        wait_slot(s)
        y = y_ref[rows, :]
        x = y * lax.rsqrt(jnp.mean(y * y, axis=-1, keepdims=True) + EPS) * lnx_ref[...]
        for t in range(tg):
            rows_ref[0, t] = x[t:t + 1, :]
            rows_ref[1, t] = y[t:t + 1, :]
        token = lax.broadcasted_iota(I32, (n_sel, tg), 1)
        half = n_sel // 2

        per_tile = half // n_tiles

        def request(t, lo, n):
            for k in range(lo, lo + n):
                slab_copy(ids_next, next_row0 + t, nxt, t, k).start(priority=k % 2)

        def dot_pass(t, carry):
            part = None
            for c in range(n_tiles):
                request(t, c * per_tile, per_tile)
                term = lax.bitcast_convert_type(bufs[s][t, c] & high, F32) \
                    * rows_ref[0, t, :, c * LANES:(c + 1) * LANES]
                part = term if part is None else part + term
            cols_ref[t] = jnp.broadcast_to(jnp.sum(part, axis=-1, keepdims=True), (n_sel, LANES))
            return carry

        lax.fori_loop(0, tg, dot_pass, 0)
        pre = jnp.zeros((n_sel, tg), F32)
        for t in range(tg):
            pre = jnp.where(token == t, cols_ref[t][:, 0:tg], pre)
        act = 0.5 * pre * (1.0 + lax.erf(pre * (2.0 ** -0.5)))
        w = gate_ref[s] * act
        for t in range(tg):
            cols_ref[t] = jnp.broadcast_to(w[:, t:t + 1], (n_sel, LANES))

        def mix_pass(t, carry):
            wt = cols_ref[t]
            sums = []
            for c in range(n_tiles):
                request(t, half + c * per_tile, per_tile)
                prod = lax.bitcast_convert_type(bufs[s][t, c] << 16, F32) * wt
                groups = [prod[g * SUBLANES:(g + 1) * SUBLANES, :] for g in range(n_sel // SUBLANES)]
                sums.append(jnp.sum(_tree_sum(groups), axis=0, keepdims=True))
            out = rows_ref[1, t] + jnp.concatenate(sums, axis=1)
            ms = jnp.mean(out * out, axis=-1, keepdims=True)
            rows_ref[2, t] = out * lax.rsqrt(ms + EPS) * lnf_ref[...]
            return carry

        lax.fori_loop(0, tg, mix_pass, 0)
        for t in range(tg):
            o_ref[s * tg + t:s * tg + t + 1, :] = rows_ref[2, t]

    for s in range(n_slots):
        group(s)

    @pl.when(i == last)
    def _():
        for g in range(ahead):
            wait_slot(g)


def _peer_experts(eidx, gate_cols, y, ln_ffn, ln_final, table, *, tg):
    m, d = y.shape
    n_sel = eidx.shape[1]
    n_tiles = table.shape[1]
    n_slots = PEER_SLOTS
    per_step = n_slots * tg
    assert m % per_step == 0 and n_tiles * LANES == d
    n_steps = m // per_step
    row_spec = pl.BlockSpec((per_step, d), lambda i: (i, 0))
    return pl.pallas_call(
        functools.partial(_peer_expert_kernel, tg=tg, n_sel=n_sel, n_slots=n_slots),
        grid=(n_steps,),
        in_specs=[
            pl.BlockSpec((per_step, n_sel), lambda i: (i, 0), memory_space=pltpu.SMEM),
            pl.BlockSpec((per_step, n_sel), lambda i: (jnp.minimum(i + 1, n_steps - 1), 0), memory_space=pltpu.SMEM),
            pl.BlockSpec((n_slots, n_sel, tg), lambda i: (i, 0, 0)),
            row_spec,
            pl.BlockSpec((1, d), lambda i: (0, 0)),
            pl.BlockSpec((1, d), lambda i: (0, 0)),
            pl.BlockSpec(memory_space=pl.ANY),
        ],
        out_specs=row_spec,
        out_shape=jax.ShapeDtypeStruct((m, d), F32),
        scratch_shapes=[pltpu.VMEM((tg, n_tiles, n_sel, LANES), jnp.uint32)] * n_slots + [
            pltpu.VMEM((3, tg, 1, d), F32), pltpu.VMEM((tg, n_sel, LANES), F32),
            pltpu.SemaphoreType.DMA((n_slots,))],
        compiler_params=_params("arbitrary"),
        name="peer_experts",
    )(eidx, eidx, gate_cols, y, ln_ffn.reshape(1, d), ln_final.reshape(1, d), table)


def _channel_mixers(y, mk, mv, nb, p, *, tm, tq, tb, tg):
    m, d = y.shape
    q = _norm_matmul([y], [p["ln_cross"]], p["w_cq"], tm=tm, tn=MEM_WIDTH, name="cross_q_proj")
    o = _cross_attention(q.reshape(nb, m // nb, MEM_WIDTH), mk, mv, tq=tq).reshape(m, MEM_WIDTH)
    y = _norm_matmul([o], None, p["w_co"], residual=y, tm=tm, tn=512, name="cross_out_proj")
    pq = _norm_matmul([y], [p["ln_ffn"]], p["w_pq"], tm=tm, tn=512, name="peer_query_proj")
    eidx_t, gate_t = _peer_topk(pq, p["sub_k1"], p["sub_k2"], tb=tb)
    n_sel = eidx_t.shape[0]
    gate_cols = gate_t.reshape(n_sel, m // tg, tg).transpose(1, 0, 2)
    return _peer_experts(eidx_t.T, gate_cols, y, p["ln_ffn"], p["ln_final"], p["expert_table"], tg=tg)


def kernel(x_prompt, x_sample, cache_a_k, cache_a_v, cache_b_k, cache_b_v, cache_mem_k, cache_mem_v, mem_prompt, ln_mix, w_in, ln_a_out, ln_b_out, w_out, b_sinks, rel_bias, ln_cross, ln_mem, w_cq, w_ckv, w_co, ln_ffn, w_pq, sub_keys_1, sub_keys_2, expert_u, expert_v, ln_final):
    depth = w_in.shape[0]
    assert depth == 1, "the caches are laid out for a single layer"
    batch, seq, d = x_prompt.shape
    dec_batch, dec_seq, _ = x_sample.shape
    a_win = cache_a_k.shape[2]
    b_win = cache_b_k.shape[2]
    mem_len = mem_prompt.shape[1]
    assert seq == a_win == A_WIN and b_win == B_WIN and seq % BLK == 0
    l = 0
    split = 3 * A_WIDTH + B_WIDTH
    w_in_main = w_in[l, :, :split].astype(BF16)
    w_in_kvb = w_in[l, :, split:].astype(BF16)
    p = dict(ln_cross=ln_cross[l], w_cq=w_cq[l].astype(BF16), w_co=w_co[l].astype(BF16), ln_ffn=ln_ffn[l],
             w_pq=w_pq[l].astype(BF16), sub_k1=sub_keys_1[l].astype(BF16), sub_k2=sub_keys_2[l].astype(BF16),
             ln_final=ln_final, expert_table=_pack_expert_tables(expert_u[l], expert_v[l]))
    w_out_bf = w_out[l].astype(BF16)
    w_ckv_bf = w_ckv[l].astype(BF16)
    sinks = b_sinks[l].astype(F32)

    def project(x2d, tm):
        main = _norm_matmul([x2d], [ln_mix[l]], w_in_main, tm=tm, tn=512, out_split=4, name="in_proj")
        kvb = _norm_matmul([x2d], [ln_mix[l]], w_in_kvb, tm=tm, tn=B_KV_WIDTH, out_split=2, name="in_proj_kvb")
        return main, kvb

    def merge(oa, ob, resid, tm):
        return _norm_matmul([oa, ob], [ln_a_out[l], ln_b_out[l]], w_out_bf, residual=resid, tm=tm, tn=512,
                            name="mixer_out_proj")

    xp = x_prompt.reshape(batch * seq, d)
    main, kvb = project(xp, PROMPT_ROW_TILE)
    n_delta_a = seq // BLK + 1
    bias_a = _prompt_bias_tiles(rel_bias, 0, _mixer_a_multiplicity, n_delta_a)
    window_b = lambda dist: ((dist >= 0) & (dist <= B_WIN)).astype(np.int32)
    bias_b = _prompt_bias_tiles(rel_bias, A_HEADS, window_b, B_WIN // BLK + 1)
    oa = _prompt_attention((main, 0), (main, 1), (main, 2), bias_a, None, batch=batch, seq=seq, gqa_pairs=0,
                           chunk_tiles=8, q_tiles=PROMPT_Q_TILES)
    ob = _prompt_attention((main, 3), (kvb, 0), (kvb, 1), bias_b, sinks, batch=batch, seq=seq,
                           gqa_pairs=B_HEADS // B_KV_HEADS // 2, chunk_tiles=B_WIN // BLK + PROMPT_Q_TILES,
                           q_tiles=PROMPT_Q_TILES)
    yp = merge(oa, ob, xp, PROMPT_ROW_TILE)
    mem_kv = _norm_matmul([mem_prompt.reshape(batch * mem_len, d)], [ln_mem[l]], w_ckv_bf, tm=512, tn=MEM_WIDTH,
                          out_split=2, name="mem_kv_proj")
    mk = mem_kv[0].reshape(batch, mem_len, MEM_WIDTH)
    mv = mem_kv[1].reshape(batch, mem_len, MEM_WIDTH)
    y_prompt = _channel_mixers(yp, mk, mv, batch, p, tm=PROMPT_ROW_TILE, tq=512, tb=128, tg=8)

    xs = x_sample.reshape(dec_batch * dec_seq, d)
    main_s, kvb_s = project(xs, 512)
    chunk = 512
    key_pos = np.arange(a_win)
    t_pos = np.arange(dec_seq)
    dist_cache = a_win + t_pos[:, None] - key_pos[None, :]
    bias_sa = _distance_bias(rel_bias, dist_cache, 0, A_HEADS, _mixer_a_multiplicity(dist_cache))
    bias_sa = bias_sa.reshape(A_HEADS * dec_seq, a_win // chunk, chunk).transpose(1, 0, 2)
    dist_new = t_pos[:, None] - np.arange(LANES)[None, :]
    mult_new = np.where(np.arange(LANES)[None, :] < dec_seq, _mixer_a_multiplicity(dist_new), 0)
    bias_sa_new = _distance_bias(rel_bias, dist_new, 0, A_HEADS, mult_new).reshape(A_HEADS * dec_seq, LANES)
    def positions_minor(x, n_pos):
        return x.reshape(dec_batch, n_pos, A_WIDTH).transpose(0, 2, 1)

    def new_tokens_tile(x):
        return jnp.pad(positions_minor(x, dec_seq), ((0, 0), (0, 0), (0, LANES - dec_seq)))

    oa_s, aks_t, avs_t = _sample_attention_a(
        main_s[0], new_tokens_tile(main_s[1]), new_tokens_tile(main_s[2]),
        positions_minor(cache_a_k[l], a_win), positions_minor(cache_a_v[l], a_win),
        bias_sa, bias_sa_new, chunk=chunk, t_new=dec_seq)
    aks = aks_t.transpose(0, 2, 1)
    avs = avs_t.transpose(0, 2, 1)
    seq_pos = np.arange(2 * b_win)
    dist_b = b_win + t_pos[:, None] - seq_pos[None, :]
    mult_b = ((dist_b >= 0) & (dist_b <= B_WIN) & (seq_pos[None, :] < b_win + dec_seq)).astype(np.int32)
    bias_sb = _distance_bias(rel_bias, dist_b, A_HEADS, B_HEADS, mult_b).reshape(B_HEADS * dec_seq, 2 * b_win)
    sink_rows = jnp.repeat(sinks, dec_seq).reshape(B_HEADS * dec_seq, 1)
    ob_s, bks, bvs = _sample_attention_b(
        main_s[3], kvb_s[0], kvb_s[1],
        cache_b_k[l].reshape(dec_batch, b_win, B_KV_WIDTH), cache_b_v[l].reshape(dec_batch, b_win, B_KV_WIDTH),
        bias_sb, sink_rows)
    ys = merge(oa_s, ob_s, xs, 512)
    mk_s = cache_mem_k[l]
    mv_s = cache_mem_v[l]
    y_sample = _channel_mixers(ys, mk_s, mv_s, dec_batch, p, tm=512, tq=dec_seq, tb=128, tg=8)

    def heads(x, *shape):
        return x.reshape(1, *shape)

    return (y_prompt.reshape(batch, seq, d), y_sample.reshape(dec_batch, dec_seq, d),
            heads(main[1], batch, seq, A_HEADS, HEAD_DIM), heads(main[2], batch, seq, A_HEADS, HEAD_DIM),
            heads(kvb[0].reshape(batch, seq, B_KV_WIDTH)[:, seq - b_win:], batch, b_win, B_KV_HEADS, HEAD_DIM),
            heads(kvb[1].reshape(batch, seq, B_KV_WIDTH)[:, seq - b_win:], batch, b_win, B_KV_HEADS, HEAD_DIM),
            heads(mk, batch, mem_len, MEM_HEADS, MEM_HEAD_DIM), heads(mv, batch, mem_len, MEM_HEADS, MEM_HEAD_DIM),
            heads(aks, dec_batch, a_win, A_HEADS, HEAD_DIM), heads(avs, dec_batch, a_win, A_HEADS, HEAD_DIM),
            heads(bks, dec_batch, b_win, B_KV_HEADS, HEAD_DIM), heads(bvs, dec_batch, b_win, B_KV_HEADS, HEAD_DIM))
```
